```python
import jax, jax.numpy as jnp
from jax import lax
import numpy as np

D_MODEL = 1024
BATCH = 8
SEQ = 2048
DEPTH = 2

HEAD_DIM = 64
RMS_EPS = 1e-6
Q_BLOCK = 128
SPARSE_Q_BLOCK = 64
MLA_HEADS = 8
MLA_NOPE = 64
MLA_ROPE = 32
MLA_QK = MLA_NOPE + MLA_ROPE
MLA_V = 64
MLA_Q_RANK = 768
MLA_KV_RANK = 256
ROPE_THETA = 10000.0
MOBA_HEADS = 4
MOBA_BLOCK = 256
MOBA_TOPK = 3
DSA_HEADS = 4
IDX_HEADS = 16
IDX_DIM = 32
DSA_TOPK = 256
D_FF = -(-8 * D_MODEL // (3 * 256)) * 256
IN_WIDTHS = (MLA_Q_RANK, MLA_KV_RANK, MLA_ROPE,
             3 * MOBA_HEADS * HEAD_DIM, 3 * DSA_HEADS * HEAD_DIM,
             IDX_HEADS * IDX_DIM, IDX_DIM, IDX_HEADS, 3 * D_MODEL)
N_IN = sum(IN_WIDTHS)

kernel_name = "hybrid_mla_moba_dsa_gated_block"


def rms_norm(x, g, eps=RMS_EPS):
    xf = x.astype(jnp.float32)
    y = xf * lax.rsqrt(jnp.mean(xf * xf, axis=-1, keepdims=True) + eps)
    return (y * g.astype(jnp.float32)).astype(x.dtype)


def alibi_slopes(n):
    return jnp.asarray([2.0 ** (-8.0 * (i + 1) / n) for i in range(n)], dtype=jnp.float32)


def apply_rope(x, pos):
    half = x.shape[-1] // 2
    freqs = ROPE_THETA ** (-jnp.arange(half, dtype=jnp.float32) / half)
    ang = pos.astype(jnp.float32)[:, None] * freqs[None, :]
    cos, sin = jnp.cos(ang), jnp.sin(ang)
    x1 = x[..., :half].astype(jnp.float32)
    x2 = x[..., half:].astype(jnp.float32)
    return jnp.concatenate([x1 * cos - x2 * sin, x1 * sin + x2 * cos], -1).astype(x.dtype)


def split_columns(proj):
    outs, start = [], 0
    for wdt in IN_WIDTHS:
        outs.append(proj[..., start:start + wdt])
        start += wdt
    return outs


def dense_causal_attention(q, k, v):
    B, H, S, Dqk = q.shape
    scale = Dqk ** -0.5
    kpos = jnp.arange(S)

    def one(i):
        start = i * Q_BLOCK
        qi = lax.dynamic_slice_in_dim(q, start, Q_BLOCK, axis=2)
        qpos = start + jnp.arange(Q_BLOCK)
        s = jnp.einsum('bhqd,bhkd->bhqk', qi, k, preferred_element_type=jnp.float32) * scale
        s = jnp.where(kpos[None, :] <= qpos[:, None], s, -jnp.inf)
        p = jax.nn.softmax(s, axis=-1).astype(v.dtype)
        return jnp.einsum('bhqk,bhkd->bhqd', p, v)

    out = lax.map(one, jnp.arange(S // Q_BLOCK))
    return out.transpose(1, 2, 0, 3, 4).reshape(B, H, S, v.shape[-1])


def moba_attention(q, k, v, slopes):
    B, H, S, Dh = q.shape
    nb = -(-S // MOBA_BLOCK)
    pad = nb * MOBA_BLOCK - S
    k_pad = jnp.pad(k, ((0, 0), (0, 0), (0, pad), (0, 0)))
    v_pad = jnp.pad(v, ((0, 0), (0, 0), (0, pad), (0, 0)))
    kb = k_pad.reshape(B, H, nb, MOBA_BLOCK, Dh)
    vb = v_pad.reshape(B, H, nb, MOBA_BLOCK, Dh)
    k_mean = jnp.mean(kb.astype(jnp.float32), axis=3)
    n_sel = max(1, min(MOBA_TOPK, nb - 1))
    scale = Dh ** -0.5
    slope = slopes[None, :, None, None]
    bi = jnp.arange(B)[:, None, None, None]
    hi = jnp.arange(H)[None, :, None, None]
    blk_off = jnp.arange(MOBA_BLOCK)
    n_keys_sel = n_sel * MOBA_BLOCK

    def one(i):
        start = i * SPARSE_Q_BLOCK
        qi = lax.dynamic_slice_in_dim(q, start, SPARSE_Q_BLOCK, axis=2)
        qpos = start + jnp.arange(SPARSE_Q_BLOCK)
        own = start // MOBA_BLOCK
        k_own = lax.dynamic_slice_in_dim(k_pad, own * MOBA_BLOCK, MOBA_BLOCK, axis=2)
        v_own = lax.dynamic_slice_in_dim(v_pad, own * MOBA_BLOCK, MOBA_BLOCK, axis=2)
        kpos_own = own * MOBA_BLOCK + blk_off
        s_own = (jnp.einsum('bhqd,bhkd->bhqk', qi, k_own, preferred_element_type=jnp.float32) * scale
                 - slope * (qpos[:, None] - kpos_own[None, :]).astype(jnp.float32))
        s_own = jnp.where(kpos_own[None, :] <= qpos[:, None], s_own, -jnp.inf)
        gate = jnp.einsum('bhqd,bhnd->bhqn', qi.astype(jnp.float32), k_mean)
        gate = jnp.where(jnp.arange(nb) < own, gate, -jnp.inf)
        _, idx = lax.top_k(gate, n_sel)
        valid = jnp.repeat(idx < own, MOBA_BLOCK, axis=-1)
        k_sel = kb[bi, hi, idx].reshape(B, H, SPARSE_Q_BLOCK, n_keys_sel, Dh)
        v_sel = vb[bi, hi, idx].reshape(B, H, SPARSE_Q_BLOCK, n_keys_sel, Dh)
        kpos_sel = (idx[..., None] * MOBA_BLOCK + blk_off).reshape(B, H, SPARSE_Q_BLOCK, n_keys_sel)
        s_sel = (jnp.einsum('bhqd,bhqkd->bhqk', qi, k_sel, preferred_element_type=jnp.float32) * scale
                 - slope * (qpos[:, None] - kpos_sel).astype(jnp.float32))
        s_sel = jnp.where(valid, s_sel, -jnp.inf)
        p = jax.nn.softmax(jnp.concatenate([s_sel, s_own], axis=-1), axis=-1).astype(v.dtype)
        return (jnp.einsum('bhqk,bhqkd->bhqd', p[..., :n_keys_sel], v_sel)
                + jnp.einsum('bhqk,bhkd->bhqd', p[..., n_keys_sel:], v_own))

    out = lax.map(one, jnp.arange(S // SPARSE_Q_BLOCK))
    return out.transpose(1, 2, 0, 3, 4).reshape(B, H, S, Dh)


def dsa_attention(q, k, v, iq, ik, iw, slopes):
    B, H, S, Dh = q.shape
    n_keep = min(DSA_TOPK, S // 4)
    scale = Dh ** -0.5
    idx_scale = (IDX_DIM * IDX_HEADS) ** -0.5
    slope = slopes[None, :, None, None]
    kpos = jnp.arange(S)
    bi = jnp.arange(B)[:, None, None]

    def one(i):
        start = i * SPARSE_Q_BLOCK
        qpos = start + jnp.arange(SPARSE_Q_BLOCK)
        qi = lax.dynamic_slice_in_dim(q, start, SPARSE_Q_BLOCK, axis=2)
        iqi = lax.dynamic_slice_in_dim(iq, start, SPARSE_Q_BLOCK, axis=1)
        iwi = lax.dynamic_slice_in_dim(iw, start, SPARSE_Q_BLOCK, axis=1)
        dots = jnp.einsum('bqhd,bsd->bqhs', iqi, ik, preferred_element_type=jnp.float32)
        score = jnp.einsum('bqhs,bqh->bqs', jax.nn.relu(dots), iwi.astype(jnp.float32)) * idx_scale
        score = jnp.where(kpos[None, None, :] <= qpos[None, :, None], score, -jnp.inf)
        _, idx = lax.top_k(score, n_keep)
        valid = idx <= qpos[None, :, None]
        k_sel = k[bi, idx]
        v_sel = v[bi, idx]
        s = (jnp.einsum('bhqd,bqkhd->bhqk', qi, k_sel, preferred_element_type=jnp.float32) * scale
             - slope * (qpos[None, None, :, None] - idx[:, None]).astype(jnp.float32))
        s = jnp.where(valid[:, None], s, -jnp.inf)
        p = jax.nn.softmax(s, axis=-1).astype(v.dtype)
        return jnp.einsum('bhqk,bqkhd->bhqd', p, v_sel)

    out = lax.map(one, jnp.arange(S // SPARSE_Q_BLOCK))
    return out.transpose(1, 2, 0, 3, 4).reshape(B, H, S, Dh)


def token_mixer(h, w_in, g_cq, w_uq, g_ckv, w_ukv, qn_mla, kn_mla, qn_moba, kn_moba,
                qn_dsa, kn_dsa, w_br_a, w_br_b, w_br_c, w_out):
    B, S, _ = h.shape
    pos = jnp.arange(S)
    slopes = alibi_slopes(MOBA_HEADS + DSA_HEADS)
    proj = jnp.einsum('bsd,dn->bsn', h, w_in)
    cq, ckv, kr, qkv_b, qkv_c, iq, ik, iw, gate_logits = split_columns(proj)

    q_a = jnp.einsum('bsr,rn->bsn', rms_norm(cq, g_cq), w_uq).reshape(B, S, MLA_HEADS, MLA_QK)
    kv_a = jnp.einsum('bsr,rn->bsn', rms_norm(ckv, g_ckv), w_ukv).reshape(B, S, MLA_HEADS, MLA_NOPE + MLA_V)
    k_a = jnp.concatenate(
        [kv_a[..., :MLA_NOPE], jnp.broadcast_to(kr[:, :, None, :], (B, S, MLA_HEADS, MLA_ROPE))], axis=-1)
    v_a = kv_a[..., MLA_NOPE:].transpose(0, 2, 1, 3)
    q_a = rms_norm(q_a, qn_mla).transpose(0, 2, 1, 3)
    k_a = rms_norm(k_a, kn_mla).transpose(0, 2, 1, 3)
    q_a = jnp.concatenate([q_a[..., :MLA_NOPE], apply_rope(q_a[..., MLA_NOPE:], pos)], axis=-1)
    k_a = jnp.concatenate([k_a[..., :MLA_NOPE], apply_rope(k_a[..., MLA_NOPE:], pos)], axis=-1)
    y_a = dense_causal_attention(q_a, k_a, v_a).transpose(0, 2, 1, 3).reshape(B, S, MLA_HEADS * MLA_V)
    y_a = jnp.einsum('bsk,kd->bsd', y_a, w_br_a)

    qkv_b = qkv_b.reshape(B, S, 3, MOBA_HEADS, HEAD_DIM)
    q_b = rms_norm(qkv_b[:, :, 0], qn_moba).transpose(0, 2, 1, 3)
    k_b = rms_norm(qkv_b[:, :, 1], kn_moba).transpose(0, 2, 1, 3)
    v_b = qkv_b[:, :, 2].transpose(0, 2, 1, 3)
    y_b = moba_attention(q_b, k_b, v_b, slopes[0::2]).transpose(0, 2, 1, 3).reshape(B, S, MOBA_HEADS * HEAD_DIM)
    y_b = jnp.einsum('bsk,kd->bsd', y_b, w_br_b)

    qkv_c = qkv_c.reshape(B, S, 3, DSA_HEADS, HEAD_DIM)
    q_c = rms_norm(qkv_c[:, :, 0], qn_dsa).transpose(0, 2, 1, 3)
    k_c = rms_norm(qkv_c[:, :, 1], kn_dsa)
    v_c = qkv_c[:, :, 2]
    y_c = dsa_attention(q_c, k_c, v_c, iq.reshape(B, S, IDX_HEADS, IDX_DIM), ik, iw, slopes[1::2])
    y_c = jnp.einsum('bsk,kd->bsd', y_c.transpose(0, 2, 1, 3).reshape(B, S, DSA_HEADS * HEAD_DIM), w_br_c)

    g_a, g_b, g_c = jnp.split(jax.nn.sigmoid(gate_logits), 3, axis=-1)
    merged = g_a * y_a + g_b * y_b + g_c * y_c
    return jnp.einsum('bsd,de->bse', merged, w_out)


def swiglu(h, w_gu, w_down):
    gu = jnp.einsum('bsd,df->bsf', h, w_gu)
    g, u = jnp.split(gu, 2, axis=-1)
    return jnp.einsum('bsf,fd->bsd', jax.nn.silu(g) * u, w_down)


def setup_inputs(seed: int = 0) -> dict:
    key = jax.random.key(seed)
    ks = iter(jax.random.split(key, 32))
    L, D = DEPTH, D_MODEL

    def w(shape, fan_in):
        return jax.random.normal(next(ks), shape, jnp.float32) * fan_in ** -0.5

    def gain(n):
        return 1.0 + 0.02 * jax.random.normal(next(ks), (L, n), jnp.float32)

    return {
        "x": jax.random.normal(next(ks), (BATCH, SEQ, D), jnp.float32),
        "c": jax.random.normal(next(ks), (BATCH, D), jnp.float32),
        "w_ada": w((L, D, 6 * D), D),
        "b_ada": 0.02 * jax.random.normal(next(ks), (L, 6 * D), jnp.float32),
        "norm1": gain(D),
        "w_in": w((L, D, N_IN), D),
        "g_cq": gain(MLA_Q_RANK),
        "w_uq": w((L, MLA_Q_RANK, MLA_HEADS * MLA_QK), MLA_Q_RANK),
        "g_ckv": gain(MLA_KV_RANK),
        "w_ukv": w((L, MLA_KV_RANK, MLA_HEADS * (MLA_NOPE + MLA_V)), MLA_KV_RANK),
        "qn_mla": gain(MLA_QK),
        "kn_mla": gain(MLA_QK),
        "qn_moba": gain(HEAD_DIM),
        "kn_moba": gain(HEAD_DIM),
        "qn_dsa": gain(HEAD_DIM),
        "kn_dsa": gain(HEAD_DIM),
        "w_br_a": w((L, MLA_HEADS * MLA_V, D), MLA_HEADS * MLA_V),
        "w_br_b": w((L, MOBA_HEADS * HEAD_DIM, D), MOBA_HEADS * HEAD_DIM),
        "w_br_c": w((L, DSA_HEADS * HEAD_DIM, D), DSA_HEADS * HEAD_DIM),
        "w_out": w((L, D, D), D),
        "norm2": gain(D),
        "w_gu": w((L, D, 2 * D_FF), D),
        "w_down": w((L, D_FF, D), D_FF),
    }


def reference(x, c, w_ada, b_ada, norm1, w_in, g_cq, w_uq, g_ckv, w_ukv, qn_mla, kn_mla,
              qn_moba, kn_moba, qn_dsa, kn_dsa, w_br_a, w_br_b, w_br_c, w_out, norm2, w_gu, w_down):
    cond = jax.nn.silu(c)
    for l in range(DEPTH):
        mod = jnp.einsum('bd,dn->bn', cond, w_ada[l]) + b_ada[l]
        sh1, sc1, g1, sh2, sc2, g2 = [m[:, None, :] for m in jnp.split(mod, 6, axis=-1)]
        h = rms_norm(x, norm1[l]) * (1.0 + sc1) + sh1
        x = x + g1 * token_mixer(h, w_in[l], g_cq[l], w_uq[l], g_ckv[l], w_ukv[l], qn_mla[l], kn_mla[l],
                                 qn_moba[l], kn_moba[l], qn_dsa[l], kn_dsa[l],
                                 w_br_a[l], w_br_b[l], w_br_c[l], w_out[l])
        h = rms_norm(x, norm2[l]) * (1.0 + sc2) + sh2
        x = x + g2 * swiglu(h, w_gu[l], w_down[l])
    return x
```

```python
import functools

import jax
import jax.numpy as jnp
from jax import lax
from jax.experimental import pallas as pl
from jax.experimental.pallas import tpu as pltpu

F32 = jnp.float32
BF16 = jnp.bfloat16

D_MODEL = 1024
HEAD_DIM = 64
RMS_EPS = 1e-6
MLA_HEADS = 8
MLA_NOPE = 64
MLA_ROPE = 32
MLA_QK = MLA_NOPE + MLA_ROPE
MLA_V = 64
MLA_Q_RANK = 768
MLA_KV_RANK = 256
ROPE_THETA = 10000.0
MOBA_HEADS = 4
MOBA_BLOCK = 256
MOBA_TOPK = 3
DSA_HEADS = 4
IDX_HEADS = 16
IDX_DIM = 32
DSA_TOPK = 256
D_FF = 2816

LANES = 128
VMEM_LIMIT = 56 * 1024 * 1024

OFF_CQ = 0
OFF_CKV = OFF_CQ + MLA_Q_RANK
OFF_KR = OFF_CKV + MLA_KV_RANK
OFF_B = OFF_KR + LANES
OFF_C = OFF_B + 3 * MOBA_HEADS * HEAD_DIM
OFF_IQ = OFF_C + 3 * DSA_HEADS * HEAD_DIM
OFF_IK = OFF_IQ + IDX_HEADS * IDX_DIM
OFF_IW = OFF_IK + LANES
OFF_G = OFF_IW + LANES
N_ALL = OFF_G + 3 * D_MODEL


def _nt(a, b):
    return lax.dot_general(a, b, (((1,), (1,)), ((), ())), preferred_element_type=F32)


def _mm(a, b):
    return jnp.dot(a, b, preferred_element_type=F32)


def _rms(x, g):
    return x * lax.rsqrt(jnp.mean(x * x, axis=-1, keepdims=True) + RMS_EPS) * g


def _const_spec(shape):
    nd = len(shape)
    return pl.BlockSpec(shape, lambda *_: (0,) * nd, pipeline_mode=pl.Buffered(1))


def _params(*sem):
    return pltpu.CompilerParams(dimension_semantics=sem, vmem_limit_bytes=VMEM_LIMIT)


def _ada_kernel(c_ref, w_ref, b_ref, o_ref):
    c = c_ref[...]
    cond = (c * jax.nn.sigmoid(c)).astype(BF16)
    o_ref[0] = _mm(cond, w_ref[0]) + b_ref[0]


def _ada(c, w_ada, b_ada):
    L, D, N = w_ada.shape
    B = c.shape[0]
    tn = 1024
    return pl.pallas_call(
        _ada_kernel,
        grid=(L, N // tn),
        in_specs=[
            pl.BlockSpec((B, D), lambda l, n: (0, 0)),
            pl.BlockSpec((1, D, tn), lambda l, n: (l, 0, n)),
            pl.BlockSpec((1, 1, tn), lambda l, n: (l, 0, n)),
        ],
        out_specs=pl.BlockSpec((1, B, tn), lambda l, n: (l, 0, n)),
        out_shape=jax.ShapeDtypeStruct((L, B, N), F32),
        compiler_params=_params("arbitrary", "arbitrary"),
        name="ada_mod",
    )(c, w_ada, b_ada.reshape(L, 1, N))


def _rope(x, c, s1, s2):
    return x * c + pltpu.roll(x, LANES - 16, 1) * s1 + pltpu.roll(x, 16, 1) * s2


def _inproj_kernel(x_ref, mod_ref, n1_ref, w_ref, gcq_ref, wuq_ref, gckv_ref, wk_ref, wv_ref,
                   qna_ref, kna_ref, rc_ref, rs1_ref, rs2_ref, bd_ref,
                   qnb_ref, knb_ref, qnc_ref, knc_ref,
                   qa_ref, ka_ref, va_ref, qb_ref, kb_ref, vb_ref, qc_ref, kc_ref, vc_ref,
                   iq_ref, ik_ref, iw_ref, gt_ref):
    x = x_ref[0]
    sh1 = mod_ref[0, 0:1, :]
    sc1 = mod_ref[0, 1:2, :]
    hb = (_rms(x, n1_ref[...]) * (1.0 + sc1) + sh1).astype(BF16)

    a = _mm(hb, w_ref[:, OFF_CQ:OFF_B])
    cqn = _rms(a[:, OFF_CQ:OFF_CKV], gcq_ref[...]).astype(BF16)
    ckvn = _rms(a[:, OFF_CKV:OFF_KR], gckv_ref[...]).astype(BF16)
    kr = a[:, OFF_KR:OFF_B]
    qraw = _mm(cqn, wuq_ref[...])
    kraw = _mm(ckvn, wk_ref[...])
    va_ref[0] = _mm(ckvn, wv_ref[...]).astype(BF16)
    rc, rs1, rs2 = rc_ref[...], rs1_ref[...], rs2_ref[...]
    qna, kna = qna_ref[...], kna_ref[...]
    inv_qk = 1.0 / MLA_QK
    for h in range(MLA_HEADS):
        sl = slice(LANES * h, LANES * (h + 1))
        qh = qraw[:, sl]
        qh = qh * lax.rsqrt(jnp.sum(qh * qh, axis=-1, keepdims=True) * inv_qk + RMS_EPS) * qna
        qa_ref[0, :, sl] = _rope(qh, rc, rs1, rs2).astype(BF16)
        kh = kraw[:, sl] + kr
        kh = kh * lax.rsqrt(jnp.sum(kh * kh, axis=-1, keepdims=True) * inv_qk + RMS_EPS) * kna
        ka_ref[0, :, sl] = _rope(kh, rc, rs1, rs2).astype(BF16)

    bd = bd_ref[...]

    def segnorm(v, g):
        ss = jnp.dot(v * v, bd, preferred_element_type=F32, precision=lax.Precision.HIGHEST)
        return v * lax.rsqrt(ss * (1.0 / HEAD_DIM) + RMS_EPS) * g

    hw = MOBA_HEADS * HEAD_DIM
    pb = _mm(hb, w_ref[:, OFF_B:OFF_C])
    qb_ref[0] = segnorm(pb[:, 0:hw], qnb_ref[...]).astype(BF16)
    kb_ref[0] = segnorm(pb[:, hw:2 * hw], knb_ref[...]).astype(BF16)
    vb_ref[0] = pb[:, 2 * hw:3 * hw].astype(BF16)
    pc = _mm(hb, w_ref[:, OFF_C:OFF_IQ])
    qc_ref[0] = segnorm(pc[:, 0:hw], qnc_ref[...]).astype(BF16)
    kc_ref[0] = segnorm(pc[:, hw:2 * hw], knc_ref[...]).astype(BF16)
    vc_ref[0] = pc[:, 2 * hw:3 * hw].astype(BF16)

    pi = _mm(hb, w_ref[:, OFF_IQ:OFF_G])
    iq_ref[0] = pi[:, 0:OFF_IK - OFF_IQ].astype(BF16)
    ik_ref[0] = pi[:, OFF_IK - OFF_IQ:OFF_IW - OFF_IQ].astype(BF16)
    iw_ref[0] = pi[:, OFF_IW - OFF_IQ:OFF_G - OFF_IQ]
    gt_ref[0] = jax.nn.sigmoid(_mm(hb, w_ref[:, OFF_G:N_ALL])).astype(BF16)


def _inproj(x, mod, lw, rope_tabs, bd, tm):
    B, S, D = x.shape
    hw = MOBA_HEADS * HEAD_DIM
    row = lambda w: pl.BlockSpec((1, tm, w), lambda b, i: (b, i, 0))
    tab = pl.BlockSpec((tm, LANES), lambda b, i: (i, 0))
    consts = [lw["n1"], lw["w_all"], lw["g_cq"], lw["w_uq"], lw["g_ckv"], lw["w_k"], lw["w_v"],
              lw["qn_a"], lw["kn_a"]]
    consts2 = [bd, lw["qn_b"], lw["kn_b"], lw["qn_c"], lw["kn_c"]]
    widths = [MLA_HEADS * LANES, MLA_HEADS * LANES, MLA_HEADS * MLA_V, hw, hw, hw, hw, hw, hw,
              IDX_HEADS * IDX_DIM, LANES, LANES, 3 * D_MODEL]
    dtypes = [BF16] * 11 + [F32, BF16]
    return pl.pallas_call(
        _inproj_kernel,
        grid=(B, S // tm),
        in_specs=[row(D), pl.BlockSpec((1, 6, D), lambda b, i: (b, 0, 0))]
        + [_const_spec(c.shape) for c in consts] + [tab, tab, tab]
        + [_const_spec(c.shape) for c in consts2],
        out_specs=[row(w) for w in widths],
        out_shape=[jax.ShapeDtypeStruct((B, S, w), dt) for w, dt in zip(widths, dtypes)],
        compiler_params=_params("arbitrary", "arbitrary"),
        name="in_proj",
    )(x, mod, *consts, *rope_tabs, *consts2)


def _mla_kernel(q_ref, k_ref, v_ref, o_ref, *, tq, scale):
    i = pl.program_id(2)
    row = lax.broadcasted_iota(jnp.int32, (tq, tq), 0)
    col = lax.broadcasted_iota(jnp.int32, (tq, tq), 1)
    causal = col <= row
    lane = lax.broadcasted_iota(jnp.int32, (tq, LANES), 1)
    start_d = pl.multiple_of(i * tq, tq)
    v_d = v_ref[0, pl.ds(start_d, tq), :]
    outs = []
    for hh in range(2):
        sl = slice(LANES * hh, LANES * (hh + 1))
        q = q_ref[0, :, sl]
        s = _nt(q, k_ref[0, pl.ds(start_d, tq), sl]) * scale
        s = jnp.where(causal, s, -jnp.inf)
        m = jnp.max(s, axis=-1, keepdims=True)
        p = jnp.exp(s - m)
        l = jnp.sum(p, axis=-1, keepdims=True)
        acc = _mm(p.astype(BF16), v_d)

        def body(j, carry, q=q, sl=sl):
            m, l, acc = carry
            st = pl.multiple_of(j * tq, tq)
            s = _nt(q, k_ref[0, pl.ds(st, tq), sl]) * scale
            m_new = jnp.maximum(m, jnp.max(s, axis=-1, keepdims=True))
            alpha = jnp.exp(m - m_new)
            p = jnp.exp(s - m_new)
            l = alpha * l + jnp.sum(p, axis=-1, keepdims=True)
            acc = alpha * acc + _mm(p.astype(BF16), v_ref[0, pl.ds(st, tq), :])
            return m_new, l, acc

        m, l, acc = lax.fori_loop(0, i, body, (m, l, acc))
        outs.append(acc / l)
    o_ref[0] = jnp.where(lane < MLA_V, outs[0], outs[1]).astype(BF16)


def _mla_attention(qa, ka, va, tq):
    B, S, _ = qa.shape
    pairs = MLA_HEADS // 2
    return pl.pallas_call(
        functools.partial(_mla_kernel, tq=tq, scale=MLA_QK ** -0.5),
        grid=(B, pairs, S // tq),
        in_specs=[
            pl.BlockSpec((1, tq, 2 * LANES), lambda b, h, i: (b, i, h)),
            pl.BlockSpec((1, S, 2 * LANES), lambda b, h, i: (b, 0, h)),
            pl.BlockSpec((1, S, LANES), lambda b, h, i: (b, 0, h)),
        ],
        out_specs=pl.BlockSpec((1, tq, LANES), lambda b, h, i: (b, i, h)),
        out_shape=jax.ShapeDtypeStruct((B, S, MLA_HEADS * MLA_V), BF16),
        compiler_params=_params("arbitrary", "arbitrary", "arbitrary"),
        name="mla_attn",
    )(qa, ka, va)


def _moba_kernel(q_ref, k_ref, v_ref, o_ref, km_ref, *, nb, n_sel, slopes, scale):
    i = pl.program_id(1)
    blk = MOBA_BLOCK
    hw = MOBA_HEADS * HEAD_DIM
    nbp = km_ref.shape[0]

    @pl.when(i == 0)
    def _():
        km_ref[...] = jnp.zeros_like(km_ref)
        for n in range(nb):
            kb = k_ref[0, n * blk:(n + 1) * blk, :].astype(F32)
            km_ref[n:n + 1, :] = jnp.mean(kb, axis=0, keepdims=True)

    q = q_ref[0]
    qf = q.astype(F32)
    km = km_ref[...]
    lane = lax.broadcasted_iota(jnp.int32, (1, hw), 1)
    row = lax.broadcasted_iota(jnp.int32, (blk, blk), 0)
    col = lax.broadcasted_iota(jnp.int32, (blk, blk), 1)
    causal = col <= row
    dist = (col - row).astype(F32)
    bidx = lax.broadcasted_iota(jnp.int32, (nbp, blk), 0)
    past = bidx < i
    oh_row = lax.broadcasted_iota(jnp.int32, (LANES, blk), 0)
    start_d = pl.multiple_of(i * blk, blk)
    k_d = k_ref[0, pl.ds(start_d, blk), :]
    v_d = v_ref[0, pl.ds(start_d, blk), :]
    out = jnp.zeros((blk, hw), F32)
    for h in range(MOBA_HEADS):
        hm = (lane >= HEAD_DIM * h) & (lane < HEAD_DIM * (h + 1))
        qh = jnp.where(hm, q, jnp.zeros_like(q))
        slope = slopes[h]
        g = lax.dot_general(jnp.where(hm, km, 0.0), jnp.where(hm, qf, 0.0), (((1,), (1,)), ((), ())),
                            preferred_element_type=F32, precision=lax.Precision.HIGHEST)
        g = jnp.where(past, g, -jnp.inf)
        rank = jnp.zeros((nbp, blk), F32)
        for n2 in range(nb):
            gm = g[n2:n2 + 1, :]
            beats = (gm > g) | ((gm == g) & (bidx > n2))
            rank = rank + jnp.where(beats, 1.0, 0.0)
        sel_t = jnp.where(past & (rank < n_sel), 1.0, 0.0)
        sel_t = jnp.concatenate([sel_t, jnp.zeros((LANES - nbp, blk), F32)], axis=0)
        sel = sel_t.T.astype(BF16)

        s = _nt(qh, k_d) * scale + slope * dist
        s = jnp.where(causal, s, -jnp.inf)
        m = jnp.max(s, axis=-1, keepdims=True)
        p = jnp.exp(s - m)
        l = jnp.sum(p, axis=-1, keepdims=True)
        acc = _mm(p.astype(BF16), v_d)

        def body(j, carry, qh=qh, sel=sel, slope=slope):
            m, l, acc = carry
            st = pl.multiple_of(j * blk, blk)
            off = ((j - i) * blk).astype(F32)
            s = _nt(qh, k_ref[0, pl.ds(st, blk), :]) * scale + slope * (dist + off)
            onehot = jnp.where(oh_row == j, 1.0, 0.0).astype(BF16)
            chosen = _mm(sel, onehot) > 0.5
            s = jnp.where(chosen, s, -jnp.inf)
            m_new = jnp.maximum(m, jnp.max(s, axis=-1, keepdims=True))
            alpha = jnp.exp(m - m_new)
            p = jnp.exp(s - m_new)
            l = alpha * l + jnp.sum(p, axis=-1, keepdims=True)
            acc = alpha * acc + _mm(p.astype(BF16), v_ref[0, pl.ds(st, blk), :])
            return m_new, l, acc

        m, l, acc = lax.fori_loop(0, i, body, (m, l, acc))
        out = jnp.where(hm, acc / l, out)
    o_ref[0] = out.astype(BF16)


def _moba_attention(qb, kb, vb, slopes):
    B, S, hw = qb.shape
    blk = MOBA_BLOCK
    nb = S // blk
    nbp = -(-nb // 8) * 8
    n_sel = max(1, min(MOBA_TOPK, nb - 1))
    full = pl.BlockSpec((1, S, hw), lambda b, i: (b, 0, 0))
    tile = pl.BlockSpec((1, blk, hw), lambda b, i: (b, i, 0))
    return pl.pallas_call(
        functools.partial(_moba_kernel, nb=nb, n_sel=n_sel, slopes=slopes, scale=HEAD_DIM ** -0.5),
        grid=(B, nb),
        in_specs=[tile, full, full],
        out_specs=tile,
        out_shape=jax.ShapeDtypeStruct((B, S, hw), BF16),
        scratch_shapes=[pltpu.VMEM((nbp, hw), F32)],
        compiler_params=_params("arbitrary", "arbitrary"),
        name="moba_attn",
    )(qb, kb, vb)


INT_MIN = -2 ** 31


def _dsa_kernel(q_ref, k_ref, v_ref, iq_ref, ik_ref, iw_ref, o_ref, *, S, tq, n_keep, slopes, scale,
                idx_scale):
    i = pl.program_id(1)
    hw = DSA_HEADS * HEAD_DIM
    qpos = i * tq + lax.broadcasted_iota(jnp.int32, (tq, S), 0)
    kpos = lax.broadcasted_iota(jnp.int32, (tq, S), 1)
    causal = kpos <= qpos
    dist = (kpos - qpos).astype(F32)

    ik = ik_ref[0]
    iw = iw_ref[0]
    lane = lax.broadcasted_iota(jnp.int32, (1, LANES), 1)
    per = LANES // IDX_DIM
    score = jnp.zeros((tq, S), F32)
    for g in range(IDX_HEADS // per):
        iqg = iq_ref[0, :, LANES * g:LANES * (g + 1)]
        for r in range(per):
            hh = per * g + r
            qm = jnp.where((lane >= IDX_DIM * r) & (lane < IDX_DIM * (r + 1)), iqg, jnp.zeros_like(iqg))
            score = score + jnp.maximum(_nt(qm, ik), 0.0) * iw[:, hh:hh + 1]
    score = jnp.where(causal, score * idx_scale, -jnp.inf)

    bits = lax.bitcast_convert_type(score, jnp.int32)
    key = jnp.where(bits >= 0, bits, bits ^ jnp.int32(0x7FFFFFFF))
    kf = float(n_keep)

    def count_ge(t):
        return jnp.sum(jnp.where(key >= t, 1.0, 0.0), axis=-1, keepdims=True)

    thr = jnp.where(count_ge(jnp.int32(0)) >= kf, jnp.int32(0), jnp.int32(INT_MIN))
    thr = jnp.broadcast_to(thr, (tq, 1))

    def bit_step(it, thr):
        cand = thr | lax.shift_left(jnp.int32(1), jnp.int32(30) - it)
        return jnp.where(count_ge(cand) >= kf, cand, thr)

    thr = lax.fori_loop(0, 31, bit_step, thr)
    keep = (key >= thr) & causal

    q = q_ref[0]
    k = k_ref[0]
    v = v_ref[0]
    lane_h = lax.broadcasted_iota(jnp.int32, (1, hw), 1)
    out = jnp.zeros((tq, hw), F32)
    for h in range(DSA_HEADS):
        hm = (lane_h >= HEAD_DIM * h) & (lane_h < HEAD_DIM * (h + 1))
        qh = jnp.where(hm, q, jnp.zeros_like(q))
        s = _nt(qh, k) * scale + slopes[h] * dist
        s = jnp.where(keep, s, -jnp.inf)
        m = jnp.max(s, axis=-1, keepdims=True)
        p = jnp.exp(s - m)
        l = jnp.sum(p, axis=-1, keepdims=True)
        out = jnp.where(hm, _mm(p.astype(BF16), v) / l, out)
    o_ref[0] = out.astype(BF16)


def _dsa_attention(qc, kc, vc, iq, ik, iw, slopes, tq):
    B, S, hw = qc.shape
    n_keep = min(DSA_TOPK, S // 4)
    full = lambda w: pl.BlockSpec((1, S, w), lambda b, i: (b, 0, 0))
    tile = lambda w: pl.BlockSpec((1, tq, w), lambda b, i: (b, i, 0))
    return pl.pallas_call(
        functools.partial(_dsa_kernel, S=S, tq=tq, n_keep=n_keep, slopes=slopes, scale=HEAD_DIM ** -0.5,
                          idx_scale=(IDX_DIM * IDX_HEADS) ** -0.5),
        grid=(B, S // tq),
        in_specs=[tile(hw), full(hw), full(hw), tile(IDX_HEADS * IDX_DIM), full(LANES), tile(LANES)],
        out_specs=tile(hw),
        out_shape=jax.ShapeDtypeStruct((B, S, hw), BF16),
        compiler_params=_params("arbitrary", "arbitrary"),
        name="dsa_attn",
    )(qc, kc, vc, iq, ik, iw)


def _merge_kernel(ya_ref, yb_ref, yc_ref, gt_ref, x_ref, mod_ref, wa_ref, wb_ref, wc_ref, wo_ref, o_ref):
    d = D_MODEL
    merged = gt_ref[0, :, 0:d].astype(F32) * _mm(ya_ref[0], wa_ref[...])
    merged = merged + gt_ref[0, :, d:2 * d].astype(F32) * _mm(yb_ref[0], wb_ref[...])
    merged = merged + gt_ref[0, :, 2 * d:3 * d].astype(F32) * _mm(yc_ref[0], wc_ref[...])
    o_ref[0] = x_ref[0] + mod_ref[0, 2:3, :] * _mm(merged.astype(BF16), wo_ref[...])


def _merge(ya, yb, yc, gt, x, mod, lw, tm):
    B, S, D = x.shape
    row = lambda w: pl.BlockSpec((1, tm, w), lambda b, i: (b, i, 0))
    consts = [lw["w_br_a"], lw["w_br_b"], lw["w_br_c"], lw["w_out"]]
    return pl.pallas_call(
        _merge_kernel,
        grid=(B, S // tm),
        in_specs=[row(ya.shape[-1]), row(yb.shape[-1]), row(yc.shape[-1]), row(3 * D), row(D),
                  pl.BlockSpec((1, 6, D), lambda b, i: (b, 0, 0))] + [_const_spec(c.shape) for c in consts],
        out_specs=row(D),
        out_shape=jax.ShapeDtypeStruct((B, S, D), F32),
        compiler_params=_params("arbitrary", "arbitrary"),
        name="merge_out",
    )(ya, yb, yc, gt, x, mod, *consts)


def _ffn_kernel(x_ref, mod_ref, n2_ref, wg_ref, wu_ref, wd_ref, o_ref, h_ref, acc_ref):
    j = pl.program_id(2)

    @pl.when(j == 0)
    def _():
        h = _rms(x_ref[0], n2_ref[...]) * (1.0 + mod_ref[0, 4:5, :]) + mod_ref[0, 3:4, :]
        h_ref[...] = h.astype(BF16)
        acc_ref[...] = jnp.zeros_like(acc_ref)

    h = h_ref[...]
    g = _mm(h, wg_ref[...])
    u = _mm(h, wu_ref[...])
    act = (g * jax.nn.sigmoid(g) * u).astype(BF16)
    acc_ref[...] += _mm(act, wd_ref[...])

    @pl.when(j == pl.num_programs(2) - 1)
    def _():
        o_ref[0] = x_ref[0] + mod_ref[0, 5:6, :] * acc_ref[...]


def _ffn(x, mod, lw, tm, tf):
    B, S, D = x.shape
    F = lw["w_g"].shape[1]
    row = pl.BlockSpec((1, tm, D), lambda b, i, j: (b, i, 0))
    return pl.pallas_call(
        _ffn_kernel,
        grid=(B, S // tm, F // tf),
        in_specs=[row, pl.BlockSpec((1, 6, D), lambda b, i, j: (b, 0, 0)),
                  pl.BlockSpec((1, D), lambda b, i, j: (0, 0)),
                  pl.BlockSpec((D, tf), lambda b, i, j: (0, j)),
                  pl.BlockSpec((D, tf), lambda b, i, j: (0, j)),
                  pl.BlockSpec((tf, D), lambda b, i, j: (j, 0))],
        out_specs=row,
        out_shape=jax.ShapeDtypeStruct((B, S, D), F32),
        scratch_shapes=[pltpu.VMEM((tm, D), BF16), pltpu.VMEM((tm, D), F32)],
        compiler_params=_params("arbitrary", "arbitrary", "arbitrary"),
        name="swiglu",
    )(x, mod, lw["n2"], lw["w_g"], lw["w_u"], lw["w_d"])


def _pad_cols(w, width):
    return jnp.pad(w, ((0, 0), (0, width - w.shape[1])))


def _layer_weights(l, norm1, w_in, g_cq, w_uq, g_ckv, w_ukv, qn_mla, kn_mla, qn_moba, kn_moba, qn_dsa,
                   kn_dsa, w_br_a, w_br_b, w_br_c, w_out, norm2, w_gu, w_down):
    D = D_MODEL
    w = w_in[l]
    o = 0
    cols = {}
    for name, wd in (("cq", MLA_Q_RANK), ("ckv", MLA_KV_RANK), ("kr", MLA_ROPE),
                     ("b", 3 * MOBA_HEADS * HEAD_DIM), ("c", 3 * DSA_HEADS * HEAD_DIM),
                     ("iq", IDX_HEADS * IDX_DIM), ("ik", IDX_DIM), ("iw", IDX_HEADS), ("g", 3 * D)):
        cols[name] = w[:, o:o + wd]
        o += wd
    kr_slot = jnp.pad(cols["kr"], ((0, 0), (MLA_NOPE, LANES - MLA_QK)))
    ik_rep = jnp.tile(cols["ik"], (1, LANES // IDX_DIM))
    w_all = jnp.concatenate([cols["cq"], cols["ckv"], kr_slot, cols["b"], cols["c"], cols["iq"], ik_rep,
                             _pad_cols(cols["iw"], LANES), cols["g"]], axis=1).astype(BF16)
    wuq = jnp.pad(w_uq[l].reshape(MLA_Q_RANK, MLA_HEADS, MLA_QK), ((0, 0), (0, 0), (0, LANES - MLA_QK)))
    wukv = w_ukv[l].reshape(MLA_KV_RANK, MLA_HEADS, MLA_NOPE + MLA_V)
    wk = jnp.pad(wukv[:, :, :MLA_NOPE], ((0, 0), (0, 0), (0, LANES - MLA_NOPE)))
    wv = wukv[:, :, MLA_NOPE:]
    row = lambda v: v.reshape(1, -1)
    return {
        "n1": row(norm1[l]), "w_all": w_all, "g_cq": row(g_cq[l]),
        "w_uq": wuq.reshape(MLA_Q_RANK, MLA_HEADS * LANES).astype(BF16), "g_ckv": row(g_ckv[l]),
        "w_k": wk.reshape(MLA_KV_RANK, MLA_HEADS * LANES).astype(BF16),
        "w_v": wv.reshape(MLA_KV_RANK, MLA_HEADS * MLA_V).astype(BF16),
        "qn_a": row(jnp.pad(qn_mla[l], (0, LANES - MLA_QK))), "kn_a": row(jnp.pad(kn_mla[l], (0, LANES - MLA_QK))),
        "qn_b": row(jnp.tile(qn_moba[l], MOBA_HEADS)), "kn_b": row(jnp.tile(kn_moba[l], MOBA_HEADS)),
        "qn_c": row(jnp.tile(qn_dsa[l], DSA_HEADS)), "kn_c": row(jnp.tile(kn_dsa[l], DSA_HEADS)),
        "w_br_a": w_br_a[l].astype(BF16), "w_br_b": w_br_b[l].astype(BF16), "w_br_c": w_br_c[l].astype(BF16),
        "w_out": w_out[l].astype(BF16), "n2": row(norm2[l]),
        "w_g": w_gu[l][:, :D_FF].astype(BF16), "w_u": w_gu[l][:, D_FF:].astype(BF16),
        "w_d": w_down[l].astype(BF16),
    }


def _rope_tables(S):
    half = MLA_ROPE // 2
    freqs = ROPE_THETA ** (-jnp.arange(half, dtype=F32) / half)
    ang = jnp.arange(S, dtype=F32)[:, None] * freqs[None, :]
    cos, sin = jnp.cos(ang), jnp.sin(ang)
    zeros = lambda n: jnp.zeros((S, n), F32)
    tail = LANES - MLA_QK
    rc = jnp.concatenate([jnp.ones((S, MLA_NOPE), F32), cos, cos, jnp.ones((S, tail), F32)], axis=1)
    rs1 = jnp.concatenate([zeros(MLA_NOPE), -sin, zeros(half + tail)], axis=1)
    rs2 = jnp.concatenate([zeros(MLA_NOPE + half), sin, zeros(tail)], axis=1)
    return rc, rs1, rs2


def kernel(x, c, w_ada, b_ada, norm1, w_in, g_cq, w_uq, g_ckv, w_ukv, qn_mla, kn_mla, qn_moba, kn_moba, qn_dsa, kn_dsa, w_br_a, w_br_b, w_br_c, w_out, norm2, w_gu, w_down):
    B, S, D = x.shape
    L = w_ada.shape[0]
    assert D == D_MODEL and S % MOBA_BLOCK == 0
    n_slopes = MOBA_HEADS + DSA_HEADS
    slopes = [2.0 ** (-8.0 * (i + 1) / n_slopes) for i in range(n_slopes)]
    mod_all = _ada(c, w_ada.astype(BF16), b_ada).reshape(L, B, 6, D)
    rope_tabs = _rope_tables(S)
    seg = jnp.arange(MOBA_HEADS * HEAD_DIM) // HEAD_DIM
    bd = (seg[:, None] == seg[None, :]).astype(F32)
    tm_proj = min(256, S)
    tm_out = min(512, S)
    for l in range(L):
        lw = _layer_weights(l, norm1, w_in, g_cq, w_uq, g_ckv, w_ukv, qn_mla, kn_mla, qn_moba, kn_moba,
                            qn_dsa, kn_dsa, w_br_a, w_br_b, w_br_c, w_out, norm2, w_gu, w_down)
        mod = mod_all[l]
        qa, ka, va, qb, kb, vb, qc, kc, vc, iq, ik, iw, gt = _inproj(x, mod, lw, rope_tabs, bd, tm_proj)
        ya = _mla_attention(qa, ka, va, tq=min(256, S))
        yb = _moba_attention(qb, kb, vb, tuple(slopes[0::2]))
        yc = _dsa_attention(qc, kc, vc, iq, ik, iw, tuple(slopes[1::2]), tq=min(128, S))
        x = _merge(ya, yb, yc, gt, x, mod, lw, tm_out)
        x = _ffn(x, mod, lw, tm_out, D_FF // 2)
    return x
```

```python
import functools

import jax
import jax.numpy as jnp
from jax import lax
from jax.experimental import pallas as pl
from jax.experimental.pallas import tpu as pltpu

F32 = jnp.float32
BF16 = jnp.bfloat16

D_MODEL = 1024
HEAD_DIM = 64
RMS_EPS = 1e-6
MLA_HEADS = 8
MLA_NOPE = 64
MLA_ROPE = 32
MLA_QK = MLA_NOPE + MLA_ROPE
MLA_V = 64
MLA_Q_RANK = 768
MLA_KV_RANK = 256
ROPE_THETA = 10000.0
MOBA_HEADS = 4
MOBA_BLOCK = 256
MOBA_TOPK = 3
DSA_HEADS = 4
IDX_HEADS = 16
IDX_DIM = 32
DSA_TOPK = 256
D_FF = 2816

LANES = 128
VMEM_LIMIT = 56 * 1024 * 1024

OFF_CQ = 0
OFF_CKV = OFF_CQ + MLA_Q_RANK
OFF_KR = OFF_CKV + MLA_KV_RANK
OFF_B = OFF_KR + LANES
OFF_C = OFF_B + 3 * MOBA_HEADS * HEAD_DIM
OFF_IQ = OFF_C + 3 * DSA_HEADS * HEAD_DIM
OFF_IK = OFF_IQ + IDX_HEADS * IDX_DIM
OFF_IW = OFF_IK + LANES
OFF_G = OFF_IW + LANES
N_ALL = OFF_G + 3 * D_MODEL


def _nt(a, b):
    return lax.dot_general(a, b, (((1,), (1,)), ((), ())), preferred_element_type=F32)


def _mm(a, b):
    return jnp.dot(a, b, preferred_element_type=F32)


def _rms(x, g):
    return x * lax.rsqrt(jnp.mean(x * x, axis=-1, keepdims=True) + RMS_EPS) * g


def _const_spec(shape):
    nd = len(shape)
    return pl.BlockSpec(shape, lambda *_: (0,) * nd, pipeline_mode=pl.Buffered(1))


def _params(*sem):
    return pltpu.CompilerParams(dimension_semantics=sem, vmem_limit_bytes=VMEM_LIMIT)


def _ada_kernel(c_ref, w_ref, b_ref, o_ref):
    c = c_ref[...]
    cond = (c * jax.nn.sigmoid(c)).astype(BF16)
    o_ref[0] = _mm(cond, w_ref[0]) + b_ref[0]


def _ada(c, w_ada, b_ada):
    L, D, N = w_ada.shape
    B = c.shape[0]
    tn = 1024
    return pl.pallas_call(
        _ada_kernel,
        grid=(L, N // tn),
        in_specs=[
            pl.BlockSpec((B, D), lambda l, n: (0, 0)),
            pl.BlockSpec((1, D, tn), lambda l, n: (l, 0, n)),
            pl.BlockSpec((1, 1, tn), lambda l, n: (l, 0, n)),
        ],
        out_specs=pl.BlockSpec((1, B, tn), lambda l, n: (l, 0, n)),
        out_shape=jax.ShapeDtypeStruct((L, B, N), F32),
        compiler_params=_params("arbitrary", "arbitrary"),
        name="ada_mod",
    )(c, w_ada, b_ada.reshape(L, 1, N))


def _rope(x, c, s1, s2):
    return x * c + pltpu.roll(x, LANES - 16, 1) * s1 + pltpu.roll(x, 16, 1) * s2


def _inproj_kernel(x_ref, mod_ref, n1_ref, w_ref, gcq_ref, wuq_ref, gckv_ref, wk_ref, wv_ref,
                   qna_ref, kna_ref, rc_ref, rs1_ref, rs2_ref, bd_ref,
                   qnb_ref, knb_ref, qnc_ref, knc_ref,
                   qa_ref, ka_ref, va_ref, qb_ref, kb_ref, vb_ref, qc_ref, kc_ref, vc_ref,
                   iq_ref, ik_ref, iw_ref, gt_ref):
    x = x_ref[0]
    sh1 = mod_ref[0, 0:1, :]
    sc1 = mod_ref[0, 1:2, :]
    hb = (_rms(x, n1_ref[...]) * (1.0 + sc1) + sh1).astype(BF16)

    a = _mm(hb, w_ref[:, OFF_CQ:OFF_B])
    cqn = _rms(a[:, OFF_CQ:OFF_CKV], gcq_ref[...]).astype(BF16)
    ckvn = _rms(a[:, OFF_CKV:OFF_KR], gckv_ref[...]).astype(BF16)
    kr = a[:, OFF_KR:OFF_B]
    qraw = _mm(cqn, wuq_ref[...])
    kraw = _mm(ckvn, wk_ref[...])
    va_ref[0] = _mm(ckvn, wv_ref[...]).astype(BF16)
    rc, rs1, rs2 = rc_ref[...], rs1_ref[...], rs2_ref[...]
    qna, kna = qna_ref[...], kna_ref[...]
    inv_qk = 1.0 / MLA_QK
    for h in range(MLA_HEADS):
        sl = slice(LANES * h, LANES * (h + 1))
        qh = qraw[:, sl]
        qh = qh * lax.rsqrt(jnp.sum(qh * qh, axis=-1, keepdims=True) * inv_qk + RMS_EPS) * qna
        qa_ref[0, :, sl] = _rope(qh, rc, rs1, rs2).astype(BF16)
        kh = kraw[:, sl] + kr
        kh = kh * lax.rsqrt(jnp.sum(kh * kh, axis=-1, keepdims=True) * inv_qk + RMS_EPS) * kna
        ka_ref[0, :, sl] = _rope(kh, rc, rs1, rs2).astype(BF16)

    bd = bd_ref[...]

    def segnorm(v, g):
        ss = jnp.dot(v * v, bd, preferred_element_type=F32, precision=lax.Precision.HIGHEST)
        return v * lax.rsqrt(ss * (1.0 / HEAD_DIM) + RMS_EPS) * g

    hw = MOBA_HEADS * HEAD_DIM
    pb = _mm(hb, w_ref[:, OFF_B:OFF_C])
    qb_ref[0] = segnorm(pb[:, 0:hw], qnb_ref[...]).astype(BF16)
    kb_ref[0] = segnorm(pb[:, hw:2 * hw], knb_ref[...]).astype(BF16)
    vb_ref[0] = pb[:, 2 * hw:3 * hw].astype(BF16)
    pc = _mm(hb, w_ref[:, OFF_C:OFF_IQ])
    qc_ref[0] = segnorm(pc[:, 0:hw], qnc_ref[...]).astype(BF16)
    kc_ref[0] = segnorm(pc[:, hw:2 * hw], knc_ref[...]).astype(BF16)
    vc_ref[0] = pc[:, 2 * hw:3 * hw].astype(BF16)

    pi = _mm(hb, w_ref[:, OFF_IQ:OFF_G])
    iq_ref[0] = pi[:, 0:OFF_IK - OFF_IQ].astype(BF16)
    ik_ref[0] = pi[:, OFF_IK - OFF_IQ:OFF_IW - OFF_IQ].astype(BF16)
    iw_ref[0] = pi[:, OFF_IW - OFF_IQ:OFF_G - OFF_IQ]
    gt_ref[0] = jax.nn.sigmoid(_mm(hb, w_ref[:, OFF_G:N_ALL])).astype(BF16)


def _inproj(x, mod, lw, rope_tabs, bd, tm):
    B, S, D = x.shape
    hw = MOBA_HEADS * HEAD_DIM
    row = lambda w: pl.BlockSpec((1, tm, w), lambda b, i: (b, i, 0))
    tab = pl.BlockSpec((tm, LANES), lambda b, i: (i, 0))
    consts = [lw["n1"], lw["w_all"], lw["g_cq"], lw["w_uq"], lw["g_ckv"], lw["w_k"], lw["w_v"],
              lw["qn_a"], lw["kn_a"]]
    consts2 = [bd, lw["qn_b"], lw["kn_b"], lw["qn_c"], lw["kn_c"]]
    widths = [MLA_HEADS * LANES, MLA_HEADS * LANES, MLA_HEADS * MLA_V, hw, hw, hw, hw, hw, hw,
              IDX_HEADS * IDX_DIM, LANES, LANES, 3 * D_MODEL]
    dtypes = [BF16] * 11 + [F32, BF16]
    return pl.pallas_call(
        _inproj_kernel,
        grid=(B, S // tm),
        in_specs=[row(D), pl.BlockSpec((1, 6, D), lambda b, i: (b, 0, 0))]
        + [_const_spec(c.shape) for c in consts] + [tab, tab, tab]
        + [_const_spec(c.shape) for c in consts2],
        out_specs=[row(w) for w in widths],
        out_shape=[jax.ShapeDtypeStruct((B, S, w), dt) for w, dt in zip(widths, dtypes)],
        compiler_params=_params("arbitrary", "arbitrary"),
        name="in_proj",
    )(x, mod, *consts, *rope_tabs, *consts2)


NEG_INF = float("-inf")


def _fold_lanes(x, op):
    out = x[:, 0:LANES]
    for t in range(1, x.shape[1] // LANES):
        out = op(out, x[:, LANES * t:LANES * (t + 1)])
    return out


def _by_tile(i, n, body):
    for c in range(n):
        pl.when(i == c)(functools.partial(body, c))


def _softmax_pv(s_ref, v_ref, n_chunks, ch, m_acc):
    m = jnp.max(m_acc, axis=-1, keepdims=True)
    l_acc = None
    acc = None
    for j in range(n_chunks):
        p = jnp.exp(s_ref[:, j * ch:(j + 1) * ch] - m)
        lj = _fold_lanes(p, jnp.add)
        pv = _mm(p.astype(BF16), v_ref[0, j * ch:(j + 1) * ch, :])
        l_acc = lj if l_acc is None else l_acc + lj
        acc = pv if acc is None else acc + pv
    return acc / jnp.sum(l_acc, axis=-1, keepdims=True)


def _mla_kernel(q_ref, k_ref, v_ref, o_ref, s_ref, *, tq, nq):
    i = pl.program_id(2)
    row = lax.broadcasted_iota(jnp.int32, (tq, tq), 0)
    col = lax.broadcasted_iota(jnp.int32, (tq, tq), 1)
    causal = col <= row
    lane = lax.broadcasted_iota(jnp.int32, (tq, LANES), 1)

    def body(c):
        outs = []
        for hh in range(2):
            sl = slice(LANES * hh, LANES * (hh + 1))
            q = q_ref[0, :, sl]
            m_acc = jnp.full((tq, LANES), NEG_INF, F32)
            for j in range(c + 1):
                s = _nt(q, k_ref[0, j * tq:(j + 1) * tq, sl])
                if j == c:
                    s = jnp.where(causal, s, NEG_INF)
                s_ref[hh, :, j * tq:(j + 1) * tq] = s
                m_acc = jnp.maximum(m_acc, _fold_lanes(s, jnp.maximum))
            outs.append(_softmax_pv(s_ref.at[hh], v_ref, c + 1, tq, m_acc))
        o_ref[0] = jnp.where(lane < MLA_V, outs[0], outs[1]).astype(BF16)

    _by_tile(i, nq, body)


def _mla_attention(qa, ka, va, tq):
    B, S, _ = qa.shape
    pairs = MLA_HEADS // 2
    nq = S // tq
    return pl.pallas_call(
        functools.partial(_mla_kernel, tq=tq, nq=nq),
        grid=(B, pairs, nq),
        in_specs=[
            pl.BlockSpec((1, tq, 2 * LANES), lambda b, h, i: (b, i, h)),
            pl.BlockSpec((1, S, 2 * LANES), lambda b, h, i: (b, 0, h)),
            pl.BlockSpec((1, S, LANES), lambda b, h, i: (b, 0, h)),
        ],
        out_specs=pl.BlockSpec((1, tq, LANES), lambda b, h, i: (b, i, h)),
        out_shape=jax.ShapeDtypeStruct((B, S, MLA_HEADS * MLA_V), BF16),
        scratch_shapes=[pltpu.VMEM((2, tq, S), F32)],
        compiler_params=_params("arbitrary", "arbitrary", "arbitrary"),
        name="mla_attn",
    )(qa, ka, va)


def _moba_kernel(q_ref, k_ref, v_ref, o_ref, km_ref, s_ref, *, nb, n_sel, slopes):
    i = pl.program_id(1)
    blk = MOBA_BLOCK
    hw = MOBA_HEADS * HEAD_DIM
    nbp = km_ref.shape[0]

    @pl.when(i == 0)
    def _():
        km_ref[...] = jnp.zeros_like(km_ref)
        for n in range(nb):
            kb = k_ref[0, n * blk:(n + 1) * blk, :].astype(F32)
            km_ref[n:n + 1, :] = jnp.mean(kb, axis=0, keepdims=True)

    lane = lax.broadcasted_iota(jnp.int32, (1, hw), 1)
    row = lax.broadcasted_iota(jnp.int32, (blk, blk), 0)
    col = lax.broadcasted_iota(jnp.int32, (blk, blk), 1)
    causal = col <= row
    kcol = lax.broadcasted_iota(jnp.int32, (1, blk), 1)
    bidx = lax.broadcasted_iota(jnp.int32, (nbp, blk), 0)

    def body(c):
        q = q_ref[0]
        past = bidx < c
        out = jnp.zeros((blk, hw), F32)
        for h in range(MOBA_HEADS):
            hm = (lane >= HEAD_DIM * h) & (lane < HEAD_DIM * (h + 1))
            qh = jnp.where(hm, q, jnp.zeros_like(q))
            if c > 0:
                g = lax.dot_general(jnp.where(hm, km_ref[...], 0.0), qh.astype(F32), (((1,), (1,)), ((), ())),
                                    preferred_element_type=F32, precision=lax.Precision.HIGHEST)
                g = jnp.where(past, g, NEG_INF)
                rank = jnp.zeros((nbp, blk), F32)
                for n2 in range(c):
                    gm = g[n2:n2 + 1, :]
                    beats = (gm > g) | ((gm == g) & (bidx > n2))
                    rank = rank + jnp.where(beats, 1.0, 0.0)
                drop_t = jnp.where(past & (rank < n_sel), 0.0, NEG_INF)
                drop_t = jnp.concatenate([drop_t, jnp.zeros((LANES - nbp, blk), F32)], axis=0)
                drop = drop_t.T
            m_acc = jnp.full((blk, LANES), NEG_INF, F32)
            for j in range(c + 1):
                kbias = slopes[h] * (kcol + (j - c) * blk).astype(F32)
                s = _nt(qh, k_ref[0, j * blk:(j + 1) * blk, :]) + kbias
                if j == c:
                    s = jnp.where(causal, s, NEG_INF)
                else:
                    s = s + drop[:, j:j + 1]
                s_ref[h, :, j * blk:(j + 1) * blk] = s
                m_acc = jnp.maximum(m_acc, _fold_lanes(s, jnp.maximum))
            out = jnp.where(hm, _softmax_pv(s_ref.at[h], v_ref, c + 1, blk, m_acc), out)
        o_ref[0] = out.astype(BF16)

    _by_tile(i, nb, body)


def _moba_attention(qb, kb, vb, slopes):
    B, S, hw = qb.shape
    blk = MOBA_BLOCK
    nb = S // blk
    nbp = -(-nb // 8) * 8
    n_sel = max(1, min(MOBA_TOPK, nb - 1))
    full = pl.BlockSpec((1, S, hw), lambda b, i: (b, 0, 0))
    tile = pl.BlockSpec((1, blk, hw), lambda b, i: (b, i, 0))
    return pl.pallas_call(
        functools.partial(_moba_kernel, nb=nb, n_sel=n_sel, slopes=slopes),
        grid=(B, nb),
        in_specs=[tile, full, full],
        out_specs=tile,
        out_shape=jax.ShapeDtypeStruct((B, S, hw), BF16),
        scratch_shapes=[pltpu.VMEM((nbp, hw), F32), pltpu.VMEM((MOBA_HEADS, blk, S), F32)],
        compiler_params=_params("arbitrary", "arbitrary"),
        name="moba_attn",
    )(qb, kb, vb)


INT_MIN = -2 ** 31


def _dsa_kernel(q_ref, k_ref, v_ref, iq_ref, ik_ref, iw_ref, o_ref, key_ref, mb_ref, s_ref, *, tq, ch, per,
                ncls, n_keep, slopes, idx_scale):
    i = pl.program_id(1)
    hw = DSA_HEADS * HEAD_DIM
    q0 = i * tq
    qpos = q0 + lax.broadcasted_iota(jnp.int32, (tq, 1), 0)
    kcol = lax.broadcasted_iota(jnp.int32, (1, ch), 1)
    lane = lax.broadcasted_iota(jnp.int32, (1, LANES), 1)
    lane_h = lax.broadcasted_iota(jnp.int32, (1, hw), 1)
    grp = LANES // IDX_DIM
    kf = float(n_keep)

    def body(c):
        n_chunks = (c + 1) * per * tq // ch
        iw = iw_ref[0]
        qms, wbs = [], []
        for hh in range(IDX_HEADS):
            g, r = divmod(hh, grp)
            iqg = iq_ref[0, :, LANES * g:LANES * (g + 1)]
            qms.append(jnp.where((lane >= IDX_DIM * r) & (lane < IDX_DIM * (r + 1)), iqg, jnp.zeros_like(iqg)))
            wbs.append(jnp.broadcast_to(iw[:, hh:hh + 1], (tq, LANES)))
        for j in range(n_chunks):
            ikc = ik_ref[0, j * ch:(j + 1) * ch, :]
            tiles = [None] * (ch // LANES)
            for hh in range(IDX_HEADS):
                d = jnp.maximum(_nt(qms[hh], ikc), 0.0)
                for t in range(ch // LANES):
                    term = d[:, LANES * t:LANES * (t + 1)] * wbs[hh]
                    tiles[t] = term if tiles[t] is None else tiles[t] + term
            score = jnp.concatenate(tiles, axis=1) * idx_scale
            score = jnp.where(kcol + j * ch <= qpos, score, NEG_INF)
            bits = lax.bitcast_convert_type(score, jnp.int32)
            key_ref[:, j * ch:(j + 1) * ch] = jnp.where(bits >= 0, bits, bits ^ jnp.int32(0x7FFFFFFF))

        def count_ge(t):
            acc = None
            for j in range(n_chunks):
                hit = jnp.where(key_ref[:, j * ch:(j + 1) * ch] >= t, 1.0, 0.0)
                f = _fold_lanes(hit, jnp.add)
                acc = f if acc is None else acc + f
            return jnp.sum(acc, axis=-1, keepdims=True)

        zero = jnp.zeros((tq, 1), jnp.int32)
        thr = jnp.where(count_ge(zero) >= kf, zero, jnp.int32(INT_MIN))

        def bit_step(it, thr):
            cand = thr | lax.shift_left(jnp.int32(1), jnp.int32(30) - it)
            return jnp.where(count_ge(cand) >= kf, cand, thr)

        thr = lax.fori_loop(0, 31, bit_step, thr)
        for j in range(n_chunks):
            keep = (key_ref[:, j * ch:(j + 1) * ch] >= thr) & (kcol + j * ch <= qpos)
            mb_ref[:, j * ch:(j + 1) * ch] = jnp.where(keep, 0.0, NEG_INF)

        q = q_ref[0]
        out = jnp.zeros((tq, hw), F32)
        for h in range(DSA_HEADS):
            hm = (lane_h >= HEAD_DIM * h) & (lane_h < HEAD_DIM * (h + 1))
            qh = jnp.where(hm, q, jnp.zeros_like(q))
            m_acc = jnp.full((tq, LANES), NEG_INF, F32)
            for j in range(n_chunks):
                kbias = slopes[h] * (kcol + j * ch - q0).astype(F32)
                s = _nt(qh, k_ref[0, j * ch:(j + 1) * ch, :]) + kbias + mb_ref[:, j * ch:(j + 1) * ch]
                s_ref[h, :, j * ch:(j + 1) * ch] = s
                m_acc = jnp.maximum(m_acc, _fold_lanes(s, jnp.maximum))
            out = jnp.where(hm, _softmax_pv(s_ref.at[h], v_ref, n_chunks, ch, m_acc), out)
        o_ref[0] = out.astype(BF16)

    for c in range(ncls):
        pl.when((i >= c * per) & (i < (c + 1) * per))(functools.partial(body, c))


def _dsa_attention(qc, kc, vc, iq, ik, iw, slopes, tq, ncls):
    B, S, hw = qc.shape
    nq = S // tq
    per = nq // ncls
    ch = min(512, per * tq)
    n_keep = min(DSA_TOPK, S // 4)
    full = lambda w: pl.BlockSpec((1, S, w), lambda b, i: (b, 0, 0))
    tile = lambda w: pl.BlockSpec((1, tq, w), lambda b, i: (b, i, 0))
    return pl.pallas_call(
        functools.partial(_dsa_kernel, tq=tq, ch=ch, per=per, ncls=ncls, n_keep=n_keep, slopes=slopes,
                          idx_scale=(IDX_DIM * IDX_HEADS) ** -0.5),
        grid=(B, nq),
        in_specs=[tile(hw), full(hw), full(hw), tile(IDX_HEADS * IDX_DIM), full(LANES), tile(LANES)],
        out_specs=tile(hw),
        out_shape=jax.ShapeDtypeStruct((B, S, hw), BF16),
        scratch_shapes=[pltpu.VMEM((tq, S), jnp.int32), pltpu.VMEM((tq, S), F32),
                        pltpu.VMEM((DSA_HEADS, tq, S), F32)],
        compiler_params=_params("arbitrary", "arbitrary"),
        name="dsa_attn",
    )(qc, kc, vc, iq, ik, iw)


def _merge_kernel(ya_ref, yb_ref, yc_ref, gt_ref, x_ref, mod_ref, wa_ref, wb_ref, wc_ref, wo_ref, o_ref):
    d = D_MODEL
    merged = gt_ref[0, :, 0:d].astype(F32) * _mm(ya_ref[0], wa_ref[...])
    merged = merged + gt_ref[0, :, d:2 * d].astype(F32) * _mm(yb_ref[0], wb_ref[...])
    merged = merged + gt_ref[0, :, 2 * d:3 * d].astype(F32) * _mm(yc_ref[0], wc_ref[...])
    o_ref[0] = x_ref[0] + mod_ref[0, 2:3, :] * _mm(merged.astype(BF16), wo_ref[...])


def _merge(ya, yb, yc, gt, x, mod, lw, tm):
    B, S, D = x.shape
    row = lambda w: pl.BlockSpec((1, tm, w), lambda b, i: (b, i, 0))
    consts = [lw["w_br_a"], lw["w_br_b"], lw["w_br_c"], lw["w_out"]]
    return pl.pallas_call(
        _merge_kernel,
        grid=(B, S // tm),
        in_specs=[row(ya.shape[-1]), row(yb.shape[-1]), row(yc.shape[-1]), row(3 * D), row(D),
                  pl.BlockSpec((1, 6, D), lambda b, i: (b, 0, 0))] + [_const_spec(c.shape) for c in consts],
        out_specs=row(D),
        out_shape=jax.ShapeDtypeStruct((B, S, D), F32),
        compiler_params=_params("arbitrary", "arbitrary"),
        name="merge_out",
    )(ya, yb, yc, gt, x, mod, *consts)


def _ffn_kernel(x_ref, mod_ref, n2_ref, wg_ref, wu_ref, wd_ref, o_ref, h_ref, acc_ref):
    j = pl.program_id(2)

    @pl.when(j == 0)
    def _():
        h = _rms(x_ref[0], n2_ref[...]) * (1.0 + mod_ref[0, 4:5, :]) + mod_ref[0, 3:4, :]
        h_ref[...] = h.astype(BF16)
        acc_ref[...] = jnp.zeros_like(acc_ref)

    h = h_ref[...]
    g = _mm(h, wg_ref[...])
    u = _mm(h, wu_ref[...])
    act = (g * jax.nn.sigmoid(g) * u).astype(BF16)
    acc_ref[...] += _mm(act, wd_ref[...])

    @pl.when(j == pl.num_programs(2) - 1)
    def _():
        o_ref[0] = x_ref[0] + mod_ref[0, 5:6, :] * acc_ref[...]


def _ffn(x, mod, lw, tm, tf):
    B, S, D = x.shape
    F = lw["w_g"].shape[1]
    row = pl.BlockSpec((1, tm, D), lambda b, i, j: (b, i, 0))
    return pl.pallas_call(
        _ffn_kernel,
        grid=(B, S // tm, F // tf),
        in_specs=[row, pl.BlockSpec((1, 6, D), lambda b, i, j: (b, 0, 0)),
                  pl.BlockSpec((1, D), lambda b, i, j: (0, 0)),
                  pl.BlockSpec((D, tf), lambda b, i, j: (0, j)),
                  pl.BlockSpec((D, tf), lambda b, i, j: (0, j)),
                  pl.BlockSpec((tf, D), lambda b, i, j: (j, 0))],
        out_specs=row,
        out_shape=jax.ShapeDtypeStruct((B, S, D), F32),
        scratch_shapes=[pltpu.VMEM((tm, D), BF16), pltpu.VMEM((tm, D), F32)],
        compiler_params=_params("arbitrary", "arbitrary", "arbitrary"),
        name="swiglu",
    )(x, mod, lw["n2"], lw["w_g"], lw["w_u"], lw["w_d"])


def _pad_cols(w, width):
    return jnp.pad(w, ((0, 0), (0, width - w.shape[1])))


def _layer_weights(l, norm1, w_in, g_cq, w_uq, g_ckv, w_ukv, qn_mla, kn_mla, qn_moba, kn_moba, qn_dsa,
                   kn_dsa, w_br_a, w_br_b, w_br_c, w_out, norm2, w_gu, w_down):
    D = D_MODEL
    w = w_in[l]
    o = 0
    cols = {}
    for name, wd in (("cq", MLA_Q_RANK), ("ckv", MLA_KV_RANK), ("kr", MLA_ROPE),
                     ("b", 3 * MOBA_HEADS * HEAD_DIM), ("c", 3 * DSA_HEADS * HEAD_DIM),
                     ("iq", IDX_HEADS * IDX_DIM), ("ik", IDX_DIM), ("iw", IDX_HEADS), ("g", 3 * D)):
        cols[name] = w[:, o:o + wd]
        o += wd
    kr_slot = jnp.pad(cols["kr"], ((0, 0), (MLA_NOPE, LANES - MLA_QK)))
    ik_rep = jnp.tile(cols["ik"], (1, LANES // IDX_DIM))
    w_all = jnp.concatenate([cols["cq"], cols["ckv"], kr_slot, cols["b"], cols["c"], cols["iq"], ik_rep,
                             _pad_cols(cols["iw"], LANES), cols["g"]], axis=1).astype(BF16)
    wuq = jnp.pad(w_uq[l].reshape(MLA_Q_RANK, MLA_HEADS, MLA_QK), ((0, 0), (0, 0), (0, LANES - MLA_QK)))
    wukv = w_ukv[l].reshape(MLA_KV_RANK, MLA_HEADS, MLA_NOPE + MLA_V)
    wk = jnp.pad(wukv[:, :, :MLA_NOPE], ((0, 0), (0, 0), (0, LANES - MLA_NOPE)))
    wv = wukv[:, :, MLA_NOPE:]
    row = lambda v: v.reshape(1, -1)
    return {
        "n1": row(norm1[l]), "w_all": w_all, "g_cq": row(g_cq[l]),
        "w_uq": wuq.reshape(MLA_Q_RANK, MLA_HEADS * LANES).astype(BF16), "g_ckv": row(g_ckv[l]),
        "w_k": wk.reshape(MLA_KV_RANK, MLA_HEADS * LANES).astype(BF16),
        "w_v": wv.reshape(MLA_KV_RANK, MLA_HEADS * MLA_V).astype(BF16),
        "qn_a": row(jnp.pad(qn_mla[l] * MLA_QK ** -0.5, (0, LANES - MLA_QK))), "kn_a": row(jnp.pad(kn_mla[l], (0, LANES - MLA_QK))),
        "qn_b": row(jnp.tile(qn_moba[l] * HEAD_DIM ** -0.5, MOBA_HEADS)), "kn_b": row(jnp.tile(kn_moba[l], MOBA_HEADS)),
        "qn_c": row(jnp.tile(qn_dsa[l] * HEAD_DIM ** -0.5, DSA_HEADS)), "kn_c": row(jnp.tile(kn_dsa[l], DSA_HEADS)),
        "w_br_a": w_br_a[l].astype(BF16), "w_br_b": w_br_b[l].astype(BF16), "w_br_c": w_br_c[l].astype(BF16),
        "w_out": w_out[l].astype(BF16), "n2": row(norm2[l]),
        "w_g": w_gu[l][:, :D_FF].astype(BF16), "w_u": w_gu[l][:, D_FF:].astype(BF16),
        "w_d": w_down[l].astype(BF16),
    }


def _rope_tables(S):
    half = MLA_ROPE // 2
    freqs = ROPE_THETA ** (-jnp.arange(half, dtype=F32) / half)
    ang = jnp.arange(S, dtype=F32)[:, None] * freqs[None, :]
    cos, sin = jnp.cos(ang), jnp.sin(ang)
    zeros = lambda n: jnp.zeros((S, n), F32)
    tail = LANES - MLA_QK
    rc = jnp.concatenate([jnp.ones((S, MLA_NOPE), F32), cos, cos, jnp.ones((S, tail), F32)], axis=1)
    rs1 = jnp.concatenate([zeros(MLA_NOPE), -sin, zeros(half + tail)], axis=1)
    rs2 = jnp.concatenate([zeros(MLA_NOPE + half), sin, zeros(tail)], axis=1)
    return rc, rs1, rs2


def _tiles(S):
    dsa_q = min(128, S)
    return {"proj": min(256, S), "out": min(512, S), "mla_q": min(256, S), "dsa_q": dsa_q,
            "dsa_cls": min(4, S // dsa_q)}


def kernel(x, c, w_ada, b_ada, norm1, w_in, g_cq, w_uq, g_ckv, w_ukv, qn_mla, kn_mla, qn_moba, kn_moba, qn_dsa, kn_dsa, w_br_a, w_br_b, w_br_c, w_out, norm2, w_gu, w_down):
    B, S, D = x.shape
    L = w_ada.shape[0]
    assert D == D_MODEL and S % MOBA_BLOCK == 0
    n_slopes = MOBA_HEADS + DSA_HEADS
    slopes = [2.0 ** (-8.0 * (i + 1) / n_slopes) for i in range(n_slopes)]
    mod_all = _ada(c, w_ada.astype(BF16), b_ada).reshape(L, B, 6, D)
    rope_tabs = _rope_tables(S)
    seg = jnp.arange(MOBA_HEADS * HEAD_DIM) // HEAD_DIM
    bd = (seg[:, None] == seg[None, :]).astype(F32)
    t = _tiles(S)
    for l in range(L):
        lw = _layer_weights(l, norm1, w_in, g_cq, w_uq, g_ckv, w_ukv, qn_mla, kn_mla, qn_moba, kn_moba,
                            qn_dsa, kn_dsa, w_br_a, w_br_b, w_br_c, w_out, norm2, w_gu, w_down)
        mod = mod_all[l]
        qa, ka, va, qb, kb, vb, qc, kc, vc, iq, ik, iw, gt = _inproj(x, mod, lw, rope_tabs, bd, t["proj"])
        ya = _mla_attention(qa, ka, va, tq=t["mla_q"])
        yb = _moba_attention(qb, kb, vb, tuple(slopes[0::2]))
        yc = _dsa_attention(qc, kc, vc, iq, ik, iw, tuple(slopes[1::2]), tq=t["dsa_q"], ncls=t["dsa_cls"])
        x = _merge(ya, yb, yc, gt, x, mod, lw, t["out"])
        x = _ffn(x, mod, lw, t["out"], D_FF // 2)
    return x
```

```python
import functools

import jax
import jax.numpy as jnp
from jax import lax
from jax.experimental import pallas as pl
from jax.experimental.pallas import tpu as pltpu

F32 = jnp.float32
BF16 = jnp.bfloat16

D_MODEL = 1024
HEAD_DIM = 64
RMS_EPS = 1e-6
MLA_HEADS = 8
MLA_NOPE = 64
MLA_ROPE = 32
MLA_QK = MLA_NOPE + MLA_ROPE
MLA_V = 64
MLA_Q_RANK = 768
MLA_KV_RANK = 256
ROPE_THETA = 10000.0
MOBA_HEADS = 4
MOBA_BLOCK = 256
MOBA_TOPK = 3
DSA_HEADS = 4
IDX_HEADS = 16
IDX_DIM = 32
DSA_TOPK = 256
D_FF = 2816

LANES = 128
VMEM_LIMIT = 56 * 1024 * 1024

OFF_CQ = 0
OFF_CKV = OFF_CQ + MLA_Q_RANK
OFF_KR = OFF_CKV + MLA_KV_RANK
OFF_B = OFF_KR + LANES
OFF_C = OFF_B + 3 * MOBA_HEADS * HEAD_DIM
OFF_IQ = OFF_C + 3 * DSA_HEADS * HEAD_DIM
OFF_IK = OFF_IQ + IDX_HEADS * IDX_DIM
OFF_IW = OFF_IK + LANES
OFF_G = OFF_IW + LANES
N_ALL = OFF_G + 3 * D_MODEL


def _nt(a, b):
    return lax.dot_general(a, b, (((1,), (1,)), ((), ())), preferred_element_type=F32)


def _mm(a, b):
    return jnp.dot(a, b, preferred_element_type=F32)


def _rms(x, g):
    return x * lax.rsqrt(jnp.mean(x * x, axis=-1, keepdims=True) + RMS_EPS) * g


def _const_spec(shape):
    nd = len(shape)
    return pl.BlockSpec(shape, lambda *_: (0,) * nd, pipeline_mode=pl.Buffered(1))


def _params(*sem):
    return pltpu.CompilerParams(dimension_semantics=sem, vmem_limit_bytes=VMEM_LIMIT)


def _ada_kernel(c_ref, w_ref, b_ref, o_ref):
    c = c_ref[...]
    cond = (c * jax.nn.sigmoid(c)).astype(BF16)
    o_ref[0] = _mm(cond, w_ref[0]) + b_ref[0]


def _ada(c, w_ada, b_ada):
    L, D, N = w_ada.shape
    B = c.shape[0]
    tn = 1024
    return pl.pallas_call(
        _ada_kernel,
        grid=(L, N // tn),
        in_specs=[
            pl.BlockSpec((B, D), lambda l, n: (0, 0)),
            pl.BlockSpec((1, D, tn), lambda l, n: (l, 0, n)),
            pl.BlockSpec((1, 1, tn), lambda l, n: (l, 0, n)),
        ],
        out_specs=pl.BlockSpec((1, B, tn), lambda l, n: (l, 0, n)),
        out_shape=jax.ShapeDtypeStruct((L, B, N), F32),
        compiler_params=_params("arbitrary", "arbitrary"),
        name="ada_mod",
    )(c, w_ada, b_ada.reshape(L, 1, N))


def _rope(x, c, s1, s2):
    return x * c + pltpu.roll(x, LANES - 16, 1) * s1 + pltpu.roll(x, 16, 1) * s2


def _inproj_kernel(x_ref, mod_ref, n1_ref, w_ref, gcq_ref, wuq_ref, gckv_ref, wk_ref, wv_ref,
                   qna_ref, kna_ref, rc_ref, rs1_ref, rs2_ref, bd_ref,
                   qnb_ref, knb_ref, qnc_ref, knc_ref,
                   qa_ref, ka_ref, va_ref, qb_ref, kb_ref, vb_ref, qc_ref, kc_ref, vc_ref,
                   iq_ref, ik_ref, iw_ref, gt_ref):
    x = x_ref[0]
    sh1 = mod_ref[0, 0:1, :]
    sc1 = mod_ref[0, 1:2, :]
    hb = (_rms(x, n1_ref[...]) * (1.0 + sc1) + sh1).astype(BF16)

    a = _mm(hb, w_ref[:, OFF_CQ:OFF_B])
    cqn = _rms(a[:, OFF_CQ:OFF_CKV], gcq_ref[...]).astype(BF16)
    ckvn = _rms(a[:, OFF_CKV:OFF_KR], gckv_ref[...]).astype(BF16)
    kr = a[:, OFF_KR:OFF_B]
    qraw = _mm(cqn, wuq_ref[...])
    kraw = _mm(ckvn, wk_ref[...])
    va_ref[0] = _mm(ckvn, wv_ref[...]).astype(BF16)
    rc, rs1, rs2 = rc_ref[...], rs1_ref[...], rs2_ref[...]
    qna, kna = qna_ref[...], kna_ref[...]
    inv_qk = 1.0 / MLA_QK
    for h in range(MLA_HEADS):
        sl = slice(LANES * h, LANES * (h + 1))
        qh = qraw[:, sl]
        qh = qh * lax.rsqrt(jnp.sum(qh * qh, axis=-1, keepdims=True) * inv_qk + RMS_EPS) * qna
        qa_ref[0, :, sl] = _rope(qh, rc, rs1, rs2).astype(BF16)
        kh = kraw[:, sl] + kr
        kh = kh * lax.rsqrt(jnp.sum(kh * kh, axis=-1, keepdims=True) * inv_qk + RMS_EPS) * kna
        ka_ref[0, :, sl] = _rope(kh, rc, rs1, rs2).astype(BF16)

    bd = bd_ref[...]

    def segnorm(v, g):
        ss = jnp.dot(v * v, bd, preferred_element_type=F32, precision=lax.Precision.HIGHEST)
        return v * lax.rsqrt(ss * (1.0 / HEAD_DIM) + RMS_EPS) * g

    hw = MOBA_HEADS * HEAD_DIM
    pb = _mm(hb, w_ref[:, OFF_B:OFF_C])
    qb_ref[0] = segnorm(pb[:, 0:hw], qnb_ref[...]).astype(BF16)
    kb_ref[0] = segnorm(pb[:, hw:2 * hw], knb_ref[...]).astype(BF16)
    vb_ref[0] = pb[:, 2 * hw:3 * hw].astype(BF16)
    pc = _mm(hb, w_ref[:, OFF_C:OFF_IQ])
    qc_ref[0] = segnorm(pc[:, 0:hw], qnc_ref[...]).astype(BF16)
    kc_ref[0] = segnorm(pc[:, hw:2 * hw], knc_ref[...]).astype(BF16)
    vc_ref[0] = pc[:, 2 * hw:3 * hw].T.astype(BF16)

    pi = _mm(hb, w_ref[:, OFF_IQ:OFF_G])
    iq_ref[0] = pi[:, 0:OFF_IK - OFF_IQ].astype(BF16)
    ik_ref[0] = pi[:, OFF_IK - OFF_IQ:OFF_IW - OFF_IQ].astype(BF16)
    iw_ref[0] = pi[:, OFF_IW - OFF_IQ:OFF_G - OFF_IQ].T
    gt_ref[0] = jax.nn.sigmoid(_mm(hb, w_ref[:, OFF_G:N_ALL])).astype(BF16)


def _inproj(x, mod, lw, rope_tabs, bd, tm):
    B, S, D = x.shape
    hw = MOBA_HEADS * HEAD_DIM
    row = lambda w: pl.BlockSpec((1, tm, w), lambda b, i: (b, i, 0))
    tab = pl.BlockSpec((tm, LANES), lambda b, i: (i, 0))
    consts = [lw["n1"], lw["w_all"], lw["g_cq"], lw["w_uq"], lw["g_ckv"], lw["w_k"], lw["w_v"],
              lw["qn_a"], lw["kn_a"]]
    consts2 = [bd, lw["qn_b"], lw["kn_b"], lw["qn_c"], lw["kn_c"]]
    widths = [MLA_HEADS * LANES, MLA_HEADS * LANES, MLA_HEADS * MLA_V, hw, hw, hw, hw, hw, hw,
              IDX_HEADS * IDX_DIM, LANES, LANES, 3 * D_MODEL]
    dtypes = [BF16] * 11 + [F32, BF16]
    transposed = [False] * 8 + [True, False, False, True, False]
    col = lambda w: pl.BlockSpec((1, w, tm), lambda b, i: (b, 0, i))
    return pl.pallas_call(
        _inproj_kernel,
        grid=(B, S // tm),
        in_specs=[row(D), pl.BlockSpec((1, 6, D), lambda b, i: (b, 0, 0))]
        + [_const_spec(c.shape) for c in consts] + [tab, tab, tab]
        + [_const_spec(c.shape) for c in consts2],
        out_specs=[col(w) if tr else row(w) for w, tr in zip(widths, transposed)],
        out_shape=[jax.ShapeDtypeStruct((B, w, S) if tr else (B, S, w), dt)
                   for w, dt, tr in zip(widths, dtypes, transposed)],
        compiler_params=_params("arbitrary", "arbitrary"),
        name="in_proj",
    )(x, mod, *consts, *rope_tabs, *consts2)


NEG_INF = float("-inf")


def _fold_lanes(x, op):
    out = x[:, 0:LANES]
    for t in range(1, x.shape[1] // LANES):
        out = op(out, x[:, LANES * t:LANES * (t + 1)])
    return out


def _by_tile(i, n, body):
    for c in range(n):
        pl.when(i == c)(functools.partial(body, c))


def _softmax_pv(s_ref, v_ref, n_chunks, ch, m_acc):
    m = jnp.max(m_acc, axis=-1, keepdims=True)
    l_acc = None
    acc = None
    for j in range(n_chunks):
        p = jnp.exp(s_ref[:, j * ch:(j + 1) * ch] - m)
        lj = _fold_lanes(p, jnp.add)
        pv = _mm(p.astype(BF16), v_ref[0, j * ch:(j + 1) * ch, :])
        l_acc = lj if l_acc is None else l_acc + lj
        acc = pv if acc is None else acc + pv
    return acc / jnp.sum(l_acc, axis=-1, keepdims=True)


def _mla_kernel(q_ref, k_ref, v_ref, o_ref, s_ref, *, tq, nq):
    i = pl.program_id(2)
    row = lax.broadcasted_iota(jnp.int32, (tq, tq), 0)
    col = lax.broadcasted_iota(jnp.int32, (tq, tq), 1)
    causal = col <= row
    lane = lax.broadcasted_iota(jnp.int32, (tq, LANES), 1)

    def body(c):
        outs = []
        for hh in range(2):
            sl = slice(LANES * hh, LANES * (hh + 1))
            q = q_ref[0, :, sl]
            m_acc = jnp.full((tq, LANES), NEG_INF, F32)
            for j in range(c + 1):
                s = _nt(q, k_ref[0, j * tq:(j + 1) * tq, sl])
                if j == c:
                    s = jnp.where(causal, s, NEG_INF)
                s_ref[hh, :, j * tq:(j + 1) * tq] = s
                m_acc = jnp.maximum(m_acc, _fold_lanes(s, jnp.maximum))
            outs.append(_softmax_pv(s_ref.at[hh], v_ref, c + 1, tq, m_acc))
        o_ref[0] = jnp.where(lane < MLA_V, outs[0], outs[1]).astype(BF16)

    _by_tile(i, nq, body)


def _mla_attention(qa, ka, va, tq):
    B, S, _ = qa.shape
    pairs = MLA_HEADS // 2
    nq = S // tq
    return pl.pallas_call(
        functools.partial(_mla_kernel, tq=tq, nq=nq),
        grid=(B, pairs, nq),
        in_specs=[
            pl.BlockSpec((1, tq, 2 * LANES), lambda b, h, i: (b, i, h)),
            pl.BlockSpec((1, S, 2 * LANES), lambda b, h, i: (b, 0, h)),
            pl.BlockSpec((1, S, LANES), lambda b, h, i: (b, 0, h)),
        ],
        out_specs=pl.BlockSpec((1, tq, LANES), lambda b, h, i: (b, i, h)),
        out_shape=jax.ShapeDtypeStruct((B, S, MLA_HEADS * MLA_V), BF16),
        scratch_shapes=[pltpu.VMEM((2, tq, S), F32)],
        compiler_params=_params("arbitrary", "arbitrary", "arbitrary"),
        name="mla_attn",
    )(qa, ka, va)


def _moba_kernel(q_ref, k_ref, v_ref, o_ref, km_ref, s_ref, *, nb, n_sel, slopes):
    i = pl.program_id(1)
    blk = MOBA_BLOCK
    hw = MOBA_HEADS * HEAD_DIM
    nbp = km_ref.shape[0]

    @pl.when(i == 0)
    def _():
        km_ref[...] = jnp.zeros_like(km_ref)
        for n in range(nb):
            kb = k_ref[0, n * blk:(n + 1) * blk, :].astype(F32)
            km_ref[n:n + 1, :] = jnp.mean(kb, axis=0, keepdims=True)

    lane = lax.broadcasted_iota(jnp.int32, (1, hw), 1)
    row = lax.broadcasted_iota(jnp.int32, (blk, blk), 0)
    col = lax.broadcasted_iota(jnp.int32, (blk, blk), 1)
    causal = col <= row
    kcol = lax.broadcasted_iota(jnp.int32, (1, blk), 1)
    bidx = lax.broadcasted_iota(jnp.int32, (nbp, blk), 0)

    def body(c):
        q = q_ref[0]
        past = bidx < c
        out = jnp.zeros((blk, hw), F32)
        for h in range(MOBA_HEADS):
            hm = (lane >= HEAD_DIM * h) & (lane < HEAD_DIM * (h + 1))
            qh = jnp.where(hm, q, jnp.zeros_like(q))
            if c > 0:
                g = lax.dot_general(jnp.where(hm, km_ref[...], 0.0), qh.astype(F32), (((1,), (1,)), ((), ())),
                                    preferred_element_type=F32, precision=lax.Precision.HIGHEST)
                g = jnp.where(past, g, NEG_INF)
                rank = jnp.zeros((nbp, blk), F32)
                for n2 in range(c):
                    gm = g[n2:n2 + 1, :]
                    beats = (gm > g) | ((gm == g) & (bidx > n2))
                    rank = rank + jnp.where(beats, 1.0, 0.0)
                drop_t = jnp.where(past & (rank < n_sel), 0.0, NEG_INF)
                drop_t = jnp.concatenate([drop_t, jnp.zeros((LANES - nbp, blk), F32)], axis=0)
                drop = drop_t.T
            m_acc = jnp.full((blk, LANES), NEG_INF, F32)
            for j in range(c + 1):
                kbias = slopes[h] * (kcol + (j - c) * blk).astype(F32)
                s = _nt(qh, k_ref[0, j * blk:(j + 1) * blk, :]) + kbias
                if j == c:
                    s = jnp.where(causal, s, NEG_INF)
                else:
                    s = s + drop[:, j:j + 1]
                s_ref[h, :, j * blk:(j + 1) * blk] = s
                m_acc = jnp.maximum(m_acc, _fold_lanes(s, jnp.maximum))
            out = jnp.where(hm, _softmax_pv(s_ref.at[h], v_ref, c + 1, blk, m_acc), out)
        o_ref[0] = out.astype(BF16)

    _by_tile(i, nb, body)


def _moba_attention(qb, kb, vb, slopes):
    B, S, hw = qb.shape
    blk = MOBA_BLOCK
    nb = S // blk
    nbp = -(-nb // 8) * 8
    n_sel = max(1, min(MOBA_TOPK, nb - 1))
    full = pl.BlockSpec((1, S, hw), lambda b, i: (b, 0, 0))
    tile = pl.BlockSpec((1, blk, hw), lambda b, i: (b, i, 0))
    return pl.pallas_call(
        functools.partial(_moba_kernel, nb=nb, n_sel=n_sel, slopes=slopes),
        grid=(B, nb),
        in_specs=[tile, full, full],
        out_specs=tile,
        out_shape=jax.ShapeDtypeStruct((B, S, hw), BF16),
        scratch_shapes=[pltpu.VMEM((nbp, hw), F32), pltpu.VMEM((MOBA_HEADS, blk, S), F32)],
        compiler_params=_params("arbitrary", "arbitrary"),
        name="moba_attn",
    )(qb, kb, vb)


INT_MIN = -2 ** 31


def _fold_rows(x, op):
    out = x[0:8, :]
    for t in range(1, x.shape[0] // 8):
        out = op(out, x[8 * t:8 * (t + 1), :])
    return out


def _dsa_kernel(q_ref, k_ref, vt_ref, iq_ref, ik_ref, iwt_ref, kpos_ref, o_ref, key_ref, mb_ref, s_ref, *,
                tq, ch, per, ncls, n_keep, slopes, idx_scale):
    i = pl.program_id(1)
    hw = DSA_HEADS * HEAD_DIM
    q0 = i * tq
    qpos = q0 + lax.broadcasted_iota(jnp.int32, (1, tq), 1)
    krow = lax.broadcasted_iota(jnp.int32, (ch, 1), 0)
    lane = lax.broadcasted_iota(jnp.int32, (1, LANES), 1)
    lane_h = lax.broadcasted_iota(jnp.int32, (1, hw), 1)
    grp = LANES // IDX_DIM
    kf = float(n_keep)

    def body(c):
        n_chunks = (c + 1) * per * tq // ch
        rows = lambda j: slice(j * ch, (j + 1) * ch)
        iwt = iwt_ref[0]
        qms = []
        for hh in range(IDX_HEADS):
            g, r = divmod(hh, grp)
            iqg = iq_ref[0, :, LANES * g:LANES * (g + 1)]
            qms.append(jnp.where((lane >= IDX_DIM * r) & (lane < IDX_DIM * (r + 1)), iqg, jnp.zeros_like(iqg)))
        for j in range(n_chunks):
            ikc = ik_ref[0, rows(j), :]
            score = None
            for hh in range(IDX_HEADS):
                term = jnp.maximum(_nt(ikc, qms[hh]), 0.0) * iwt[hh:hh + 1, :]
                score = term if score is None else score + term
            score = jnp.where(krow + j * ch <= qpos, score * idx_scale, NEG_INF)
            bits = lax.bitcast_convert_type(score, jnp.int32)
            key_ref[rows(j), :] = jnp.where(bits >= 0, bits, bits ^ jnp.int32(0x7FFFFFFF))

        def count_ge(t):
            acc = None
            for j in range(n_chunks):
                f = _fold_rows(jnp.where(key_ref[rows(j), :] >= t, 1.0, 0.0), jnp.add)
                acc = f if acc is None else acc + f
            return jnp.sum(acc, axis=0, keepdims=True)

        zero = jnp.zeros((1, tq), jnp.int32)
        thr = jnp.where(count_ge(zero) >= kf, zero, jnp.int32(INT_MIN))

        def bit_step(it, thr):
            cand = thr | lax.shift_left(jnp.int32(1), jnp.int32(30) - it)
            return jnp.where(count_ge(cand) >= kf, cand, thr)

        thr = lax.fori_loop(0, 31, bit_step, thr)
        for j in range(n_chunks):
            keep = (key_ref[rows(j), :] >= thr) & (krow + j * ch <= qpos)
            mb_ref[rows(j), :] = jnp.where(keep, 0.0, NEG_INF)

        q = q_ref[0]
        q0f = q0.astype(F32)
        outs = []
        for h in range(DSA_HEADS):
            hm = (lane_h >= HEAD_DIM * h) & (lane_h < HEAD_DIM * (h + 1))
            qh = jnp.where(hm, q, jnp.zeros_like(q))
            m8 = jnp.full((8, tq), NEG_INF, F32)
            for j in range(n_chunks):
                kb = slopes[h] * (kpos_ref[rows(j), :] - q0f)
                kb = jnp.concatenate([kb] * (tq // LANES), axis=1)
                s = _nt(k_ref[0, rows(j), :], qh) + kb + mb_ref[rows(j), :]
                s_ref[h, rows(j), :] = s
                m8 = jnp.maximum(m8, _fold_rows(s, jnp.maximum))
            m = jnp.max(m8, axis=0, keepdims=True)
            l8 = None
            acc = None
            for j in range(n_chunks):
                p = jnp.exp(s_ref[h, rows(j), :] - m)
                f = _fold_rows(p, jnp.add)
                pv = _mm(vt_ref[0, HEAD_DIM * h:HEAD_DIM * (h + 1), rows(j)], p.astype(BF16))
                l8 = f if l8 is None else l8 + f
                acc = pv if acc is None else acc + pv
            outs.append(acc / jnp.sum(l8, axis=0, keepdims=True))
        o_ref[0] = jnp.concatenate(outs, axis=0).T.astype(BF16)

    for c in range(ncls):
        pl.when((i >= c * per) & (i < (c + 1) * per))(functools.partial(body, c))


def _dsa_attention(qc, kc, vct, iq, ik, iwt, slopes, tq, ncls):
    B, S, hw = qc.shape
    nq = S // tq
    per = nq // ncls
    ch = tq
    n_keep = min(DSA_TOPK, S // 4)
    kpos = jnp.broadcast_to(jnp.arange(S, dtype=F32)[:, None], (S, LANES))
    full = lambda w: pl.BlockSpec((1, S, w), lambda b, i: (b, 0, 0))
    tile = lambda w: pl.BlockSpec((1, tq, w), lambda b, i: (b, i, 0))
    return pl.pallas_call(
        functools.partial(_dsa_kernel, tq=tq, ch=ch, per=per, ncls=ncls, n_keep=n_keep, slopes=slopes,
                          idx_scale=(IDX_DIM * IDX_HEADS) ** -0.5),
        grid=(B, nq),
        in_specs=[tile(hw), full(hw), pl.BlockSpec((1, hw, S), lambda b, i: (b, 0, 0)),
                  tile(IDX_HEADS * IDX_DIM), full(LANES), pl.BlockSpec((1, LANES, tq), lambda b, i: (b, 0, i)),
                  pl.BlockSpec((S, LANES), lambda b, i: (0, 0))],
        out_specs=tile(hw),
        out_shape=jax.ShapeDtypeStruct((B, S, hw), BF16),
        scratch_shapes=[pltpu.VMEM((S, tq), jnp.int32), pltpu.VMEM((S, tq), F32),
                        pltpu.VMEM((DSA_HEADS, S, tq), F32)],
        compiler_params=_params("arbitrary", "arbitrary"),
        name="dsa_attn",
    )(qc, kc, vct, iq, ik, iwt, kpos)


def _merge_kernel(ya_ref, yb_ref, yc_ref, gt_ref, x_ref, mod_ref, wa_ref, wb_ref, wc_ref, wo_ref, o_ref):
    d = D_MODEL
    merged = gt_ref[0, :, 0:d].astype(F32) * _mm(ya_ref[0], wa_ref[...])
    merged = merged + gt_ref[0, :, d:2 * d].astype(F32) * _mm(yb_ref[0], wb_ref[...])
    merged = merged + gt_ref[0, :, 2 * d:3 * d].astype(F32) * _mm(yc_ref[0], wc_ref[...])
    o_ref[0] = x_ref[0] + mod_ref[0, 2:3, :] * _mm(merged.astype(BF16), wo_ref[...])


def _merge(ya, yb, yc, gt, x, mod, lw, tm):
    B, S, D = x.shape
    row = lambda w: pl.BlockSpec((1, tm, w), lambda b, i: (b, i, 0))
    consts = [lw["w_br_a"], lw["w_br_b"], lw["w_br_c"], lw["w_out"]]
    return pl.pallas_call(
        _merge_kernel,
        grid=(B, S // tm),
        in_specs=[row(ya.shape[-1]), row(yb.shape[-1]), row(yc.shape[-1]), row(3 * D), row(D),
                  pl.BlockSpec((1, 6, D), lambda b, i: (b, 0, 0))] + [_const_spec(c.shape) for c in consts],
        out_specs=row(D),
        out_shape=jax.ShapeDtypeStruct((B, S, D), F32),
        compiler_params=_params("arbitrary", "arbitrary"),
        name="merge_out",
    )(ya, yb, yc, gt, x, mod, *consts)


def _ffn_kernel(x_ref, mod_ref, n2_ref, wg_ref, wu_ref, wd_ref, o_ref, h_ref, acc_ref):
    j = pl.program_id(2)

    @pl.when(j == 0)
    def _():
        h = _rms(x_ref[0], n2_ref[...]) * (1.0 + mod_ref[0, 4:5, :]) + mod_ref[0, 3:4, :]
        h_ref[...] = h.astype(BF16)
        acc_ref[...] = jnp.zeros_like(acc_ref)

    h = h_ref[...]
    g = _mm(h, wg_ref[...])
    u = _mm(h, wu_ref[...])
    act = (g * jax.nn.sigmoid(g) * u).astype(BF16)
    acc_ref[...] += _mm(act, wd_ref[...])

    @pl.when(j == pl.num_programs(2) - 1)
    def _():
        o_ref[0] = x_ref[0] + mod_ref[0, 5:6, :] * acc_ref[...]


def _ffn(x, mod, lw, tm, tf):
    B, S, D = x.shape
    F = lw["w_g"].shape[1]
    row = pl.BlockSpec((1, tm, D), lambda b, i, j: (b, i, 0))
    return pl.pallas_call(
        _ffn_kernel,
        grid=(B, S // tm, F // tf),
        in_specs=[row, pl.BlockSpec((1, 6, D), lambda b, i, j: (b, 0, 0)),
                  pl.BlockSpec((1, D), lambda b, i, j: (0, 0)),
                  pl.BlockSpec((D, tf), lambda b, i, j: (0, j)),
                  pl.BlockSpec((D, tf), lambda b, i, j: (0, j)),
                  pl.BlockSpec((tf, D), lambda b, i, j: (j, 0))],
        out_specs=row,
        out_shape=jax.ShapeDtypeStruct((B, S, D), F32),
        scratch_shapes=[pltpu.VMEM((tm, D), BF16), pltpu.VMEM((tm, D), F32)],
        compiler_params=_params("arbitrary", "arbitrary", "arbitrary"),
        name="swiglu",
    )(x, mod, lw["n2"], lw["w_g"], lw["w_u"], lw["w_d"])


def _pad_cols(w, width):
    return jnp.pad(w, ((0, 0), (0, width - w.shape[1])))


def _layer_weights(l, norm1, w_in, g_cq, w_uq, g_ckv, w_ukv, qn_mla, kn_mla, qn_moba, kn_moba, qn_dsa,
                   kn_dsa, w_br_a, w_br_b, w_br_c, w_out, norm2, w_gu, w_down):
    D = D_MODEL
    w = w_in[l]
    o = 0
    cols = {}
    for name, wd in (("cq", MLA_Q_RANK), ("ckv", MLA_KV_RANK), ("kr", MLA_ROPE),
                     ("b", 3 * MOBA_HEADS * HEAD_DIM), ("c", 3 * DSA_HEADS * HEAD_DIM),
                     ("iq", IDX_HEADS * IDX_DIM), ("ik", IDX_DIM), ("iw", IDX_HEADS), ("g", 3 * D)):
        cols[name] = w[:, o:o + wd]
        o += wd
    kr_slot = jnp.pad(cols["kr"], ((0, 0), (MLA_NOPE, LANES - MLA_QK)))
    ik_rep = jnp.tile(cols["ik"], (1, LANES // IDX_DIM))
    w_all = jnp.concatenate([cols["cq"], cols["ckv"], kr_slot, cols["b"], cols["c"], cols["iq"], ik_rep,
                             _pad_cols(cols["iw"], LANES), cols["g"]], axis=1).astype(BF16)
    wuq = jnp.pad(w_uq[l].reshape(MLA_Q_RANK, MLA_HEADS, MLA_QK), ((0, 0), (0, 0), (0, LANES - MLA_QK)))
    wukv = w_ukv[l].reshape(MLA_KV_RANK, MLA_HEADS, MLA_NOPE + MLA_V)
    wk = jnp.pad(wukv[:, :, :MLA_NOPE], ((0, 0), (0, 0), (0, LANES - MLA_NOPE)))
    wv = wukv[:, :, MLA_NOPE:]
    row = lambda v: v.reshape(1, -1)
    return {
        "n1": row(norm1[l]), "w_all": w_all, "g_cq": row(g_cq[l]),
        "w_uq": wuq.reshape(MLA_Q_RANK, MLA_HEADS * LANES).astype(BF16), "g_ckv": row(g_ckv[l]),
        "w_k": wk.reshape(MLA_KV_RANK, MLA_HEADS * LANES).astype(BF16),
        "w_v": wv.reshape(MLA_KV_RANK, MLA_HEADS * MLA_V).astype(BF16),
        "qn_a": row(jnp.pad(qn_mla[l] * MLA_QK ** -0.5, (0, LANES - MLA_QK))), "kn_a": row(jnp.pad(kn_mla[l], (0, LANES - MLA_QK))),
        "qn_b": row(jnp.tile(qn_moba[l] * HEAD_DIM ** -0.5, MOBA_HEADS)), "kn_b": row(jnp.tile(kn_moba[l], MOBA_HEADS)),
        "qn_c": row(jnp.tile(qn_dsa[l] * HEAD_DIM ** -0.5, DSA_HEADS)), "kn_c": row(jnp.tile(kn_dsa[l], DSA_HEADS)),
        "w_br_a": w_br_a[l].astype(BF16), "w_br_b": w_br_b[l].astype(BF16), "w_br_c": w_br_c[l].astype(BF16),
        "w_out": w_out[l].astype(BF16), "n2": row(norm2[l]),
        "w_g": w_gu[l][:, :D_FF].astype(BF16), "w_u": w_gu[l][:, D_FF:].astype(BF16),
        "w_d": w_down[l].astype(BF16),
    }


def _rope_tables(S):
    half = MLA_ROPE // 2
    freqs = ROPE_THETA ** (-jnp.arange(half, dtype=F32) / half)
    ang = jnp.arange(S, dtype=F32)[:, None] * freqs[None, :]
    cos, sin = jnp.cos(ang), jnp.sin(ang)
    zeros = lambda n: jnp.zeros((S, n), F32)
    tail = LANES - MLA_QK
    rc = jnp.concatenate([jnp.ones((S, MLA_NOPE), F32), cos, cos, jnp.ones((S, tail), F32)], axis=1)
    rs1 = jnp.concatenate([zeros(MLA_NOPE), -sin, zeros(half + tail)], axis=1)
    rs2 = jnp.concatenate([zeros(MLA_NOPE + half), sin, zeros(tail)], axis=1)
    return rc, rs1, rs2


def _tiles(S):
    dsa_q = min(256, S)
    return {"proj": min(256, S), "out": min(512, S), "mla_q": min(256, S), "dsa_q": dsa_q,
            "dsa_cls": min(4, S // dsa_q)}


def kernel(x, c, w_ada, b_ada, norm1, w_in, g_cq, w_uq, g_ckv, w_ukv, qn_mla, kn_mla, qn_moba, kn_moba, qn_dsa, kn_dsa, w_br_a, w_br_b, w_br_c, w_out, norm2, w_gu, w_down):
    B, S, D = x.shape
    L = w_ada.shape[0]
    assert D == D_MODEL and S % MOBA_BLOCK == 0
    n_slopes = MOBA_HEADS + DSA_HEADS
    slopes = [2.0 ** (-8.0 * (i + 1) / n_slopes) for i in range(n_slopes)]
    mod_all = _ada(c, w_ada.astype(BF16), b_ada).reshape(L, B, 6, D)
    rope_tabs = _rope_tables(S)
    seg = jnp.arange(MOBA_HEADS * HEAD_DIM) // HEAD_DIM
    bd = (seg[:, None] == seg[None, :]).astype(F32)
    t = _tiles(S)
    for l in range(L):
        lw = _layer_weights(l, norm1, w_in, g_cq, w_uq, g_ckv, w_ukv, qn_mla, kn_mla, qn_moba, kn_moba,
                            qn_dsa, kn_dsa, w_br_a, w_br_b, w_br_c, w_out, norm2, w_gu, w_down)
        mod = mod_all[l]
        qa, ka, va, qb, kb, vb, qc, kc, vc, iq, ik, iw, gt = _inproj(x, mod, lw, rope_tabs, bd, t["proj"])
        ya = _mla_attention(qa, ka, va, tq=t["mla_q"])
        yb = _moba_attention(qb, kb, vb, tuple(slopes[0::2]))
        yc = _dsa_attention(qc, kc, vc, iq, ik, iw, tuple(slopes[1::2]), tq=t["dsa_q"], ncls=t["dsa_cls"])
        x = _merge(ya, yb, yc, gt, x, mod, lw, t["out"])
        x = _ffn(x, mod, lw, t["out"], D_FF // 2)
    return x
```

```python
import functools

import jax
import jax.numpy as jnp
from jax import lax
from jax.experimental import pallas as pl
from jax.experimental.pallas import tpu as pltpu

F32 = jnp.float32
BF16 = jnp.bfloat16

D_MODEL = 1024
HEAD_DIM = 64
RMS_EPS = 1e-6
MLA_HEADS = 8
MLA_NOPE = 64
MLA_ROPE = 32
MLA_QK = MLA_NOPE + MLA_ROPE
MLA_V = 64
MLA_Q_RANK = 768
MLA_KV_RANK = 256
ROPE_THETA = 10000.0
MOBA_HEADS = 4
MOBA_BLOCK = 256
MOBA_TOPK = 3
DSA_HEADS = 4
IDX_HEADS = 16
IDX_DIM = 32
DSA_TOPK = 256
D_FF = 2816

LANES = 128
VMEM_LIMIT = 56 * 1024 * 1024

OFF_CQ = 0
OFF_CKV = OFF_CQ + MLA_Q_RANK
OFF_KR = OFF_CKV + MLA_KV_RANK
OFF_B = OFF_KR + LANES
OFF_C = OFF_B + 3 * MOBA_HEADS * HEAD_DIM
OFF_IQ = OFF_C + 3 * DSA_HEADS * HEAD_DIM
OFF_IK = OFF_IQ + IDX_HEADS * IDX_DIM
OFF_IW = OFF_IK + LANES
OFF_G = OFF_IW + LANES
N_ALL = OFF_G + 3 * D_MODEL


def _nt(a, b):
    return lax.dot_general(a, b, (((1,), (1,)), ((), ())), preferred_element_type=F32)


def _mm(a, b):
    return jnp.dot(a, b, preferred_element_type=F32)


def _split3(x):
    hi = x.astype(BF16)
    r = x - hi.astype(F32)
    mid = r.astype(BF16)
    lo = (r - mid.astype(F32)).astype(BF16)
    return hi, mid, lo


def _rms(x, g):
    return x * lax.rsqrt(jnp.mean(x * x, axis=-1, keepdims=True) + RMS_EPS) * g


def _const_spec(shape):
    nd = len(shape)
    return pl.BlockSpec(shape, lambda *_: (0,) * nd, pipeline_mode=pl.Buffered(1))


def _params(*sem):
    return pltpu.CompilerParams(dimension_semantics=sem, vmem_limit_bytes=VMEM_LIMIT)


def _ada_kernel(c_ref, w_ref, b_ref, o_ref):
    c = c_ref[...]
    cond = (c * jax.nn.sigmoid(c)).astype(BF16)
    o_ref[0] = _mm(cond, w_ref[0]) + b_ref[0]


def _ada(c, w_ada, b_ada):
    L, D, N = w_ada.shape
    B = c.shape[0]
    tn = 1024
    return pl.pallas_call(
        _ada_kernel,
        grid=(L, N // tn),
        in_specs=[
            pl.BlockSpec((B, D), lambda l, n: (0, 0)),
            pl.BlockSpec((1, D, tn), lambda l, n: (l, 0, n)),
            pl.BlockSpec((1, 1, tn), lambda l, n: (l, 0, n)),
        ],
        out_specs=pl.BlockSpec((1, B, tn), lambda l, n: (l, 0, n)),
        out_shape=jax.ShapeDtypeStruct((L, B, N), F32),
        compiler_params=_params("arbitrary", "arbitrary"),
        name="ada_mod",
    )(c, w_ada, b_ada.reshape(L, 1, N))


def _rope(x, c, s1, s2):
    return x * c + pltpu.roll(x, LANES - 16, 1) * s1 + pltpu.roll(x, 16, 1) * s2


def _inproj_kernel(x_ref, mod_ref, n1_ref, w_ref, gcq_ref, wuq_ref, gckv_ref, wk_ref, wv_ref,
                   qna_ref, kna_ref, rc_ref, rs1_ref, rs2_ref, bd_ref,
                   qnb_ref, knb_ref, qnc_ref, knc_ref,
                   qa_ref, ka_ref, va_ref, qb_ref, kb_ref, vb_ref, qc_ref, kc_ref, vc_ref,
                   iq_ref, ik_ref, iw_ref, gt_ref):
    x = x_ref[0]
    sh1 = mod_ref[0, 0:1, :]
    sc1 = mod_ref[0, 1:2, :]
    hb = (_rms(x, n1_ref[...]) * (1.0 + sc1) + sh1).astype(BF16)

    a = _mm(hb, w_ref[:, OFF_CQ:OFF_B])
    cqn = _rms(a[:, OFF_CQ:OFF_CKV], gcq_ref[...]).astype(BF16)
    ckvn = _rms(a[:, OFF_CKV:OFF_KR], gckv_ref[...]).astype(BF16)
    kr = a[:, OFF_KR:OFF_B]
    qraw = _mm(cqn, wuq_ref[...])
    kraw = _mm(ckvn, wk_ref[...])
    va_ref[0] = _mm(ckvn, wv_ref[...]).astype(BF16)
    rc, rs1, rs2 = rc_ref[...], rs1_ref[...], rs2_ref[...]
    qna, kna = qna_ref[...], kna_ref[...]
    inv_qk = 1.0 / MLA_QK
    for h in range(MLA_HEADS):
        sl = slice(LANES * h, LANES * (h + 1))
        qh = qraw[:, sl]
        qh = qh * lax.rsqrt(jnp.sum(qh * qh, axis=-1, keepdims=True) * inv_qk + RMS_EPS) * qna
        qa_ref[0, :, sl] = _rope(qh, rc, rs1, rs2).astype(BF16)
        kh = kraw[:, sl] + kr
        kh = kh * lax.rsqrt(jnp.sum(kh * kh, axis=-1, keepdims=True) * inv_qk + RMS_EPS) * kna
        ka_ref[0, :, sl] = _rope(kh, rc, rs1, rs2).astype(BF16)

    bd = bd_ref[...]

    def segnorm(v, g):
        ss = sum(_mm(part, bd) for part in _split3(v * v))
        return v * lax.rsqrt(ss * (1.0 / HEAD_DIM) + RMS_EPS) * g

    hw = MOBA_HEADS * HEAD_DIM
    pb = _mm(hb, w_ref[:, OFF_B:OFF_C])
    qb_ref[0] = segnorm(pb[:, 0:hw], qnb_ref[...]).astype(BF16)
    kb_ref[0] = segnorm(pb[:, hw:2 * hw], knb_ref[...]).astype(BF16)
    vb_ref[0] = pb[:, 2 * hw:3 * hw].astype(BF16)
    pc = _mm(hb, w_ref[:, OFF_C:OFF_IQ])
    qc_ref[0] = segnorm(pc[:, 0:hw], qnc_ref[...]).astype(BF16)
    kc_ref[0] = segnorm(pc[:, hw:2 * hw], knc_ref[...]).astype(BF16)
    vc_ref[0] = pc[:, 2 * hw:3 * hw].T.astype(BF16)

    pi = _mm(hb, w_ref[:, OFF_IQ:OFF_G])
    iq_ref[0] = pi[:, 0:OFF_IK - OFF_IQ].astype(BF16)
    ik_ref[0] = pi[:, OFF_IK - OFF_IQ:OFF_IW - OFF_IQ].astype(BF16)
    iw_ref[0] = pi[:, OFF_IW - OFF_IQ:OFF_G - OFF_IQ].T
    gt_ref[0] = jax.nn.sigmoid(_mm(hb, w_ref[:, OFF_G:N_ALL])).astype(BF16)


def _inproj(x, mod, lw, rope_tabs, bd, tm):
    B, S, D = x.shape
    hw = MOBA_HEADS * HEAD_DIM
    row = lambda w: pl.BlockSpec((1, tm, w), lambda b, i: (b, i, 0))
    tab = pl.BlockSpec((tm, LANES), lambda b, i: (i, 0))
    consts = [lw["n1"], lw["w_all"], lw["g_cq"], lw["w_uq"], lw["g_ckv"], lw["w_k"], lw["w_v"],
              lw["qn_a"], lw["kn_a"]]
    consts2 = [bd, lw["qn_b"], lw["kn_b"], lw["qn_c"], lw["kn_c"]]
    widths = [MLA_HEADS * LANES, MLA_HEADS * LANES, MLA_HEADS * MLA_V, hw, hw, hw, hw, hw, hw,
              IDX_HEADS * IDX_DIM, LANES, LANES, 3 * D_MODEL]
    dtypes = [BF16] * 11 + [F32, BF16]
    transposed = [False] * 8 + [True, False, False, True, False]
    col = lambda w: pl.BlockSpec((1, w, tm), lambda b, i: (b, 0, i))
    return pl.pallas_call(
        _inproj_kernel,
        grid=(B, S // tm),
        in_specs=[row(D), pl.BlockSpec((1, 6, D), lambda b, i: (b, 0, 0))]
        + [_const_spec(c.shape) for c in consts] + [tab, tab, tab]
        + [_const_spec(c.shape) for c in consts2],
        out_specs=[col(w) if tr else row(w) for w, tr in zip(widths, transposed)],
        out_shape=[jax.ShapeDtypeStruct((B, w, S) if tr else (B, S, w), dt)
                   for w, dt, tr in zip(widths, dtypes, transposed)],
        compiler_params=_params("arbitrary", "arbitrary"),
        name="in_proj",
    )(x, mod, *consts, *rope_tabs, *consts2)


NEG_INF = float("-inf")


def _fold_lanes(x, op):
    out = x[:, 0:LANES]
    for t in range(1, x.shape[1] // LANES):
        out = op(out, x[:, LANES * t:LANES * (t + 1)])
    return out


def _by_tile(i, n, body):
    for c in range(n):
        pl.when(i == c)(functools.partial(body, c))


def _softmax_pv(s_ref, v_ref, n_chunks, ch, m_acc):
    m = jnp.max(m_acc, axis=-1, keepdims=True)
    l_acc = None
    acc = None
    for j in range(n_chunks):
        p = jnp.exp(s_ref[:, j * ch:(j + 1) * ch] - m)
        lj = _fold_lanes(p, jnp.add)
        pv = _mm(p.astype(BF16), v_ref[0, j * ch:(j + 1) * ch, :])
        l_acc = lj if l_acc is None else l_acc + lj
        acc = pv if acc is None else acc + pv
    return acc / jnp.sum(l_acc, axis=-1, keepdims=True)


def _mla_kernel(q_ref, k_ref, v_ref, o_ref, s_ref, *, tq, nq):
    i = pl.program_id(2)
    row = lax.broadcasted_iota(jnp.int32, (tq, tq), 0)
    col = lax.broadcasted_iota(jnp.int32, (tq, tq), 1)
    causal = col <= row
    lane = lax.broadcasted_iota(jnp.int32, (tq, LANES), 1)

    def body(c):
        outs = []
        for hh in range(2):
            sl = slice(LANES * hh, LANES * (hh + 1))
            q = q_ref[0, :, sl]
            m_acc = jnp.full((tq, LANES), NEG_INF, F32)
            for j in range(c + 1):
                s = _nt(q, k_ref[0, j * tq:(j + 1) * tq, sl])
                if j == c:
                    s = jnp.where(causal, s, NEG_INF)
                s_ref[hh, :, j * tq:(j + 1) * tq] = s
                m_acc = jnp.maximum(m_acc, _fold_lanes(s, jnp.maximum))
            outs.append(_softmax_pv(s_ref.at[hh], v_ref, c + 1, tq, m_acc))
        o_ref[0] = jnp.where(lane < MLA_V, outs[0], outs[1]).astype(BF16)

    _by_tile(i, nq, body)


def _mla_attention(qa, ka, va, tq):
    B, S, _ = qa.shape
    pairs = MLA_HEADS // 2
    nq = S // tq
    return pl.pallas_call(
        functools.partial(_mla_kernel, tq=tq, nq=nq),
        grid=(B, pairs, nq),
        in_specs=[
            pl.BlockSpec((1, tq, 2 * LANES), lambda b, h, i: (b, i, h)),
            pl.BlockSpec((1, S, 2 * LANES), lambda b, h, i: (b, 0, h)),
            pl.BlockSpec((1, S, LANES), lambda b, h, i: (b, 0, h)),
        ],
        out_specs=pl.BlockSpec((1, tq, LANES), lambda b, h, i: (b, i, h)),
        out_shape=jax.ShapeDtypeStruct((B, S, MLA_HEADS * MLA_V), BF16),
        scratch_shapes=[pltpu.VMEM((2, tq, S), F32)],
        compiler_params=_params("arbitrary", "arbitrary", "arbitrary"),
        name="mla_attn",
    )(qa, ka, va)


def _moba_kernel(q_ref, k_ref, v_ref, o_ref, km_ref, s_ref, *, nb, n_sel, slopes):
    i = pl.program_id(1)
    blk = MOBA_BLOCK
    hw = MOBA_HEADS * HEAD_DIM
    nbp = km_ref.shape[0] // (3 * MOBA_HEADS)
    lane = lax.broadcasted_iota(jnp.int32, (1, hw), 1)

    @pl.when(i == 0)
    def _():
        means = [jnp.mean(k_ref[0, n * blk:(n + 1) * blk, :].astype(F32), axis=0, keepdims=True)
                 for n in range(nb)]
        km = jnp.concatenate(means + [jnp.zeros((1, hw), F32)] * (nbp - nb), axis=0)
        pieces = []
        for h in range(MOBA_HEADS):
            hm = (lane >= HEAD_DIM * h) & (lane < HEAD_DIM * (h + 1))
            pieces += [p.astype(F32) for p in _split3(jnp.where(hm, km, 0.0))]
        km_ref[...] = jnp.concatenate(pieces, axis=0).astype(BF16)

    row = lax.broadcasted_iota(jnp.int32, (blk, blk), 0)
    col = lax.broadcasted_iota(jnp.int32, (blk, blk), 1)
    causal = col <= row
    kcol = lax.broadcasted_iota(jnp.int32, (1, blk), 1)
    bidx = lax.broadcasted_iota(jnp.int32, (nbp, blk), 0)

    def body(c):
        q = q_ref[0]
        past = bidx < c
        out = jnp.zeros((blk, hw), F32)
        if c > 0:
            g_all = _nt(km_ref[...], q)
        for h in range(MOBA_HEADS):
            hm = (lane >= HEAD_DIM * h) & (lane < HEAD_DIM * (h + 1))
            qh = jnp.where(hm, q, jnp.zeros_like(q))
            if c > 0:
                g = sum(g_all[(3 * h + p) * nbp:(3 * h + p + 1) * nbp, :] for p in range(3))
                g = jnp.where(past, g, NEG_INF)
                rank = jnp.zeros((nbp, blk), F32)
                for n2 in range(c):
                    gm = g[n2:n2 + 1, :]
                    beats = (gm > g) | ((gm == g) & (bidx > n2))
                    rank = rank + jnp.where(beats, 1.0, 0.0)
                drop_t = jnp.where(past & (rank < n_sel), 0.0, NEG_INF)
                drop_t = jnp.concatenate([drop_t, jnp.zeros((LANES - nbp, blk), F32)], axis=0)
                drop = drop_t.T
            m_acc = jnp.full((blk, LANES), NEG_INF, F32)
            for j in range(c + 1):
                kbias = slopes[h] * (kcol + (j - c) * blk).astype(F32)
                s = _nt(qh, k_ref[0, j * blk:(j + 1) * blk, :]) + kbias
                if j == c:
                    s = jnp.where(causal, s, NEG_INF)
                else:
                    s = s + drop[:, j:j + 1]
                s_ref[h, :, j * blk:(j + 1) * blk] = s
                m_acc = jnp.maximum(m_acc, _fold_lanes(s, jnp.maximum))
            out = jnp.where(hm, _softmax_pv(s_ref.at[h], v_ref, c + 1, blk, m_acc), out)
        o_ref[0] = out.astype(BF16)

    _by_tile(i, nb, body)


def _moba_attention(qb, kb, vb, slopes):
    B, S, hw = qb.shape
    blk = MOBA_BLOCK
    nb = S // blk
    nbp = -(-nb // 8) * 8
    n_sel = max(1, min(MOBA_TOPK, nb - 1))
    full = pl.BlockSpec((1, S, hw), lambda b, i: (b, 0, 0))
    tile = pl.BlockSpec((1, blk, hw), lambda b, i: (b, i, 0))
    return pl.pallas_call(
        functools.partial(_moba_kernel, nb=nb, n_sel=n_sel, slopes=slopes),
        grid=(B, nb),
        in_specs=[tile, full, full],
        out_specs=tile,
        out_shape=jax.ShapeDtypeStruct((B, S, hw), BF16),
        scratch_shapes=[pltpu.VMEM((3 * MOBA_HEADS * nbp, hw), BF16), pltpu.VMEM((MOBA_HEADS, blk, S), F32)],
        compiler_params=_params("arbitrary", "arbitrary"),
        name="moba_attn",
    )(qb, kb, vb)


INT_MIN = -2 ** 31


def _fold_rows(x, op, n=8):
    out = x[0:n, :]
    for t in range(1, x.shape[0] // n):
        out = op(out, x[n * t:n * (t + 1), :])
    return out


def _kth_largest_key(key_ref, hi_ref, lo_ref, rows, n_chunks, tq, k):
    i16, i32 = jnp.int16, jnp.int32
    one, none = jnp.int16(1), jnp.int16(0)

    def count(ref, pred):
        acc = None
        for j in range(n_chunks):
            f = _fold_rows(jnp.where(pred(ref[rows(j), :]), one, none), jnp.add, 16)
            acc = f if acc is None else acc + f
        return jnp.sum(acc.astype(i32), axis=0, keepdims=True)

    zero = jnp.zeros((1, tq), i32)
    hi = jnp.where(count(hi_ref, lambda v: v >= none) >= k, zero, i32(-2 ** 15))

    def step_hi(it, hi):
        cand = hi | lax.shift_left(i32(1), i32(14) - it)
        c16 = cand.astype(i16)
        return jnp.where(count(hi_ref, lambda v: v >= c16) >= k, cand, hi)

    hi = lax.fori_loop(0, 15, step_hi, hi)
    h16 = hi.astype(i16)
    for j in range(n_chunks):
        low = (key_ref[rows(j), :] & i32(0xFFFF)) - i32(2 ** 15)
        lo_ref[rows(j), :] = jnp.where(hi_ref[rows(j), :] == h16, low.astype(i16), i16(-2 ** 15))
    above = count(hi_ref, lambda v: v > h16)

    def step_lo(it, lo):
        cand = lo | lax.shift_left(i32(1), i32(15) - it)
        c16 = (cand - i32(2 ** 15)).astype(i16)
        return jnp.where(above + count(lo_ref, lambda v: v >= c16) >= k, cand, lo)

    lo = lax.fori_loop(0, 16, step_lo, zero)
    return lax.shift_left(hi, 16) | lo


def _dsa_kernel(q_ref, k_ref, vt_ref, iq_ref, ik_ref, iwt_ref, kpos_ref, o_ref, key_ref, hi_ref, lo_ref,
                mb_ref, s_ref, *, tq, ch, per, ncls, n_keep, slopes, idx_scale):
    i = pl.program_id(1)
    hw = DSA_HEADS * HEAD_DIM
    q0 = i * tq
    qpos = q0 + lax.broadcasted_iota(jnp.int32, (1, tq), 1)
    krow = lax.broadcasted_iota(jnp.int32, (ch, 1), 0)
    lane = lax.broadcasted_iota(jnp.int32, (1, LANES), 1)
    lane_h = lax.broadcasted_iota(jnp.int32, (1, hw), 1)
    grp = LANES // IDX_DIM

    def body(c):
        n_chunks = (c + 1) * per * tq // ch
        rows = lambda j: slice(j * ch, (j + 1) * ch)
        iwt = iwt_ref[0]
        qms = []
        for hh in range(IDX_HEADS):
            g, r = divmod(hh, grp)
            iqg = iq_ref[0, :, LANES * g:LANES * (g + 1)]
            qms.append(jnp.where((lane >= IDX_DIM * r) & (lane < IDX_DIM * (r + 1)), iqg, jnp.zeros_like(iqg)))
        for j in range(n_chunks):
            ikc = ik_ref[0, rows(j), :]
            score = None
            for hh in range(IDX_HEADS):
                term = jnp.maximum(_nt(ikc, qms[hh]), 0.0) * iwt[hh:hh + 1, :]
                score = term if score is None else score + term
            score = jnp.where(krow + j * ch <= qpos, score * idx_scale, NEG_INF)
            bits = lax.bitcast_convert_type(score, jnp.int32)
            key = jnp.where(bits >= 0, bits, bits ^ jnp.int32(0x7FFFFFFF))
            key_ref[rows(j), :] = key
            hi_ref[rows(j), :] = lax.shift_right_arithmetic(key, 16).astype(jnp.int16)

        if n_chunks * ch <= n_keep:
            thr = jnp.full((1, tq), INT_MIN, jnp.int32)
        else:
            thr = _kth_largest_key(key_ref, hi_ref, lo_ref, rows, n_chunks, tq, n_keep)
        for j in range(n_chunks):
            keep = (key_ref[rows(j), :] >= thr) & (krow + j * ch <= qpos)
            mb_ref[rows(j), :] = jnp.where(keep, 0.0, NEG_INF)

        q = q_ref[0]
        q0f = q0.astype(F32)
        outs = []
        for h in range(DSA_HEADS):
            hm = (lane_h >= HEAD_DIM * h) & (lane_h < HEAD_DIM * (h + 1))
            qh = jnp.where(hm, q, jnp.zeros_like(q))
            m8 = jnp.full((8, tq), NEG_INF, F32)
            for j in range(n_chunks):
                kb = slopes[h] * (kpos_ref[rows(j), :] - q0f)
                kb = jnp.concatenate([kb] * (tq // LANES), axis=1)
                s = _nt(k_ref[0, rows(j), :], qh) + kb + mb_ref[rows(j), :]
                s_ref[h, rows(j), :] = s
                m8 = jnp.maximum(m8, _fold_rows(s, jnp.maximum))
            m = jnp.max(m8, axis=0, keepdims=True)
            l8 = None
            acc = None
            for j in range(n_chunks):
                p = jnp.exp(s_ref[h, rows(j), :] - m)
                f = _fold_rows(p, jnp.add)
                pv = _mm(vt_ref[0, HEAD_DIM * h:HEAD_DIM * (h + 1), rows(j)], p.astype(BF16))
                l8 = f if l8 is None else l8 + f
                acc = pv if acc is None else acc + pv
            outs.append(acc / jnp.sum(l8, axis=0, keepdims=True))
        o_ref[0] = jnp.concatenate(outs, axis=0).T.astype(BF16)

    for c in range(ncls):
        pl.when((i >= c * per) & (i < (c + 1) * per))(functools.partial(body, c))


def _dsa_attention(qc, kc, vct, iq, ik, iwt, slopes, tq, ncls):
    B, S, hw = qc.shape
    nq = S // tq
    per = nq // ncls
    ch = tq
    n_keep = min(DSA_TOPK, S // 4)
    kpos = jnp.broadcast_to(jnp.arange(S, dtype=F32)[:, None], (S, LANES))
    full = lambda w: pl.BlockSpec((1, S, w), lambda b, i: (b, 0, 0))
    tile = lambda w: pl.BlockSpec((1, tq, w), lambda b, i: (b, i, 0))
    return pl.pallas_call(
        functools.partial(_dsa_kernel, tq=tq, ch=ch, per=per, ncls=ncls, n_keep=n_keep, slopes=slopes,
                          idx_scale=(IDX_DIM * IDX_HEADS) ** -0.5),
        grid=(B, nq),
        in_specs=[tile(hw), full(hw), pl.BlockSpec((1, hw, S), lambda b, i: (b, 0, 0)),
                  tile(IDX_HEADS * IDX_DIM), full(LANES), pl.BlockSpec((1, LANES, tq), lambda b, i: (b, 0, i)),
                  pl.BlockSpec((S, LANES), lambda b, i: (0, 0))],
        out_specs=tile(hw),
        out_shape=jax.ShapeDtypeStruct((B, S, hw), BF16),
        scratch_shapes=[pltpu.VMEM((S, tq), jnp.int32), pltpu.VMEM((S, tq), jnp.int16),
                        pltpu.VMEM((S, tq), jnp.int16), pltpu.VMEM((S, tq), F32),
                        pltpu.VMEM((DSA_HEADS, S, tq), F32)],
        compiler_params=_params("arbitrary", "arbitrary"),
        name="dsa_attn",
    )(qc, kc, vct, iq, ik, iwt, kpos)


def _merge_kernel(ya_ref, yb_ref, yc_ref, gt_ref, x_ref, mod_ref, wa_ref, wb_ref, wc_ref, wo_ref, o_ref):
    d = D_MODEL
    merged = gt_ref[0, :, 0:d].astype(F32) * _mm(ya_ref[0], wa_ref[...])
    merged = merged + gt_ref[0, :, d:2 * d].astype(F32) * _mm(yb_ref[0], wb_ref[...])
    merged = merged + gt_ref[0, :, 2 * d:3 * d].astype(F32) * _mm(yc_ref[0], wc_ref[...])
    o_ref[0] = x_ref[0] + mod_ref[0, 2:3, :] * _mm(merged.astype(BF16), wo_ref[...])


def _merge(ya, yb, yc, gt, x, mod, lw, tm):
    B, S, D = x.shape
    row = lambda w: pl.BlockSpec((1, tm, w), lambda b, i: (b, i, 0))
    consts = [lw["w_br_a"], lw["w_br_b"], lw["w_br_c"], lw["w_out"]]
    return pl.pallas_call(
        _merge_kernel,
        grid=(B, S // tm),
        in_specs=[row(ya.shape[-1]), row(yb.shape[-1]), row(yc.shape[-1]), row(3 * D), row(D),
                  pl.BlockSpec((1, 6, D), lambda b, i: (b, 0, 0))] + [_const_spec(c.shape) for c in consts],
        out_specs=row(D),
        out_shape=jax.ShapeDtypeStruct((B, S, D), F32),
        compiler_params=_params("arbitrary", "arbitrary"),
        name="merge_out",
    )(ya, yb, yc, gt, x, mod, *consts)


def _ffn_kernel(x_ref, mod_ref, n2_ref, wg_ref, wu_ref, wd_ref, o_ref, h_ref, acc_ref):
    j = pl.program_id(2)

    @pl.when(j == 0)
    def _():
        h = _rms(x_ref[0], n2_ref[...]) * (1.0 + mod_ref[0, 4:5, :]) + mod_ref[0, 3:4, :]
        h_ref[...] = h.astype(BF16)
        acc_ref[...] = jnp.zeros_like(acc_ref)

    h = h_ref[...]
    g = _mm(h, wg_ref[...])
    u = _mm(h, wu_ref[...])
    act = (g * jax.nn.sigmoid(g) * u).astype(BF16)
    acc_ref[...] += _mm(act, wd_ref[...])

    @pl.when(j == pl.num_programs(2) - 1)
    def _():
        o_ref[0] = x_ref[0] + mod_ref[0, 5:6, :] * acc_ref[...]


def _ffn(x, mod, lw, tm, tf):
    B, S, D = x.shape
    F = lw["w_g"].shape[1]
    row = pl.BlockSpec((1, tm, D), lambda b, i, j: (b, i, 0))
    return pl.pallas_call(
        _ffn_kernel,
        grid=(B, S // tm, F // tf),
        in_specs=[row, pl.BlockSpec((1, 6, D), lambda b, i, j: (b, 0, 0)),
                  pl.BlockSpec((1, D), lambda b, i, j: (0, 0)),
                  pl.BlockSpec((D, tf), lambda b, i, j: (0, j)),
                  pl.BlockSpec((D, tf), lambda b, i, j: (0, j)),
                  pl.BlockSpec((tf, D), lambda b, i, j: (j, 0))],
        out_specs=row,
        out_shape=jax.ShapeDtypeStruct((B, S, D), F32),
        scratch_shapes=[pltpu.VMEM((tm, D), BF16), pltpu.VMEM((tm, D), F32)],
        compiler_params=_params("arbitrary", "arbitrary", "arbitrary"),
        name="swiglu",
    )(x, mod, lw["n2"], lw["w_g"], lw["w_u"], lw["w_d"])


def _pad_cols(w, width):
    return jnp.pad(w, ((0, 0), (0, width - w.shape[1])))


def _layer_weights(l, norm1, w_in, g_cq, w_uq, g_ckv, w_ukv, qn_mla, kn_mla, qn_moba, kn_moba, qn_dsa,
                   kn_dsa, w_br_a, w_br_b, w_br_c, w_out, norm2, w_gu, w_down):
    D = D_MODEL
    w = w_in[l]
    o = 0
    cols = {}
    for name, wd in (("cq", MLA_Q_RANK), ("ckv", MLA_KV_RANK), ("kr", MLA_ROPE),
                     ("b", 3 * MOBA_HEADS * HEAD_DIM), ("c", 3 * DSA_HEADS * HEAD_DIM),
                     ("iq", IDX_HEADS * IDX_DIM), ("ik", IDX_DIM), ("iw", IDX_HEADS), ("g", 3 * D)):
        cols[name] = w[:, o:o + wd]
        o += wd
    kr_slot = jnp.pad(cols["kr"], ((0, 0), (MLA_NOPE, LANES - MLA_QK)))
    ik_rep = jnp.tile(cols["ik"], (1, LANES // IDX_DIM))
    w_all = jnp.concatenate([cols["cq"], cols["ckv"], kr_slot, cols["b"], cols["c"], cols["iq"], ik_rep,
                             _pad_cols(cols["iw"], LANES), cols["g"]], axis=1).astype(BF16)
    wuq = jnp.pad(w_uq[l].reshape(MLA_Q_RANK, MLA_HEADS, MLA_QK), ((0, 0), (0, 0), (0, LANES - MLA_QK)))
    wukv = w_ukv[l].reshape(MLA_KV_RANK, MLA_HEADS, MLA_NOPE + MLA_V)
    wk = jnp.pad(wukv[:, :, :MLA_NOPE], ((0, 0), (0, 0), (0, LANES - MLA_NOPE)))
    wv = wukv[:, :, MLA_NOPE:]
    row = lambda v: v.reshape(1, -1)
    return {
        "n1": row(norm1[l]), "w_all": w_all, "g_cq": row(g_cq[l]),
        "w_uq": wuq.reshape(MLA_Q_RANK, MLA_HEADS * LANES).astype(BF16), "g_ckv": row(g_ckv[l]),
        "w_k": wk.reshape(MLA_KV_RANK, MLA_HEADS * LANES).astype(BF16),
        "w_v": wv.reshape(MLA_KV_RANK, MLA_HEADS * MLA_V).astype(BF16),
        "qn_a": row(jnp.pad(qn_mla[l] * MLA_QK ** -0.5, (0, LANES - MLA_QK))), "kn_a": row(jnp.pad(kn_mla[l], (0, LANES - MLA_QK))),
        "qn_b": row(jnp.tile(qn_moba[l] * HEAD_DIM ** -0.5, MOBA_HEADS)), "kn_b": row(jnp.tile(kn_moba[l], MOBA_HEADS)),
        "qn_c": row(jnp.tile(qn_dsa[l] * HEAD_DIM ** -0.5, DSA_HEADS)), "kn_c": row(jnp.tile(kn_dsa[l], DSA_HEADS)),
        "w_br_a": w_br_a[l].astype(BF16), "w_br_b": w_br_b[l].astype(BF16), "w_br_c": w_br_c[l].astype(BF16),
        "w_out": w_out[l].astype(BF16), "n2": row(norm2[l]),
        "w_g": w_gu[l][:, :D_FF].astype(BF16), "w_u": w_gu[l][:, D_FF:].astype(BF16),
        "w_d": w_down[l].astype(BF16),
    }


def _rope_tables(S):
    half = MLA_ROPE // 2
    freqs = ROPE_THETA ** (-jnp.arange(half, dtype=F32) / half)
    ang = jnp.arange(S, dtype=F32)[:, None] * freqs[None, :]
    cos, sin = jnp.cos(ang), jnp.sin(ang)
    zeros = lambda n: jnp.zeros((S, n), F32)
    tail = LANES - MLA_QK
    rc = jnp.concatenate([jnp.ones((S, MLA_NOPE), F32), cos, cos, jnp.ones((S, tail), F32)], axis=1)
    rs1 = jnp.concatenate([zeros(MLA_NOPE), -sin, zeros(half + tail)], axis=1)
    rs2 = jnp.concatenate([zeros(MLA_NOPE + half), sin, zeros(tail)], axis=1)
    return rc, rs1, rs2


def _tiles(S):
    dsa_q = min(256, S)
    return {"proj": min(256, S), "out": min(512, S), "mla_q": min(256, S), "dsa_q": dsa_q,
            "dsa_cls": S // dsa_q}


def kernel(x, c, w_ada, b_ada, norm1, w_in, g_cq, w_uq, g_ckv, w_ukv, qn_mla, kn_mla, qn_moba, kn_moba, qn_dsa, kn_dsa, w_br_a, w_br_b, w_br_c, w_out, norm2, w_gu, w_down):
    B, S, D = x.shape
    L = w_ada.shape[0]
    assert D == D_MODEL and S % MOBA_BLOCK == 0
    n_slopes = MOBA_HEADS + DSA_HEADS
    slopes = [2.0 ** (-8.0 * (i + 1) / n_slopes) for i in range(n_slopes)]
    mod_all = _ada(c, w_ada.astype(BF16), b_ada).reshape(L, B, 6, D)
    rope_tabs = _rope_tables(S)
    seg = jnp.arange(MOBA_HEADS * HEAD_DIM) // HEAD_DIM
    bd = (seg[:, None] == seg[None, :]).astype(BF16)
    t = _tiles(S)
    for l in range(L):
        lw = _layer_weights(l, norm1, w_in, g_cq, w_uq, g_ckv, w_ukv, qn_mla, kn_mla, qn_moba, kn_moba,
                            qn_dsa, kn_dsa, w_br_a, w_br_b, w_br_c, w_out, norm2, w_gu, w_down)
        mod = mod_all[l]
        qa, ka, va, qb, kb, vb, qc, kc, vc, iq, ik, iw, gt = _inproj(x, mod, lw, rope_tabs, bd, t["proj"])
        ya = _mla_attention(qa, ka, va, tq=t["mla_q"])
        yb = _moba_attention(qb, kb, vb, tuple(slopes[0::2]))
        yc = _dsa_attention(qc, kc, vc, iq, ik, iw, tuple(slopes[1::2]), tq=t["dsa_q"], ncls=t["dsa_cls"])
        x = _merge(ya, yb, yc, gt, x, mod, lw, t["out"])
        x = _ffn(x, mod, lw, t["out"], D_FF // 2)
    return x
```

```python
import functools

import jax
import jax.numpy as jnp
from jax import lax
from jax.experimental import pallas as pl
from jax.experimental.pallas import tpu as pltpu

F32 = jnp.float32
BF16 = jnp.bfloat16

D_MODEL = 1024
HEAD_DIM = 64
RMS_EPS = 1e-6
MLA_HEADS = 8
MLA_NOPE = 64
MLA_ROPE = 32
MLA_QK = MLA_NOPE + MLA_ROPE
MLA_V = 64
MLA_Q_RANK = 768
MLA_KV_RANK = 256
ROPE_THETA = 10000.0
MOBA_HEADS = 4
MOBA_BLOCK = 256
MOBA_TOPK = 3
DSA_HEADS = 4
IDX_HEADS = 16
IDX_DIM = 32
DSA_TOPK = 256
D_FF = 2816

LANES = 128
VMEM_LIMIT = 56 * 1024 * 1024

OFF_CQ = 0
OFF_CKV = OFF_CQ + MLA_Q_RANK
OFF_KR = OFF_CKV + MLA_KV_RANK
OFF_B = OFF_KR + LANES
OFF_C = OFF_B + 3 * MOBA_HEADS * HEAD_DIM
OFF_IQ = OFF_C + 3 * DSA_HEADS * HEAD_DIM
OFF_IK = OFF_IQ + IDX_HEADS * IDX_DIM
OFF_IW = OFF_IK + LANES
OFF_G = OFF_IW + LANES
N_ALL = OFF_G + 3 * D_MODEL


def _nt(a, b):
    return lax.dot_general(a, b, (((1,), (1,)), ((), ())), preferred_element_type=F32)


def _mm(a, b):
    return jnp.dot(a, b, preferred_element_type=F32)


def _split3(x):
    hi = x.astype(BF16)
    r = x - hi.astype(F32)
    mid = r.astype(BF16)
    lo = (r - mid.astype(F32)).astype(BF16)
    return hi, mid, lo


def _rms(x, g):
    return x * lax.rsqrt(jnp.mean(x * x, axis=-1, keepdims=True) + RMS_EPS) * g


def _const_spec(shape):
    nd = len(shape)
    return pl.BlockSpec(shape, lambda *_: (0,) * nd, pipeline_mode=pl.Buffered(1))


def _params(*sem):
    return pltpu.CompilerParams(dimension_semantics=sem, vmem_limit_bytes=VMEM_LIMIT)


def _ada_kernel(c_ref, w_ref, b_ref, o_ref):
    c = c_ref[...]
    cond = (c * jax.nn.sigmoid(c)).astype(BF16)
    o_ref[0] = _mm(cond, w_ref[0]) + b_ref[0]


def _ada(c, w_ada, b_ada):
    L, D, N = w_ada.shape
    B = c.shape[0]
    tn = 1024
    return pl.pallas_call(
        _ada_kernel,
        grid=(L, N // tn),
        in_specs=[
            pl.BlockSpec((B, D), lambda l, n: (0, 0)),
            pl.BlockSpec((1, D, tn), lambda l, n: (l, 0, n)),
            pl.BlockSpec((1, 1, tn), lambda l, n: (l, 0, n)),
        ],
        out_specs=pl.BlockSpec((1, B, tn), lambda l, n: (l, 0, n)),
        out_shape=jax.ShapeDtypeStruct((L, B, N), F32),
        compiler_params=_params("arbitrary", "arbitrary"),
        name="ada_mod",
    )(c, w_ada, b_ada.reshape(L, 1, N))


def _rope(x, c, s1, s2):
    return x * c + pltpu.roll(x, LANES - 16, 1) * s1 + pltpu.roll(x, 16, 1) * s2


def _inproj_kernel(x_ref, mod_ref, n1_ref, w_ref, gcq_ref, wuq_ref, gckv_ref, wk_ref, wv_ref,
                   qna_ref, kna_ref, rc_ref, rs1_ref, rs2_ref, bd_ref,
                   qnb_ref, knb_ref, qnc_ref, knc_ref,
                   qa_ref, ka_ref, va_ref, qb_ref, kb_ref, vb_ref, qc_ref, kc_ref, vc_ref,
                   iq_ref, ik_ref, iw_ref, gt_ref):
    x = x_ref[0]
    sh1 = mod_ref[0, 0:1, :]
    sc1 = mod_ref[0, 1:2, :]
    hb = (_rms(x, n1_ref[...]) * (1.0 + sc1) + sh1).astype(BF16)

    a = _mm(hb, w_ref[:, OFF_CQ:OFF_B])
    cqn = _rms(a[:, OFF_CQ:OFF_CKV], gcq_ref[...]).astype(BF16)
    ckvn = _rms(a[:, OFF_CKV:OFF_KR], gckv_ref[...]).astype(BF16)
    kr = a[:, OFF_KR:OFF_B]
    qraw = _mm(cqn, wuq_ref[...])
    kraw = _mm(ckvn, wk_ref[...])
    va_ref[0] = _mm(ckvn, wv_ref[...]).astype(BF16)
    rc, rs1, rs2 = rc_ref[...], rs1_ref[...], rs2_ref[...]
    qna, kna = qna_ref[...], kna_ref[...]
    inv_qk = 1.0 / MLA_QK
    for h in range(MLA_HEADS):
        sl = slice(LANES * h, LANES * (h + 1))
        qh = qraw[:, sl]
        qh = qh * lax.rsqrt(jnp.sum(qh * qh, axis=-1, keepdims=True) * inv_qk + RMS_EPS) * qna
        qa_ref[0, :, sl] = _rope(qh, rc, rs1, rs2).astype(BF16)
        kh = kraw[:, sl] + kr
        kh = kh * lax.rsqrt(jnp.sum(kh * kh, axis=-1, keepdims=True) * inv_qk + RMS_EPS) * kna
        ka_ref[0, :, sl] = _rope(kh, rc, rs1, rs2).astype(BF16)

    bd = bd_ref[...]

    def segnorm(v, g):
        ss = sum(_mm(part, bd) for part in _split3(v * v))
        return v * lax.rsqrt(ss * (1.0 / HEAD_DIM) + RMS_EPS) * g

    hw = MOBA_HEADS * HEAD_DIM
    pb = _mm(hb, w_ref[:, OFF_B:OFF_C])
    qb_ref[0] = segnorm(pb[:, 0:hw], qnb_ref[...]).astype(BF16)
    kb_ref[0] = segnorm(pb[:, hw:2 * hw], knb_ref[...]).astype(BF16)
    vb_ref[0] = pb[:, 2 * hw:3 * hw].astype(BF16)
    pc = _mm(hb, w_ref[:, OFF_C:OFF_IQ])
    qc_ref[0] = segnorm(pc[:, 0:hw], qnc_ref[...]).astype(BF16)
    kc_ref[0] = segnorm(pc[:, hw:2 * hw], knc_ref[...]).astype(BF16)
    vc_ref[0] = pc[:, 2 * hw:3 * hw].T.astype(BF16)

    pi = _mm(hb, w_ref[:, OFF_IQ:OFF_G])
    iq_ref[0] = pi[:, 0:OFF_IK - OFF_IQ].astype(BF16)
    ik_ref[0] = pi[:, OFF_IK - OFF_IQ:OFF_IW - OFF_IQ].astype(BF16)
    iw_ref[0] = pi[:, OFF_IW - OFF_IQ:OFF_G - OFF_IQ].T
    gt_ref[0] = jax.nn.sigmoid(_mm(hb, w_ref[:, OFF_G:N_ALL])).astype(BF16)


def _inproj(x, mod, lw, rope_tabs, bd, tm):
    B, S, D = x.shape
    hw = MOBA_HEADS * HEAD_DIM
    row = lambda w: pl.BlockSpec((1, tm, w), lambda b, i: (b, i, 0))
    tab = pl.BlockSpec((tm, LANES), lambda b, i: (i, 0))
    consts = [lw["n1"], lw["w_all"], lw["g_cq"], lw["w_uq"], lw["g_ckv"], lw["w_k"], lw["w_v"],
              lw["qn_a"], lw["kn_a"]]
    consts2 = [bd, lw["qn_b"], lw["kn_b"], lw["qn_c"], lw["kn_c"]]
    widths = [MLA_HEADS * LANES, MLA_HEADS * LANES, MLA_HEADS * MLA_V, hw, hw, hw, hw, hw, hw,
              IDX_HEADS * IDX_DIM, LANES, LANES, 3 * D_MODEL]
    dtypes = [BF16] * 11 + [F32, BF16]
    transposed = [False] * 8 + [True, False, False, True, False]
    col = lambda w: pl.BlockSpec((1, w, tm), lambda b, i: (b, 0, i))
    return pl.pallas_call(
        _inproj_kernel,
        grid=(B, S // tm),
        in_specs=[row(D), pl.BlockSpec((1, 6, D), lambda b, i: (b, 0, 0))]
        + [_const_spec(c.shape) for c in consts] + [tab, tab, tab]
        + [_const_spec(c.shape) for c in consts2],
        out_specs=[col(w) if tr else row(w) for w, tr in zip(widths, transposed)],
        out_shape=[jax.ShapeDtypeStruct((B, w, S) if tr else (B, S, w), dt)
                   for w, dt, tr in zip(widths, dtypes, transposed)],
        compiler_params=_params("arbitrary", "arbitrary"),
        name="in_proj",
    )(x, mod, *consts, *rope_tabs, *consts2)


NEG_INF = float("-inf")


def _fold_lanes(x, op):
    out = x[:, 0:LANES]
    for t in range(1, x.shape[1] // LANES):
        out = op(out, x[:, LANES * t:LANES * (t + 1)])
    return out


def _by_tile(i, n, body):
    for c in range(n):
        pl.when(i == c)(functools.partial(body, c))


def _softmax_pv(s_ref, v_ref, n_chunks, ch, m_acc):
    m = jnp.max(m_acc, axis=-1, keepdims=True)
    l_acc = None
    acc = None
    for j in range(n_chunks):
        p = jnp.exp(s_ref[:, j * ch:(j + 1) * ch] - m)
        lj = _fold_lanes(p, jnp.add)
        pv = _mm(p.astype(BF16), v_ref[0, j * ch:(j + 1) * ch, :])
        l_acc = lj if l_acc is None else l_acc + lj
        acc = pv if acc is None else acc + pv
    return acc / jnp.sum(l_acc, axis=-1, keepdims=True)


def _mla_kernel(q_ref, k_ref, v_ref, o_ref, s_ref, *, tq, nq):
    row = lax.broadcasted_iota(jnp.int32, (tq, tq), 0)
    col = lax.broadcasted_iota(jnp.int32, (tq, tq), 1)
    causal = col <= row
    lane = lax.broadcasted_iota(jnp.int32, (tq, LANES), 1)

    for c in range(nq):
        outs = []
        for hh in range(2):
            sl = slice(LANES * hh, LANES * (hh + 1))
            q = q_ref[0, c * tq:(c + 1) * tq, sl]
            rows = s_ref.at[c % 2, hh]
            m_acc = jnp.full((tq, LANES), NEG_INF, F32)
            for j in range(c + 1):
                s = _nt(q, k_ref[0, j * tq:(j + 1) * tq, sl])
                if j == c:
                    s = jnp.where(causal, s, NEG_INF)
                rows[:, j * tq:(j + 1) * tq] = s
                m_acc = jnp.maximum(m_acc, _fold_lanes(s, jnp.maximum))
            outs.append(_softmax_pv(rows, v_ref, c + 1, tq, m_acc))
        o_ref[0, c * tq:(c + 1) * tq, :] = jnp.where(lane < MLA_V, outs[0], outs[1]).astype(BF16)


def _mla_attention(qa, ka, va, tq):
    B, S, _ = qa.shape
    pairs = MLA_HEADS // 2
    nq = S // tq
    wide = pl.BlockSpec((1, S, 2 * LANES), lambda b, h: (b, 0, h))
    narrow = pl.BlockSpec((1, S, LANES), lambda b, h: (b, 0, h))
    return pl.pallas_call(
        functools.partial(_mla_kernel, tq=tq, nq=nq),
        grid=(B, pairs),
        in_specs=[wide, wide, narrow],
        out_specs=narrow,
        out_shape=jax.ShapeDtypeStruct((B, S, MLA_HEADS * MLA_V), BF16),
        scratch_shapes=[pltpu.VMEM((2, 2, tq, S), F32)],
        compiler_params=_params("arbitrary", "arbitrary"),
        name="mla_attn",
    )(qa, ka, va)


def _moba_kernel(q_ref, k_ref, v_ref, o_ref, km_ref, s_ref, *, nb, n_sel, slopes):
    i = pl.program_id(1)
    blk = MOBA_BLOCK
    hw = MOBA_HEADS * HEAD_DIM
    nbp = km_ref.shape[0] // (3 * MOBA_HEADS)
    lane = lax.broadcasted_iota(jnp.int32, (1, hw), 1)

    @pl.when(i == 0)
    def _():
        means = [jnp.mean(k_ref[0, n * blk:(n + 1) * blk, :].astype(F32), axis=0, keepdims=True)
                 for n in range(nb)]
        km = jnp.concatenate(means + [jnp.zeros((1, hw), F32)] * (nbp - nb), axis=0)
        pieces = []
        for h in range(MOBA_HEADS):
            hm = (lane >= HEAD_DIM * h) & (lane < HEAD_DIM * (h + 1))
            pieces += [p.astype(F32) for p in _split3(jnp.where(hm, km, 0.0))]
        km_ref[...] = jnp.concatenate(pieces, axis=0).astype(BF16)

    row = lax.broadcasted_iota(jnp.int32, (blk, blk), 0)
    col = lax.broadcasted_iota(jnp.int32, (blk, blk), 1)
    causal = col <= row
    kcol = lax.broadcasted_iota(jnp.int32, (1, blk), 1)
    bidx = lax.broadcasted_iota(jnp.int32, (nbp, blk), 0)

    def body(c):
        q = q_ref[0]
        past = bidx < c
        out = jnp.zeros((blk, hw), F32)
        if c > 0:
            g_all = _nt(km_ref[...], q)
        for h in range(MOBA_HEADS):
            hm = (lane >= HEAD_DIM * h) & (lane < HEAD_DIM * (h + 1))
            qh = jnp.where(hm, q, jnp.zeros_like(q))
            if c > 0:
                g = sum(g_all[(3 * h + p) * nbp:(3 * h + p + 1) * nbp, :] for p in range(3))
                g = jnp.where(past, g, NEG_INF)
                rank = jnp.zeros((nbp, blk), F32)
                for n2 in range(c):
                    gm = g[n2:n2 + 1, :]
                    beats = (gm > g) | ((gm == g) & (bidx > n2))
                    rank = rank + jnp.where(beats, 1.0, 0.0)
                drop_t = jnp.where(past & (rank < n_sel), 0.0, NEG_INF)
                drop_t = jnp.concatenate([drop_t, jnp.zeros((LANES - nbp, blk), F32)], axis=0)
                drop = drop_t.T
            m_acc = jnp.full((blk, LANES), NEG_INF, F32)
            for j in range(c + 1):
                kbias = slopes[h] * (kcol + (j - c) * blk).astype(F32)
                s = _nt(qh, k_ref[0, j * blk:(j + 1) * blk, :]) + kbias
                if j == c:
                    s = jnp.where(causal, s, NEG_INF)
                else:
                    s = s + drop[:, j:j + 1]
                s_ref[h, :, j * blk:(j + 1) * blk] = s
                m_acc = jnp.maximum(m_acc, _fold_lanes(s, jnp.maximum))
            out = jnp.where(hm, _softmax_pv(s_ref.at[h], v_ref, c + 1, blk, m_acc), out)
        o_ref[0] = out.astype(BF16)

    _by_tile(i, nb, body)


def _moba_attention(qb, kb, vb, slopes):
    B, S, hw = qb.shape
    blk = MOBA_BLOCK
    nb = S // blk
    nbp = -(-nb // 8) * 8
    n_sel = max(1, min(MOBA_TOPK, nb - 1))
    full = pl.BlockSpec((1, S, hw), lambda b, i: (b, 0, 0))
    tile = pl.BlockSpec((1, blk, hw), lambda b, i: (b, i, 0))
    return pl.pallas_call(
        functools.partial(_moba_kernel, nb=nb, n_sel=n_sel, slopes=slopes),
        grid=(B, nb),
        in_specs=[tile, full, full],
        out_specs=tile,
        out_shape=jax.ShapeDtypeStruct((B, S, hw), BF16),
        scratch_shapes=[pltpu.VMEM((3 * MOBA_HEADS * nbp, hw), BF16), pltpu.VMEM((MOBA_HEADS, blk, S), F32)],
        compiler_params=_params("arbitrary", "arbitrary"),
        name="moba_attn",
    )(qb, kb, vb)


INT_MIN = -2 ** 31


def _fold_rows(x, op, n=8):
    out = x[0:n, :]
    for t in range(1, x.shape[0] // n):
        out = op(out, x[n * t:n * (t + 1), :])
    return out


def _kth_largest_key(key_ref, hi_ref, lo_ref, rows, n_chunks, tq, k):
    i16, i32 = jnp.int16, jnp.int32
    one, none = jnp.int16(1), jnp.int16(0)

    def count(ref, pred):
        acc = None
        for j in range(n_chunks):
            f = _fold_rows(jnp.where(pred(ref[rows(j), :]), one, none), jnp.add, 16)
            acc = f if acc is None else acc + f
        return jnp.sum(acc.astype(i32), axis=0, keepdims=True)

    zero = jnp.zeros((1, tq), i32)
    hi = jnp.where(count(hi_ref, lambda v: v >= none) >= k, zero, i32(-2 ** 15))

    def step_hi(it, hi):
        cand = hi | lax.shift_left(i32(1), i32(14) - it)
        c16 = cand.astype(i16)
        return jnp.where(count(hi_ref, lambda v: v >= c16) >= k, cand, hi)

    hi = lax.fori_loop(0, 15, step_hi, hi)
    h16 = hi.astype(i16)
    for j in range(n_chunks):
        low = (key_ref[rows(j), :] & i32(0xFFFF)) - i32(2 ** 15)
        lo_ref[rows(j), :] = jnp.where(hi_ref[rows(j), :] == h16, low.astype(i16), i16(-2 ** 15))
    above = count(hi_ref, lambda v: v > h16)

    def step_lo(it, lo):
        cand = lo | lax.shift_left(i32(1), i32(15) - it)
        c16 = (cand - i32(2 ** 15)).astype(i16)
        return jnp.where(above + count(lo_ref, lambda v: v >= c16) >= k, cand, lo)

    lo = lax.fori_loop(0, 16, step_lo, zero)
    return lax.shift_left(hi, 16) | lo


def _dsa_kernel(q_ref, k_ref, vt_ref, iq_ref, ik_ref, iwt_ref, kpos_ref, o_ref, key_ref, hi_ref, lo_ref,
                mb_ref, s_ref, *, tq, ch, per, ncls, n_keep, slopes, idx_scale):
    i = pl.program_id(1)
    hw = DSA_HEADS * HEAD_DIM
    q0 = i * tq
    qpos = q0 + lax.broadcasted_iota(jnp.int32, (1, tq), 1)
    krow = lax.broadcasted_iota(jnp.int32, (ch, 1), 0)
    lane = lax.broadcasted_iota(jnp.int32, (1, LANES), 1)
    lane_h = lax.broadcasted_iota(jnp.int32, (1, hw), 1)
    grp = LANES // IDX_DIM

    def body(c):
        n_chunks = (c + 1) * per * tq // ch
        rows = lambda j: slice(j * ch, (j + 1) * ch)
        iwt = iwt_ref[0]
        qms = []
        for hh in range(IDX_HEADS):
            g, r = divmod(hh, grp)
            iqg = iq_ref[0, :, LANES * g:LANES * (g + 1)]
            qms.append(jnp.where((lane >= IDX_DIM * r) & (lane < IDX_DIM * (r + 1)), iqg, jnp.zeros_like(iqg)))
        for j in range(n_chunks):
            ikc = ik_ref[0, rows(j), :]
            score = None
            for hh in range(IDX_HEADS):
                term = jnp.maximum(_nt(ikc, qms[hh]), 0.0) * iwt[hh:hh + 1, :]
                score = term if score is None else score + term
            score = jnp.where(krow + j * ch <= qpos, score * idx_scale, NEG_INF)
            bits = lax.bitcast_convert_type(score, jnp.int32)
            key = jnp.where(bits >= 0, bits, bits ^ jnp.int32(0x7FFFFFFF))
            key_ref[rows(j), :] = key
            hi_ref[rows(j), :] = lax.shift_right_arithmetic(key, 16).astype(jnp.int16)

        if n_chunks * ch <= n_keep:
            thr = jnp.full((1, tq), INT_MIN, jnp.int32)
        else:
            thr = _kth_largest_key(key_ref, hi_ref, lo_ref, rows, n_chunks, tq, n_keep)
        for j in range(n_chunks):
            keep = (key_ref[rows(j), :] >= thr) & (krow + j * ch <= qpos)
            mb_ref[rows(j), :] = jnp.where(keep, 0.0, NEG_INF)

        q = q_ref[0]
        q0f = q0.astype(F32)
        outs = []
        for h in range(DSA_HEADS):
            hm = (lane_h >= HEAD_DIM * h) & (lane_h < HEAD_DIM * (h + 1))
            qh = jnp.where(hm, q, jnp.zeros_like(q))
            m8 = jnp.full((8, tq), NEG_INF, F32)
            for j in range(n_chunks):
                kb = slopes[h] * (kpos_ref[rows(j), :] - q0f)
                kb = jnp.concatenate([kb] * (tq // LANES), axis=1)
                s = _nt(k_ref[0, rows(j), :], qh) + kb + mb_ref[rows(j), :]
                s_ref[h, rows(j), :] = s
                m8 = jnp.maximum(m8, _fold_rows(s, jnp.maximum))
            m = jnp.max(m8, axis=0, keepdims=True)
            l8 = None
            acc = None
            for j in range(n_chunks):
                p = jnp.exp(s_ref[h, rows(j), :] - m)
                f = _fold_rows(p, jnp.add)
                pv = _mm(vt_ref[0, HEAD_DIM * h:HEAD_DIM * (h + 1), rows(j)], p.astype(BF16))
                l8 = f if l8 is None else l8 + f
                acc = pv if acc is None else acc + pv
            outs.append(acc / jnp.sum(l8, axis=0, keepdims=True))
        o_ref[0] = jnp.concatenate(outs, axis=0).T.astype(BF16)

    for c in range(ncls):
        pl.when((i >= c * per) & (i < (c + 1) * per))(functools.partial(body, c))


def _dsa_attention(qc, kc, vct, iq, ik, iwt, slopes, tq, ncls):
    B, S, hw = qc.shape
    nq = S // tq
    per = nq // ncls
    ch = tq
    n_keep = min(DSA_TOPK, S // 4)
    kpos = jnp.broadcast_to(jnp.arange(S, dtype=F32)[:, None], (S, LANES))
    full = lambda w: pl.BlockSpec((1, S, w), lambda b, i: (b, 0, 0))
    tile = lambda w: pl.BlockSpec((1, tq, w), lambda b, i: (b, i, 0))
    return pl.pallas_call(
        functools.partial(_dsa_kernel, tq=tq, ch=ch, per=per, ncls=ncls, n_keep=n_keep, slopes=slopes,
                          idx_scale=(IDX_DIM * IDX_HEADS) ** -0.5),
        grid=(B, nq),
        in_specs=[tile(hw), full(hw), pl.BlockSpec((1, hw, S), lambda b, i: (b, 0, 0)),
                  tile(IDX_HEADS * IDX_DIM), full(LANES), pl.BlockSpec((1, LANES, tq), lambda b, i: (b, 0, i)),
                  pl.BlockSpec((S, LANES), lambda b, i: (0, 0))],
        out_specs=tile(hw),
        out_shape=jax.ShapeDtypeStruct((B, S, hw), BF16),
        scratch_shapes=[pltpu.VMEM((S, tq), jnp.int32), pltpu.VMEM((S, tq), jnp.int16),
                        pltpu.VMEM((S, tq), jnp.int16), pltpu.VMEM((S, tq), F32),
                        pltpu.VMEM((DSA_HEADS, S, tq), F32)],
        compiler_params=_params("arbitrary", "arbitrary"),
        name="dsa_attn",
    )(qc, kc, vct, iq, ik, iwt, kpos)


def _merge_kernel(ya_ref, yb_ref, yc_ref, gt_ref, x_ref, mod_ref, wa_ref, wb_ref, wc_ref, wo_ref, o_ref):
    d = D_MODEL
    merged = gt_ref[0, :, 0:d].astype(F32) * _mm(ya_ref[0], wa_ref[...])
    merged = merged + gt_ref[0, :, d:2 * d].astype(F32) * _mm(yb_ref[0], wb_ref[...])
    merged = merged + gt_ref[0, :, 2 * d:3 * d].astype(F32) * _mm(yc_ref[0], wc_ref[...])
    o_ref[0] = x_ref[0] + mod_ref[0, 2:3, :] * _mm(merged.astype(BF16), wo_ref[...])


def _merge(ya, yb, yc, gt, x, mod, lw, tm):
    B, S, D = x.shape
    row = lambda w: pl.BlockSpec((1, tm, w), lambda b, i: (b, i, 0))
    consts = [lw["w_br_a"], lw["w_br_b"], lw["w_br_c"], lw["w_out"]]
    return pl.pallas_call(
        _merge_kernel,
        grid=(B, S // tm),
        in_specs=[row(ya.shape[-1]), row(yb.shape[-1]), row(yc.shape[-1]), row(3 * D), row(D),
                  pl.BlockSpec((1, 6, D), lambda b, i: (b, 0, 0))] + [_const_spec(c.shape) for c in consts],
        out_specs=row(D),
        out_shape=jax.ShapeDtypeStruct((B, S, D), F32),
        compiler_params=_params("arbitrary", "arbitrary"),
        name="merge_out",
    )(ya, yb, yc, gt, x, mod, *consts)


def _ffn_kernel(x_ref, mod_ref, n2_ref, wg_ref, wu_ref, wd_ref, o_ref, h_ref, acc_ref):
    j = pl.program_id(2)

    @pl.when(j == 0)
    def _():
        h = _rms(x_ref[0], n2_ref[...]) * (1.0 + mod_ref[0, 4:5, :]) + mod_ref[0, 3:4, :]
        h_ref[...] = h.astype(BF16)
        acc_ref[...] = jnp.zeros_like(acc_ref)

    h = h_ref[...]
    g = _mm(h, wg_ref[...])
    u = _mm(h, wu_ref[...])
    act = (g * jax.nn.sigmoid(g) * u).astype(BF16)
    acc_ref[...] += _mm(act, wd_ref[...])

    @pl.when(j == pl.num_programs(2) - 1)
    def _():
        o_ref[0] = x_ref[0] + mod_ref[0, 5:6, :] * acc_ref[...]


def _ffn(x, mod, lw, tm, tf):
    B, S, D = x.shape
    F = lw["w_g"].shape[1]
    row = pl.BlockSpec((1, tm, D), lambda b, i, j: (b, i, 0))
    return pl.pallas_call(
        _ffn_kernel,
        grid=(B, S // tm, F // tf),
        in_specs=[row, pl.BlockSpec((1, 6, D), lambda b, i, j: (b, 0, 0)),
                  pl.BlockSpec((1, D), lambda b, i, j: (0, 0)),
                  pl.BlockSpec((D, tf), lambda b, i, j: (0, j)),
                  pl.BlockSpec((D, tf), lambda b, i, j: (0, j)),
                  pl.BlockSpec((tf, D), lambda b, i, j: (j, 0))],
        out_specs=row,
        out_shape=jax.ShapeDtypeStruct((B, S, D), F32),
        scratch_shapes=[pltpu.VMEM((tm, D), BF16), pltpu.VMEM((tm, D), F32)],
        compiler_params=_params("arbitrary", "arbitrary", "arbitrary"),
        name="swiglu",
    )(x, mod, lw["n2"], lw["w_g"], lw["w_u"], lw["w_d"])


def _pad_cols(w, width):
    return jnp.pad(w, ((0, 0), (0, width - w.shape[1])))


def _layer_weights(l, norm1, w_in, g_cq, w_uq, g_ckv, w_ukv, qn_mla, kn_mla, qn_moba, kn_moba, qn_dsa,
                   kn_dsa, w_br_a, w_br_b, w_br_c, w_out, norm2, w_gu, w_down):
    D = D_MODEL
    w = w_in[l]
    o = 0
    cols = {}
    for name, wd in (("cq", MLA_Q_RANK), ("ckv", MLA_KV_RANK), ("kr", MLA_ROPE),
                     ("b", 3 * MOBA_HEADS * HEAD_DIM), ("c", 3 * DSA_HEADS * HEAD_DIM),
                     ("iq", IDX_HEADS * IDX_DIM), ("ik", IDX_DIM), ("iw", IDX_HEADS), ("g", 3 * D)):
        cols[name] = w[:, o:o + wd]
        o += wd
    kr_slot = jnp.pad(cols["kr"], ((0, 0), (MLA_NOPE, LANES - MLA_QK)))
    ik_rep = jnp.tile(cols["ik"], (1, LANES // IDX_DIM))
    w_all = jnp.concatenate([cols["cq"], cols["ckv"], kr_slot, cols["b"], cols["c"], cols["iq"], ik_rep,
                             _pad_cols(cols["iw"], LANES), cols["g"]], axis=1).astype(BF16)
    wuq = jnp.pad(w_uq[l].reshape(MLA_Q_RANK, MLA_HEADS, MLA_QK), ((0, 0), (0, 0), (0, LANES - MLA_QK)))
    wukv = w_ukv[l].reshape(MLA_KV_RANK, MLA_HEADS, MLA_NOPE + MLA_V)
    wk = jnp.pad(wukv[:, :, :MLA_NOPE], ((0, 0), (0, 0), (0, LANES - MLA_NOPE)))
    wv = wukv[:, :, MLA_NOPE:]
    row = lambda v: v.reshape(1, -1)
    return {
        "n1": row(norm1[l]), "w_all": w_all, "g_cq": row(g_cq[l]),
        "w_uq": wuq.reshape(MLA_Q_RANK, MLA_HEADS * LANES).astype(BF16), "g_ckv": row(g_ckv[l]),
        "w_k": wk.reshape(MLA_KV_RANK, MLA_HEADS * LANES).astype(BF16),
        "w_v": wv.reshape(MLA_KV_RANK, MLA_HEADS * MLA_V).astype(BF16),
        "qn_a": row(jnp.pad(qn_mla[l] * MLA_QK ** -0.5, (0, LANES - MLA_QK))), "kn_a": row(jnp.pad(kn_mla[l], (0, LANES - MLA_QK))),
        "qn_b": row(jnp.tile(qn_moba[l] * HEAD_DIM ** -0.5, MOBA_HEADS)), "kn_b": row(jnp.tile(kn_moba[l], MOBA_HEADS)),
        "qn_c": row(jnp.tile(qn_dsa[l] * HEAD_DIM ** -0.5, DSA_HEADS)), "kn_c": row(jnp.tile(kn_dsa[l], DSA_HEADS)),
        "w_br_a": w_br_a[l].astype(BF16), "w_br_b": w_br_b[l].astype(BF16), "w_br_c": w_br_c[l].astype(BF16),
        "w_out": w_out[l].astype(BF16), "n2": row(norm2[l]),
        "w_g": w_gu[l][:, :D_FF].astype(BF16), "w_u": w_gu[l][:, D_FF:].astype(BF16),
        "w_d": w_down[l].astype(BF16),
    }


def _rope_tables(S):
    half = MLA_ROPE // 2
    freqs = ROPE_THETA ** (-jnp.arange(half, dtype=F32) / half)
    ang = jnp.arange(S, dtype=F32)[:, None] * freqs[None, :]
    cos, sin = jnp.cos(ang), jnp.sin(ang)
    zeros = lambda n: jnp.zeros((S, n), F32)
    tail = LANES - MLA_QK
    rc = jnp.concatenate([jnp.ones((S, MLA_NOPE), F32), cos, cos, jnp.ones((S, tail), F32)], axis=1)
    rs1 = jnp.concatenate([zeros(MLA_NOPE), -sin, zeros(half + tail)], axis=1)
    rs2 = jnp.concatenate([zeros(MLA_NOPE + half), sin, zeros(tail)], axis=1)
    return rc, rs1, rs2


def _tiles(S):
    dsa_q = min(256, S)
    return {"proj": min(256, S), "out": min(512, S), "mla_q": min(256, S), "dsa_q": dsa_q,
            "dsa_cls": min(4, S // dsa_q)}


def kernel(x, c, w_ada, b_ada, norm1, w_in, g_cq, w_uq, g_ckv, w_ukv, qn_mla, kn_mla, qn_moba, kn_moba, qn_dsa, kn_dsa, w_br_a, w_br_b, w_br_c, w_out, norm2, w_gu, w_down):
    B, S, D = x.shape
    L = w_ada.shape[0]
    assert D == D_MODEL and S % MOBA_BLOCK == 0
    n_slopes = MOBA_HEADS + DSA_HEADS
    slopes = [2.0 ** (-8.0 * (i + 1) / n_slopes) for i in range(n_slopes)]
    mod_all = _ada(c, w_ada.astype(BF16), b_ada).reshape(L, B, 6, D)
    rope_tabs = _rope_tables(S)
    seg = jnp.arange(MOBA_HEADS * HEAD_DIM) // HEAD_DIM
    bd = (seg[:, None] == seg[None, :]).astype(BF16)
    t = _tiles(S)
    for l in range(L):
        lw = _layer_weights(l, norm1, w_in, g_cq, w_uq, g_ckv, w_ukv, qn_mla, kn_mla, qn_moba, kn_moba,
                            qn_dsa, kn_dsa, w_br_a, w_br_b, w_br_c, w_out, norm2, w_gu, w_down)
        mod = mod_all[l]
        qa, ka, va, qb, kb, vb, qc, kc, vc, iq, ik, iw, gt = _inproj(x, mod, lw, rope_tabs, bd, t["proj"])
        ya = _mla_attention(qa, ka, va, tq=t["mla_q"])
        yb = _moba_attention(qb, kb, vb, tuple(slopes[0::2]))
        yc = _dsa_attention(qc, kc, vc, iq, ik, iw, tuple(slopes[1::2]), tq=t["dsa_q"], ncls=t["dsa_cls"])
        x = _merge(ya, yb, yc, gt, x, mod, lw, t["out"])
        x = _ffn(x, mod, lw, t["out"], D_FF // 2)
    return x
```

```python
import functools

import jax
import jax.numpy as jnp
from jax import lax
from jax.experimental import pallas as pl
from jax.experimental.pallas import tpu as pltpu

F32 = jnp.float32
BF16 = jnp.bfloat16

D_MODEL = 1024
HEAD_DIM = 64
RMS_EPS = 1e-6
MLA_HEADS = 8
MLA_NOPE = 64
MLA_ROPE = 32
MLA_QK = MLA_NOPE + MLA_ROPE
MLA_V = 64
MLA_Q_RANK = 768
MLA_KV_RANK = 256
ROPE_THETA = 10000.0
MOBA_HEADS = 4
MOBA_BLOCK = 256
MOBA_TOPK = 3
DSA_HEADS = 4
IDX_HEADS = 16
IDX_DIM = 32
DSA_TOPK = 256
D_FF = 2816

LANES = 128
VMEM_LIMIT = 56 * 1024 * 1024

OFF_CQ = 0
OFF_CKV = OFF_CQ + MLA_Q_RANK
OFF_KR = OFF_CKV + MLA_KV_RANK
OFF_B = OFF_KR + LANES
OFF_C = OFF_B + 3 * MOBA_HEADS * HEAD_DIM
OFF_IQ = OFF_C + 3 * DSA_HEADS * HEAD_DIM
OFF_IK = OFF_IQ + IDX_HEADS * IDX_DIM
OFF_IW = OFF_IK + LANES
OFF_G = OFF_IW + LANES
N_ALL = OFF_G + 3 * D_MODEL


def _nt(a, b):
    return lax.dot_general(a, b, (((1,), (1,)), ((), ())), preferred_element_type=F32)


def _mm(a, b):
    return jnp.dot(a, b, preferred_element_type=F32)


def _split3(x):
    hi = x.astype(BF16)
    r = x - hi.astype(F32)
    mid = r.astype(BF16)
    lo = (r - mid.astype(F32)).astype(BF16)
    return hi, mid, lo


def _rms(x, g):
    return x * lax.rsqrt(jnp.mean(x * x, axis=-1, keepdims=True) + RMS_EPS) * g


def _const_spec(shape):
    nd = len(shape)
    return pl.BlockSpec(shape, lambda *_: (0,) * nd, pipeline_mode=pl.Buffered(1))


def _params(*sem):
    return pltpu.CompilerParams(dimension_semantics=sem, vmem_limit_bytes=VMEM_LIMIT)


def _ada_kernel(c_ref, w_ref, b_ref, o_ref):
    c = c_ref[...]
    cond = (c * jax.nn.sigmoid(c)).astype(BF16)
    o_ref[0] = _mm(cond, w_ref[0]) + b_ref[0]


def _ada(c, w_ada, b_ada):
    L, D, N = w_ada.shape
    B = c.shape[0]
    tn = 1024
    return pl.pallas_call(
        _ada_kernel,
        grid=(L, N // tn),
        in_specs=[
            pl.BlockSpec((B, D), lambda l, n: (0, 0)),
            pl.BlockSpec((1, D, tn), lambda l, n: (l, 0, n)),
            pl.BlockSpec((1, 1, tn), lambda l, n: (l, 0, n)),
        ],
        out_specs=pl.BlockSpec((1, B, tn), lambda l, n: (l, 0, n)),
        out_shape=jax.ShapeDtypeStruct((L, B, N), F32),
        compiler_params=_params("arbitrary", "arbitrary"),
        name="ada_mod",
    )(c, w_ada, b_ada.reshape(L, 1, N))


def _rope(x, c, s1, s2):
    return x * c + pltpu.roll(x, LANES - 16, 1) * s1 + pltpu.roll(x, 16, 1) * s2


def _inproj_kernel(x_ref, mod_ref, n1_ref, w_ref, gcq_ref, wuq_ref, gckv_ref, wk_ref, wv_ref,
                   qna_ref, kna_ref, rc_ref, rs1_ref, rs2_ref, bd_ref,
                   qnb_ref, knb_ref, qnc_ref, knc_ref,
                   qa_ref, ka_ref, va_ref, qb_ref, kb_ref, vb_ref, qc_ref, kc_ref, vc_ref,
                   iq_ref, ik_ref, iw_ref, gt_ref):
    x = x_ref[0]
    sh1 = mod_ref[0, 0:1, :]
    sc1 = mod_ref[0, 1:2, :]
    hb = (_rms(x, n1_ref[...]) * (1.0 + sc1) + sh1).astype(BF16)

    a = _mm(hb, w_ref[:, OFF_CQ:OFF_B])
    cqn = _rms(a[:, OFF_CQ:OFF_CKV], gcq_ref[...]).astype(BF16)
    ckvn = _rms(a[:, OFF_CKV:OFF_KR], gckv_ref[...]).astype(BF16)
    kr = a[:, OFF_KR:OFF_B]
    qraw = _mm(cqn, wuq_ref[...])
    kraw = _mm(ckvn, wk_ref[...])
    va_ref[0] = _mm(ckvn, wv_ref[...]).astype(BF16)
    rc, rs1, rs2 = rc_ref[...], rs1_ref[...], rs2_ref[...]
    qna, kna = qna_ref[...], kna_ref[...]
    inv_qk = 1.0 / MLA_QK
    for h in range(MLA_HEADS):
        sl = slice(LANES * h, LANES * (h + 1))
        qh = qraw[:, sl]
        qh = qh * lax.rsqrt(jnp.sum(qh * qh, axis=-1, keepdims=True) * inv_qk + RMS_EPS) * qna
        qa_ref[0, :, sl] = _rope(qh, rc, rs1, rs2).astype(BF16)
        kh = kraw[:, sl] + kr
        kh = kh * lax.rsqrt(jnp.sum(kh * kh, axis=-1, keepdims=True) * inv_qk + RMS_EPS) * kna
        ka_ref[0, :, sl] = _rope(kh, rc, rs1, rs2).astype(BF16)

    bd = bd_ref[...]

    def segnorm(v, g):
        ss = sum(_mm(part, bd) for part in _split3(v * v))
        return v * lax.rsqrt(ss * (1.0 / HEAD_DIM) + RMS_EPS) * g

    hw = MOBA_HEADS * HEAD_DIM
    pb = _mm(hb, w_ref[:, OFF_B:OFF_C])
    qb_ref[0] = segnorm(pb[:, 0:hw], qnb_ref[...]).astype(BF16)
    kb_ref[0] = segnorm(pb[:, hw:2 * hw], knb_ref[...]).astype(BF16)
    vb_ref[0] = pb[:, 2 * hw:3 * hw].astype(BF16)
    pc = _mm(hb, w_ref[:, OFF_C:OFF_IQ])
    qc_ref[0] = segnorm(pc[:, 0:hw], qnc_ref[...]).astype(BF16)
    kc_ref[0] = segnorm(pc[:, hw:2 * hw], knc_ref[...]).astype(BF16)
    vc_ref[0] = pc[:, 2 * hw:3 * hw].T.astype(BF16)

    pi = _mm(hb, w_ref[:, OFF_IQ:OFF_G])
    iq_ref[0] = pi[:, 0:OFF_IK - OFF_IQ].astype(BF16)
    ik_ref[0] = pi[:, OFF_IK - OFF_IQ:OFF_IW - OFF_IQ].astype(BF16)
    iw_ref[0] = pi[:, OFF_IW - OFF_IQ:OFF_G - OFF_IQ].T
    gt_ref[0] = jax.nn.sigmoid(_mm(hb, w_ref[:, OFF_G:N_ALL])).astype(BF16)


def _inproj(x, mod, lw, rope_tabs, bd, tm):
    B, S, D = x.shape
    hw = MOBA_HEADS * HEAD_DIM
    row = lambda w: pl.BlockSpec((1, tm, w), lambda b, i: (b, i, 0))
    tab = pl.BlockSpec((tm, LANES), lambda b, i: (i, 0))
    consts = [lw["n1"], lw["w_all"], lw["g_cq"], lw["w_uq"], lw["g_ckv"], lw["w_k"], lw["w_v"],
              lw["qn_a"], lw["kn_a"]]
    consts2 = [bd, lw["qn_b"], lw["kn_b"], lw["qn_c"], lw["kn_c"]]
    widths = [MLA_HEADS * LANES, MLA_HEADS * LANES, MLA_HEADS * MLA_V, hw, hw, hw, hw, hw, hw,
              IDX_HEADS * IDX_DIM, LANES, LANES, 3 * D_MODEL]
    dtypes = [BF16] * 11 + [F32, BF16]
    transposed = [False] * 8 + [True, False, False, True, False]
    col = lambda w: pl.BlockSpec((1, w, tm), lambda b, i: (b, 0, i))
    return pl.pallas_call(
        _inproj_kernel,
        grid=(B, S // tm),
        in_specs=[row(D), pl.BlockSpec((1, 6, D), lambda b, i: (b, 0, 0))]
        + [_const_spec(c.shape) for c in consts] + [tab, tab, tab]
        + [_const_spec(c.shape) for c in consts2],
        out_specs=[col(w) if tr else row(w) for w, tr in zip(widths, transposed)],
        out_shape=[jax.ShapeDtypeStruct((B, w, S) if tr else (B, S, w), dt)
                   for w, dt, tr in zip(widths, dtypes, transposed)],
        compiler_params=_params("arbitrary", "arbitrary"),
        name="in_proj",
    )(x, mod, *consts, *rope_tabs, *consts2)


NEG_INF = float("-inf")


def _fold_lanes(x, op):
    out = x[:, 0:LANES]
    for t in range(1, x.shape[1] // LANES):
        out = op(out, x[:, LANES * t:LANES * (t + 1)])
    return out


def _softmax_pv(s_ref, v_ref, n_chunks, ch, m_acc):
    m = jnp.max(m_acc, axis=-1, keepdims=True)
    l_acc = None
    acc = None
    for j in range(n_chunks):
        p = jnp.exp(s_ref[:, j * ch:(j + 1) * ch] - m)
        lj = _fold_lanes(p, jnp.add)
        pv = _mm(p.astype(BF16), v_ref[0, j * ch:(j + 1) * ch, :])
        l_acc = lj if l_acc is None else l_acc + lj
        acc = pv if acc is None else acc + pv
    return acc / jnp.sum(l_acc, axis=-1, keepdims=True)


def _mla_kernel(q_ref, k_ref, v_ref, o_ref, s_ref, *, tq, nq):
    row = lax.broadcasted_iota(jnp.int32, (tq, tq), 0)
    col = lax.broadcasted_iota(jnp.int32, (tq, tq), 1)
    causal = col <= row
    lane = lax.broadcasted_iota(jnp.int32, (tq, LANES), 1)

    for c in range(nq):
        outs = []
        for hh in range(2):
            sl = slice(LANES * hh, LANES * (hh + 1))
            q = q_ref[0, c * tq:(c + 1) * tq, sl]
            rows = s_ref.at[c % 2, hh]
            m_acc = jnp.full((tq, LANES), NEG_INF, F32)
            for j in range(c + 1):
                s = _nt(q, k_ref[0, j * tq:(j + 1) * tq, sl])
                if j == c:
                    s = jnp.where(causal, s, NEG_INF)
                rows[:, j * tq:(j + 1) * tq] = s
                m_acc = jnp.maximum(m_acc, _fold_lanes(s, jnp.maximum))
            outs.append(_softmax_pv(rows, v_ref, c + 1, tq, m_acc))
        o_ref[0, c * tq:(c + 1) * tq, :] = jnp.where(lane < MLA_V, outs[0], outs[1]).astype(BF16)


def _mla_attention(qa, ka, va, tq):
    B, S, _ = qa.shape
    pairs = MLA_HEADS // 2
    nq = S // tq
    wide = pl.BlockSpec((1, S, 2 * LANES), lambda b, h: (b, 0, h))
    narrow = pl.BlockSpec((1, S, LANES), lambda b, h: (b, 0, h))
    return pl.pallas_call(
        functools.partial(_mla_kernel, tq=tq, nq=nq),
        grid=(B, pairs),
        in_specs=[wide, wide, narrow],
        out_specs=narrow,
        out_shape=jax.ShapeDtypeStruct((B, S, MLA_HEADS * MLA_V), BF16),
        scratch_shapes=[pltpu.VMEM((2, 2, tq, S), F32)],
        compiler_params=_params("arbitrary", "arbitrary"),
        name="mla_attn",
    )(qa, ka, va)


def _moba_kernel(q_ref, k_ref, v_ref, o_ref, s_ref, *, nb, nbp, n_sel, slopes):
    blk = MOBA_BLOCK
    hw = MOBA_HEADS * HEAD_DIM
    lane = lax.broadcasted_iota(jnp.int32, (1, hw), 1)

    means = [jnp.mean(k_ref[0, n * blk:(n + 1) * blk, :].astype(F32), axis=0, keepdims=True)
             for n in range(nb)]
    km = jnp.concatenate(means + [jnp.zeros((1, hw), F32)] * (nbp - nb), axis=0)
    pieces = []
    for h in range(MOBA_HEADS):
        hm = (lane >= HEAD_DIM * h) & (lane < HEAD_DIM * (h + 1))
        pieces += [p.astype(F32) for p in _split3(jnp.where(hm, km, 0.0))]
    km_stack = jnp.concatenate(pieces, axis=0).astype(BF16)

    row = lax.broadcasted_iota(jnp.int32, (blk, blk), 0)
    col = lax.broadcasted_iota(jnp.int32, (blk, blk), 1)
    causal = col <= row
    kcol = lax.broadcasted_iota(jnp.int32, (1, blk), 1)
    bidx = lax.broadcasted_iota(jnp.int32, (nbp, blk), 0)

    for c in range(nb):
        q = q_ref[0, c * blk:(c + 1) * blk, :]
        past = bidx < c
        out = jnp.zeros((blk, hw), F32)
        if c > 0:
            g_all = _nt(km_stack, q)
        for h in range(MOBA_HEADS):
            hm = (lane >= HEAD_DIM * h) & (lane < HEAD_DIM * (h + 1))
            qh = jnp.where(hm, q, jnp.zeros_like(q))
            if c > 0:
                g = sum(g_all[(3 * h + p) * nbp:(3 * h + p + 1) * nbp, :] for p in range(3))
                g = jnp.where(past, g, NEG_INF)
                rank = jnp.zeros((nbp, blk), F32)
                for n2 in range(c):
                    gm = g[n2:n2 + 1, :]
                    beats = (gm > g) | ((gm == g) & (bidx > n2))
                    rank = rank + jnp.where(beats, 1.0, 0.0)
                drop_t = jnp.where(past & (rank < n_sel), 0.0, NEG_INF)
                drop_t = jnp.concatenate([drop_t, jnp.zeros((LANES - nbp, blk), F32)], axis=0)
                drop = drop_t.T
            rows = s_ref.at[c % 2, h]
            m_acc = jnp.full((blk, LANES), NEG_INF, F32)
            for j in range(c + 1):
                kbias = slopes[h] * (kcol + (j - c) * blk).astype(F32)
                s = _nt(qh, k_ref[0, j * blk:(j + 1) * blk, :]) + kbias
                if j == c:
                    s = jnp.where(causal, s, NEG_INF)
                else:
                    s = s + drop[:, j:j + 1]
                rows[:, j * blk:(j + 1) * blk] = s
                m_acc = jnp.maximum(m_acc, _fold_lanes(s, jnp.maximum))
            out = jnp.where(hm, _softmax_pv(rows, v_ref, c + 1, blk, m_acc), out)
        o_ref[0, c * blk:(c + 1) * blk, :] = out.astype(BF16)


def _moba_attention(qb, kb, vb, slopes):
    B, S, hw = qb.shape
    blk = MOBA_BLOCK
    nb = S // blk
    nbp = -(-nb // 8) * 8
    n_sel = max(1, min(MOBA_TOPK, nb - 1))
    full = pl.BlockSpec((1, S, hw), lambda b: (b, 0, 0))
    return pl.pallas_call(
        functools.partial(_moba_kernel, nb=nb, nbp=nbp, n_sel=n_sel, slopes=slopes),
        grid=(B,),
        in_specs=[full, full, full],
        out_specs=full,
        out_shape=jax.ShapeDtypeStruct((B, S, hw), BF16),
        scratch_shapes=[pltpu.VMEM((2, MOBA_HEADS, blk, S), F32)],
        compiler_params=_params("arbitrary"),
        name="moba_attn",
    )(qb, kb, vb)


INT_MIN = -2 ** 31


def _fold_rows(x, op, n=8):
    out = x[0:n, :]
    for t in range(1, x.shape[0] // n):
        out = op(out, x[n * t:n * (t + 1), :])
    return out


def _kth_largest_key(key_ref, hi_ref, lo_ref, rows, n_chunks, tq, k):
    i16, i32 = jnp.int16, jnp.int32
    one, none = jnp.int16(1), jnp.int16(0)

    def count(ref, pred):
        acc = None
        for j in range(n_chunks):
            f = _fold_rows(jnp.where(pred(ref[rows(j), :]), one, none), jnp.add, 16)
            acc = f if acc is None else acc + f
        return jnp.sum(acc.astype(i32), axis=0, keepdims=True)

    zero = jnp.zeros((1, tq), i32)
    hi = jnp.where(count(hi_ref, lambda v: v >= none) >= k, zero, i32(-2 ** 15))

    def step_hi(it, hi):
        cand = hi | lax.shift_left(i32(1), i32(14) - it)
        c16 = cand.astype(i16)
        return jnp.where(count(hi_ref, lambda v: v >= c16) >= k, cand, hi)

    hi = lax.fori_loop(0, 15, step_hi, hi)
    h16 = hi.astype(i16)
    for j in range(n_chunks):
        low = (key_ref[rows(j), :] & i32(0xFFFF)) - i32(2 ** 15)
        lo_ref[rows(j), :] = jnp.where(hi_ref[rows(j), :] == h16, low.astype(i16), i16(-2 ** 15))
    above = count(hi_ref, lambda v: v > h16)

    def step_lo(it, lo):
        cand = lo | lax.shift_left(i32(1), i32(15) - it)
        c16 = (cand - i32(2 ** 15)).astype(i16)
        return jnp.where(above + count(lo_ref, lambda v: v >= c16) >= k, cand, lo)

    lo = lax.fori_loop(0, 16, step_lo, zero)
    return lax.shift_left(hi, 16) | lo


def _dsa_kernel(q_ref, k_ref, vt_ref, iq_ref, ik_ref, iwt_ref, kpos_ref, o_ref, key_ref, hi_ref, lo_ref,
                mb_ref, s_ref, *, tq, ch, per, ncls, n_keep, slopes, idx_scale):
    i = pl.program_id(1)
    hw = DSA_HEADS * HEAD_DIM
    q0 = i * tq
    qpos = q0 + lax.broadcasted_iota(jnp.int32, (1, tq), 1)
    krow = lax.broadcasted_iota(jnp.int32, (ch, 1), 0)
    lane = lax.broadcasted_iota(jnp.int32, (1, LANES), 1)
    lane_h = lax.broadcasted_iota(jnp.int32, (1, hw), 1)
    grp = LANES // IDX_DIM

    def body(c):
        n_chunks = (c + 1) * per * tq // ch
        rows = lambda j: slice(j * ch, (j + 1) * ch)
        iwt = iwt_ref[0]
        qms = []
        for hh in range(IDX_HEADS):
            g, r = divmod(hh, grp)
            iqg = iq_ref[0, :, LANES * g:LANES * (g + 1)]
            qms.append(jnp.where((lane >= IDX_DIM * r) & (lane < IDX_DIM * (r + 1)), iqg, jnp.zeros_like(iqg)))
        for j in range(n_chunks):
            ikc = ik_ref[0, rows(j), :]
            score = None
            for hh in range(IDX_HEADS):
                term = jnp.maximum(_nt(ikc, qms[hh]), 0.0) * iwt[hh:hh + 1, :]
                score = term if score is None else score + term
            score = jnp.where(krow + j * ch <= qpos, score * idx_scale, NEG_INF)
            bits = lax.bitcast_convert_type(score, jnp.int32)
            key = jnp.where(bits >= 0, bits, bits ^ jnp.int32(0x7FFFFFFF))
            key_ref[rows(j), :] = key
            hi_ref[rows(j), :] = lax.shift_right_arithmetic(key, 16).astype(jnp.int16)

        if n_chunks * ch <= n_keep:
            thr = jnp.full((1, tq), INT_MIN, jnp.int32)
        else:
            thr = _kth_largest_key(key_ref, hi_ref, lo_ref, rows, n_chunks, tq, n_keep)
        for j in range(n_chunks):
            keep = (key_ref[rows(j), :] >= thr) & (krow + j * ch <= qpos)
            mb_ref[rows(j), :] = jnp.where(keep, 0.0, NEG_INF)

        q = q_ref[0]
        q0f = q0.astype(F32)
        outs = []
        for h in range(DSA_HEADS):
            hm = (lane_h >= HEAD_DIM * h) & (lane_h < HEAD_DIM * (h + 1))
            qh = jnp.where(hm, q, jnp.zeros_like(q))
            m8 = jnp.full((8, tq), NEG_INF, F32)
            for j in range(n_chunks):
                kb = slopes[h] * (kpos_ref[rows(j), :] - q0f)
                kb = jnp.concatenate([kb] * (tq // LANES), axis=1)
                s = _nt(k_ref[0, rows(j), :], qh) + kb + mb_ref[rows(j), :]
                s_ref[h, rows(j), :] = s
                m8 = jnp.maximum(m8, _fold_rows(s, jnp.maximum))
            m = jnp.max(m8, axis=0, keepdims=True)
            l8 = None
            acc = None
            for j in range(n_chunks):
                p = jnp.exp(s_ref[h, rows(j), :] - m)
                f = _fold_rows(p, jnp.add)
                pv = _mm(vt_ref[0, HEAD_DIM * h:HEAD_DIM * (h + 1), rows(j)], p.astype(BF16))
                l8 = f if l8 is None else l8 + f
                acc = pv if acc is None else acc + pv
            outs.append(acc / jnp.sum(l8, axis=0, keepdims=True))
        o_ref[0] = jnp.concatenate(outs, axis=0).T.astype(BF16)

    for c in range(ncls):
        pl.when((i >= c * per) & (i < (c + 1) * per))(functools.partial(body, c))


def _dsa_attention(qc, kc, vct, iq, ik, iwt, slopes, tq, ncls):
    B, S, hw = qc.shape
    nq = S // tq
    per = nq // ncls
    ch = tq
    n_keep = min(DSA_TOPK, S // 4)
    kpos = jnp.broadcast_to(jnp.arange(S, dtype=F32)[:, None], (S, LANES))
    full = lambda w: pl.BlockSpec((1, S, w), lambda b, i: (b, 0, 0))
    tile = lambda w: pl.BlockSpec((1, tq, w), lambda b, i: (b, i, 0))
    return pl.pallas_call(
        functools.partial(_dsa_kernel, tq=tq, ch=ch, per=per, ncls=ncls, n_keep=n_keep, slopes=slopes,
                          idx_scale=(IDX_DIM * IDX_HEADS) ** -0.5),
        grid=(B, nq),
        in_specs=[tile(hw), full(hw), pl.BlockSpec((1, hw, S), lambda b, i: (b, 0, 0)),
                  tile(IDX_HEADS * IDX_DIM), full(LANES), pl.BlockSpec((1, LANES, tq), lambda b, i: (b, 0, i)),
                  pl.BlockSpec((S, LANES), lambda b, i: (0, 0))],
        out_specs=tile(hw),
        out_shape=jax.ShapeDtypeStruct((B, S, hw), BF16),
        scratch_shapes=[pltpu.VMEM((S, tq), jnp.int32), pltpu.VMEM((S, tq), jnp.int16),
                        pltpu.VMEM((S, tq), jnp.int16), pltpu.VMEM((S, tq), F32),
                        pltpu.VMEM((DSA_HEADS, S, tq), F32)],
        compiler_params=_params("arbitrary", "arbitrary"),
        name="dsa_attn",
    )(qc, kc, vct, iq, ik, iwt, kpos)


def _merge_kernel(ya_ref, yb_ref, yc_ref, gt_ref, x_ref, mod_ref, wa_ref, wb_ref, wc_ref, wo_ref, o_ref):
    d = D_MODEL
    merged = gt_ref[0, :, 0:d].astype(F32) * _mm(ya_ref[0], wa_ref[...])
    merged = merged + gt_ref[0, :, d:2 * d].astype(F32) * _mm(yb_ref[0], wb_ref[...])
    merged = merged + gt_ref[0, :, 2 * d:3 * d].astype(F32) * _mm(yc_ref[0], wc_ref[...])
    o_ref[0] = x_ref[0] + mod_ref[0, 2:3, :] * _mm(merged.astype(BF16), wo_ref[...])


def _merge(ya, yb, yc, gt, x, mod, lw, tm):
    B, S, D = x.shape
    row = lambda w: pl.BlockSpec((1, tm, w), lambda b, i: (b, i, 0))
    consts = [lw["w_br_a"], lw["w_br_b"], lw["w_br_c"], lw["w_out"]]
    return pl.pallas_call(
        _merge_kernel,
        grid=(B, S // tm),
        in_specs=[row(ya.shape[-1]), row(yb.shape[-1]), row(yc.shape[-1]), row(3 * D), row(D),
                  pl.BlockSpec((1, 6, D), lambda b, i: (b, 0, 0))] + [_const_spec(c.shape) for c in consts],
        out_specs=row(D),
        out_shape=jax.ShapeDtypeStruct((B, S, D), F32),
        compiler_params=_params("arbitrary", "arbitrary"),
        name="merge_out",
    )(ya, yb, yc, gt, x, mod, *consts)


def _ffn_kernel(x_ref, mod_ref, n2_ref, wg_ref, wu_ref, wd_ref, o_ref):
    x = x_ref[0]
    h = (_rms(x, n2_ref[...]) * (1.0 + mod_ref[0, 4:5, :]) + mod_ref[0, 3:4, :]).astype(BF16)
    g = _mm(h, wg_ref[...])
    u = _mm(h, wu_ref[...])
    act = (g * jax.nn.sigmoid(g) * u).astype(BF16)
    o_ref[0] = x + mod_ref[0, 5:6, :] * _mm(act, wd_ref[...])


def _ffn(x, mod, lw, tm):
    B, S, D = x.shape
    row = pl.BlockSpec((1, tm, D), lambda b, i: (b, i, 0))
    consts = [lw["n2"], lw["w_g"], lw["w_u"], lw["w_d"]]
    return pl.pallas_call(
        _ffn_kernel,
        grid=(B, S // tm),
        in_specs=[row, pl.BlockSpec((1, 6, D), lambda b, i: (b, 0, 0))] + [_const_spec(c.shape) for c in consts],
        out_specs=row,
        out_shape=jax.ShapeDtypeStruct((B, S, D), F32),
        compiler_params=_params("arbitrary", "arbitrary"),
        name="swiglu",
    )(x, mod, *consts)


def _pad_cols(w, width):
    return jnp.pad(w, ((0, 0), (0, width - w.shape[1])))


def _layer_weights(l, norm1, w_in, g_cq, w_uq, g_ckv, w_ukv, qn_mla, kn_mla, qn_moba, kn_moba, qn_dsa,
                   kn_dsa, w_br_a, w_br_b, w_br_c, w_out, norm2, w_gu, w_down):
    D = D_MODEL
    w = w_in[l]
    o = 0
    cols = {}
    for name, wd in (("cq", MLA_Q_RANK), ("ckv", MLA_KV_RANK), ("kr", MLA_ROPE),
                     ("b", 3 * MOBA_HEADS * HEAD_DIM), ("c", 3 * DSA_HEADS * HEAD_DIM),
                     ("iq", IDX_HEADS * IDX_DIM), ("ik", IDX_DIM), ("iw", IDX_HEADS), ("g", 3 * D)):
        cols[name] = w[:, o:o + wd]
        o += wd
    kr_slot = jnp.pad(cols["kr"], ((0, 0), (MLA_NOPE, LANES - MLA_QK)))
    ik_rep = jnp.tile(cols["ik"], (1, LANES // IDX_DIM))
    w_all = jnp.concatenate([cols["cq"], cols["ckv"], kr_slot, cols["b"], cols["c"], cols["iq"], ik_rep,
                             _pad_cols(cols["iw"], LANES), cols["g"]], axis=1).astype(BF16)
    wuq = jnp.pad(w_uq[l].reshape(MLA_Q_RANK, MLA_HEADS, MLA_QK), ((0, 0), (0, 0), (0, LANES - MLA_QK)))
    wukv = w_ukv[l].reshape(MLA_KV_RANK, MLA_HEADS, MLA_NOPE + MLA_V)
    wk = jnp.pad(wukv[:, :, :MLA_NOPE], ((0, 0), (0, 0), (0, LANES - MLA_NOPE)))
    wv = wukv[:, :, MLA_NOPE:]
    row = lambda v: v.reshape(1, -1)
    return {
        "n1": row(norm1[l]), "w_all": w_all, "g_cq": row(g_cq[l]),
        "w_uq": wuq.reshape(MLA_Q_RANK, MLA_HEADS * LANES).astype(BF16), "g_ckv": row(g_ckv[l]),
        "w_k": wk.reshape(MLA_KV_RANK, MLA_HEADS * LANES).astype(BF16),
        "w_v": wv.reshape(MLA_KV_RANK, MLA_HEADS * MLA_V).astype(BF16),
        "qn_a": row(jnp.pad(qn_mla[l] * MLA_QK ** -0.5, (0, LANES - MLA_QK))), "kn_a": row(jnp.pad(kn_mla[l], (0, LANES - MLA_QK))),
        "qn_b": row(jnp.tile(qn_moba[l] * HEAD_DIM ** -0.5, MOBA_HEADS)), "kn_b": row(jnp.tile(kn_moba[l], MOBA_HEADS)),
        "qn_c": row(jnp.tile(qn_dsa[l] * HEAD_DIM ** -0.5, DSA_HEADS)), "kn_c": row(jnp.tile(kn_dsa[l], DSA_HEADS)),
        "w_br_a": w_br_a[l].astype(BF16), "w_br_b": w_br_b[l].astype(BF16), "w_br_c": w_br_c[l].astype(BF16),
        "w_out": w_out[l].astype(BF16), "n2": row(norm2[l]),
        "w_g": w_gu[l][:, :D_FF].astype(BF16), "w_u": w_gu[l][:, D_FF:].astype(BF16),
        "w_d": w_down[l].astype(BF16),
    }


def _rope_tables(S):
    half = MLA_ROPE // 2
    freqs = ROPE_THETA ** (-jnp.arange(half, dtype=F32) / half)
    ang = jnp.arange(S, dtype=F32)[:, None] * freqs[None, :]
    cos, sin = jnp.cos(ang), jnp.sin(ang)
    zeros = lambda n: jnp.zeros((S, n), F32)
    tail = LANES - MLA_QK
    rc = jnp.concatenate([jnp.ones((S, MLA_NOPE), F32), cos, cos, jnp.ones((S, tail), F32)], axis=1)
    rs1 = jnp.concatenate([zeros(MLA_NOPE), -sin, zeros(half + tail)], axis=1)
    rs2 = jnp.concatenate([zeros(MLA_NOPE + half), sin, zeros(tail)], axis=1)
    return rc, rs1, rs2


def _tiles(S):
    dsa_q = min(256, S)
    return {"proj": min(256, S), "out": min(512, S), "mla_q": min(256, S), "dsa_q": dsa_q,
            "dsa_cls": min(4, S // dsa_q)}


def kernel(x, c, w_ada, b_ada, norm1, w_in, g_cq, w_uq, g_ckv, w_ukv, qn_mla, kn_mla, qn_moba, kn_moba, qn_dsa, kn_dsa, w_br_a, w_br_b, w_br_c, w_out, norm2, w_gu, w_down):
    B, S, D = x.shape
    L = w_ada.shape[0]
    assert D == D_MODEL and S % MOBA_BLOCK == 0
    n_slopes = MOBA_HEADS + DSA_HEADS
    slopes = [2.0 ** (-8.0 * (i + 1) / n_slopes) for i in range(n_slopes)]
    mod_all = _ada(c, w_ada.astype(BF16), b_ada).reshape(L, B, 6, D)
    rope_tabs = _rope_tables(S)
    seg = jnp.arange(MOBA_HEADS * HEAD_DIM) // HEAD_DIM
    bd = (seg[:, None] == seg[None, :]).astype(BF16)
    t = _tiles(S)
    for l in range(L):
        lw = _layer_weights(l, norm1, w_in, g_cq, w_uq, g_ckv, w_ukv, qn_mla, kn_mla, qn_moba, kn_moba,
                            qn_dsa, kn_dsa, w_br_a, w_br_b, w_br_c, w_out, norm2, w_gu, w_down)
        mod = mod_all[l]
        qa, ka, va, qb, kb, vb, qc, kc, vc, iq, ik, iw, gt = _inproj(x, mod, lw, rope_tabs, bd, t["proj"])
        ya = _mla_attention(qa, ka, va, tq=t["mla_q"])
        yb = _moba_attention(qb, kb, vb, tuple(slopes[0::2]))
        yc = _dsa_attention(qc, kc, vc, iq, ik, iw, tuple(slopes[1::2]), tq=t["dsa_q"], ncls=t["dsa_cls"])
        x = _merge(ya, yb, yc, gt, x, mod, lw, t["out"])
        x = _ffn(x, mod, lw, t["out"])
    return x
```

```python
import functools

import jax
import jax.numpy as jnp
from jax import lax
from jax.experimental import pallas as pl
from jax.experimental.pallas import tpu as pltpu

F32 = jnp.float32
BF16 = jnp.bfloat16

D_MODEL = 1024
HEAD_DIM = 64
RMS_EPS = 1e-6
MLA_HEADS = 8
MLA_NOPE = 64
MLA_ROPE = 32
MLA_QK = MLA_NOPE + MLA_ROPE
MLA_V = 64
MLA_Q_RANK = 768
MLA_KV_RANK = 256
ROPE_THETA = 10000.0
MOBA_HEADS = 4
MOBA_BLOCK = 256
MOBA_TOPK = 3
DSA_HEADS = 4
IDX_HEADS = 16
IDX_DIM = 32
DSA_TOPK = 256
D_FF = 2816

LANES = 128
VMEM_LIMIT = 56 * 1024 * 1024

OFF_CQ = 0
OFF_CKV = OFF_CQ + MLA_Q_RANK
OFF_KR = OFF_CKV + MLA_KV_RANK
OFF_B = OFF_KR + LANES
OFF_C = OFF_B + 2 * MOBA_HEADS * HEAD_DIM
OFF_IQ = OFF_C + 2 * DSA_HEADS * HEAD_DIM
OFF_IK = OFF_IQ + IDX_HEADS * IDX_DIM
OFF_G = OFF_IK + LANES
N_ALL = OFF_G + 3 * D_MODEL
ROW_VB = 0
ROW_VC = ROW_VB + MOBA_HEADS * HEAD_DIM
ROW_IW = ROW_VC + DSA_HEADS * HEAD_DIM
N_T = ROW_IW + LANES


def _nt(a, b):
    return lax.dot_general(a, b, (((1,), (1,)), ((), ())), preferred_element_type=F32)


def _mm(a, b):
    return jnp.dot(a, b, preferred_element_type=F32)


def _split3(x):
    hi = x.astype(BF16)
    r = x - hi.astype(F32)
    mid = r.astype(BF16)
    lo = (r - mid.astype(F32)).astype(BF16)
    return hi, mid, lo


def _rms(x, g):
    return x * lax.rsqrt(jnp.mean(x * x, axis=-1, keepdims=True) + RMS_EPS) * g


def _const_spec(shape):
    nd = len(shape)
    return pl.BlockSpec(shape, lambda *_: (0,) * nd, pipeline_mode=pl.Buffered(1))


def _params(*sem):
    return pltpu.CompilerParams(dimension_semantics=sem, vmem_limit_bytes=VMEM_LIMIT)


def _ada_kernel(c_ref, w_ref, b_ref, o_ref):
    c = c_ref[...]
    cond = (c * jax.nn.sigmoid(c)).astype(BF16)
    o_ref[0] = _mm(cond, w_ref[0]) + b_ref[0]


def _ada(c, w_ada, b_ada):
    L, D, N = w_ada.shape
    B = c.shape[0]
    tn = 1024
    return pl.pallas_call(
        _ada_kernel,
        grid=(L, N // tn),
        in_specs=[
            pl.BlockSpec((B, D), lambda l, n: (0, 0)),
            pl.BlockSpec((1, D, tn), lambda l, n: (l, 0, n)),
            pl.BlockSpec((1, 1, tn), lambda l, n: (l, 0, n)),
        ],
        out_specs=pl.BlockSpec((1, B, tn), lambda l, n: (l, 0, n)),
        out_shape=jax.ShapeDtypeStruct((L, B, N), F32),
        compiler_params=_params("arbitrary", "arbitrary"),
        name="ada_mod",
    )(c, w_ada, b_ada.reshape(L, 1, N))


def _rope(x, c, s1, s2):
    return x * c + pltpu.roll(x, LANES - 16, 1) * s1 + pltpu.roll(x, 16, 1) * s2


def _inproj_kernel(x_ref, mod_ref, n1_ref, w_ref, wt_ref, gcq_ref, wuq_ref, gckv_ref, wk_ref, wv_ref,
                   qna_ref, kna_ref, rc_ref, rs1_ref, rs2_ref, bd_ref,
                   qnb_ref, knb_ref, qnc_ref, knc_ref,
                   qa_ref, ka_ref, va_ref, qb_ref, kb_ref, vb_ref, qc_ref, kc_ref, vc_ref,
                   iq_ref, ik_ref, iw_ref, gt_ref):
    x = x_ref[0]
    sh1 = mod_ref[0, 0:1, :]
    sc1 = mod_ref[0, 1:2, :]
    hb = (_rms(x, n1_ref[...]) * (1.0 + sc1) + sh1).astype(BF16)

    a = _mm(hb, w_ref[:, OFF_CQ:OFF_B])
    cqn = _rms(a[:, OFF_CQ:OFF_CKV], gcq_ref[...]).astype(BF16)
    ckvn = _rms(a[:, OFF_CKV:OFF_KR], gckv_ref[...]).astype(BF16)
    kr = a[:, OFF_KR:OFF_B]
    qraw = _mm(cqn, wuq_ref[...])
    kraw = _mm(ckvn, wk_ref[...])
    va_ref[0] = _nt(wv_ref[...], ckvn).astype(BF16)
    rc, rs1, rs2 = rc_ref[...], rs1_ref[...], rs2_ref[...]
    qna, kna = qna_ref[...], kna_ref[...]
    inv_qk = 1.0 / MLA_QK
    for h in range(MLA_HEADS):
        sl = slice(LANES * h, LANES * (h + 1))
        qh = qraw[:, sl]
        qh = qh * lax.rsqrt(jnp.sum(qh * qh, axis=-1, keepdims=True) * inv_qk + RMS_EPS) * qna
        qa_ref[0, :, sl] = _rope(qh, rc, rs1, rs2).astype(BF16)
        kh = kraw[:, sl] + kr
        kh = kh * lax.rsqrt(jnp.sum(kh * kh, axis=-1, keepdims=True) * inv_qk + RMS_EPS) * kna
        ka_ref[0, :, sl] = _rope(kh, rc, rs1, rs2).astype(BF16)

    bd = bd_ref[...]

    def segnorm(v, g):
        ss = sum(_mm(part, bd) for part in _split3(v * v))
        return v * lax.rsqrt(ss * (1.0 / HEAD_DIM) + RMS_EPS) * g

    hw = MOBA_HEADS * HEAD_DIM
    pb = _mm(hb, w_ref[:, OFF_B:OFF_C])
    qb_ref[0] = segnorm(pb[:, 0:hw], qnb_ref[...]).astype(BF16)
    kb_ref[0] = segnorm(pb[:, hw:2 * hw], knb_ref[...]).astype(BF16)
    pc = _mm(hb, w_ref[:, OFF_C:OFF_IQ])
    qc_ref[0] = segnorm(pc[:, 0:hw], qnc_ref[...]).astype(BF16)
    kc_ref[0] = segnorm(pc[:, hw:2 * hw], knc_ref[...]).astype(BF16)

    pt = _nt(wt_ref[...], hb)
    vb_ref[0] = pt[ROW_VB:ROW_VC, :].astype(BF16)
    vc_ref[0] = pt[ROW_VC:ROW_IW, :].astype(BF16)
    iw_ref[0] = pt[ROW_IW:N_T, :]

    pi = _mm(hb, w_ref[:, OFF_IQ:OFF_G])
    iq_ref[0] = pi[:, 0:OFF_IK - OFF_IQ].astype(BF16)
    ik_ref[0] = pi[:, OFF_IK - OFF_IQ:OFF_G - OFF_IQ].astype(BF16)
    gt_ref[0] = jax.nn.sigmoid(_mm(hb, w_ref[:, OFF_G:N_ALL])).astype(BF16)


def _inproj(x, mod, lw, rope_tabs, bd, tm):
    B, S, D = x.shape
    hw = MOBA_HEADS * HEAD_DIM
    row = lambda w: pl.BlockSpec((1, tm, w), lambda b, i: (b, i, 0))
    tab = pl.BlockSpec((tm, LANES), lambda b, i: (i, 0))
    consts = [lw["n1"], lw["w_all"], lw["w_t"], lw["g_cq"], lw["w_uq"], lw["g_ckv"], lw["w_k"], lw["w_v"],
              lw["qn_a"], lw["kn_a"]]
    consts2 = [bd, lw["qn_b"], lw["kn_b"], lw["qn_c"], lw["kn_c"]]
    widths = [MLA_HEADS * LANES, MLA_HEADS * LANES, MLA_HEADS * MLA_V, hw, hw, hw, hw, hw, hw,
              IDX_HEADS * IDX_DIM, LANES, LANES, 3 * D_MODEL]
    dtypes = [BF16] * 11 + [F32, BF16]
    transposed = [False, False, True] * 3 + [False, False, True, False]
    col = lambda w: pl.BlockSpec((1, w, tm), lambda b, i: (b, 0, i))
    return pl.pallas_call(
        _inproj_kernel,
        grid=(B, S // tm),
        in_specs=[row(D), pl.BlockSpec((1, 6, D), lambda b, i: (b, 0, 0))]
        + [_const_spec(c.shape) for c in consts] + [tab, tab, tab]
        + [_const_spec(c.shape) for c in consts2],
        out_specs=[col(w) if tr else row(w) for w, tr in zip(widths, transposed)],
        out_shape=[jax.ShapeDtypeStruct((B, w, S) if tr else (B, S, w), dt)
                   for w, dt, tr in zip(widths, dtypes, transposed)],
        compiler_params=_params("arbitrary", "arbitrary"),
        name="in_proj",
    )(x, mod, *consts, *rope_tabs, *consts2)


NEG_INF = float("-inf")


def _fold_rows(x, op, n=8):
    out = x[0:n, :]
    for t in range(1, x.shape[0] // n):
        out = op(out, x[n * t:n * (t + 1), :])
    return out


def _interleave(units):
    pending = []
    for make in units:
        first, second = make()
        for t in range(max(len(first), len(pending))):
            if t < len(first):
                first[t]()
            if t < len(pending):
                pending[t]()
        pending = second
    for step in pending:
        step()


class _Unit:
    def __init__(self, logits, n_chunks, ch, tq, score, vt_chunk, done):
        self.m8 = jnp.full((8, tq), NEG_INF, F32)
        self.l8 = None
        self.acc = None

        def first(j):
            s = score(j)
            logits[j * ch:(j + 1) * ch, :] = s
            self.m8 = jnp.maximum(self.m8, _fold_rows(s, jnp.maximum))

        def second(j):
            if j == 0:
                self.m = jnp.max(self.m8, axis=0, keepdims=True)
            p = jnp.exp(logits[j * ch:(j + 1) * ch, :] - self.m)
            f = _fold_rows(p, jnp.add)
            pv = _mm(vt_chunk(j), p.astype(BF16))
            self.l8 = f if self.l8 is None else self.l8 + f
            self.acc = pv if self.acc is None else self.acc + pv
            if j == n_chunks - 1:
                done(self.acc / jnp.sum(self.l8, axis=0, keepdims=True))

        self.steps = ([functools.partial(first, j) for j in range(n_chunks)],
                      [functools.partial(second, j) for j in range(n_chunks)])


def _mla_kernel(q_ref, k_ref, vt_ref, o_ref, s_ref, *, tq, nq):
    krow = lax.broadcasted_iota(jnp.int32, (tq, tq), 0)
    qcol = lax.broadcasted_iota(jnp.int32, (tq, tq), 1)
    causal = krow <= qcol
    outs = {}

    def unit(c, hh, u):
        sl = slice(LANES * hh, LANES * (hh + 1))
        q = q_ref[0, c * tq:(c + 1) * tq, sl]

        def score(j):
            s = _nt(k_ref[0, j * tq:(j + 1) * tq, sl], q)
            return jnp.where(causal, s, NEG_INF) if j == c else s

        def done(out):
            outs[c, hh] = out
            if hh == 1:
                pair = jnp.concatenate([outs[c, 0], outs[c, 1]], axis=0)
                o_ref[0, c * tq:(c + 1) * tq, :] = pair.T.astype(BF16)

        vt = lambda j: vt_ref[0, MLA_V * hh:MLA_V * (hh + 1), j * tq:(j + 1) * tq]
        return _Unit(s_ref.at[u % 3], c + 1, tq, tq, score, vt, done).steps

    _interleave([functools.partial(unit, c, hh, 2 * c + hh) for c in range(nq) for hh in range(2)])


def _mla_attention(qa, ka, vat, tq):
    B, S, _ = qa.shape
    pairs = MLA_HEADS // 2
    nq = S // tq
    wide = pl.BlockSpec((1, S, 2 * LANES), lambda b, h: (b, 0, h))
    return pl.pallas_call(
        functools.partial(_mla_kernel, tq=tq, nq=nq),
        grid=(B, pairs),
        in_specs=[wide, wide, pl.BlockSpec((1, 2 * MLA_V, S), lambda b, h: (b, h, 0))],
        out_specs=pl.BlockSpec((1, S, 2 * MLA_V), lambda b, h: (b, 0, h)),
        out_shape=jax.ShapeDtypeStruct((B, S, MLA_HEADS * MLA_V), BF16),
        scratch_shapes=[pltpu.VMEM((3, S, tq), F32)],
        compiler_params=_params("arbitrary", "arbitrary"),
        name="mla_attn",
    )(qa, ka, vat)


def _moba_kernel(q_ref, k_ref, vt_ref, o_ref, s_ref, *, nb, nbp, n_sel, slopes):
    blk = MOBA_BLOCK
    hw = MOBA_HEADS * HEAD_DIM
    lane = lax.broadcasted_iota(jnp.int32, (1, hw), 1)

    means = [jnp.mean(k_ref[0, n * blk:(n + 1) * blk, :].astype(F32), axis=0, keepdims=True)
             for n in range(nb)]
    km = jnp.concatenate(means + [jnp.zeros((1, hw), F32)] * (nbp - nb), axis=0)
    pieces = []
    for h in range(MOBA_HEADS):
        hm = (lane >= HEAD_DIM * h) & (lane < HEAD_DIM * (h + 1))
        pieces += [p.astype(F32) for p in _split3(jnp.where(hm, km, 0.0))]
    km_stack = jnp.concatenate(pieces, axis=0).astype(BF16)

    krow = lax.broadcasted_iota(jnp.int32, (blk, blk), 0)
    qcol = lax.broadcasted_iota(jnp.int32, (blk, blk), 1)
    causal = krow <= qcol
    krow_f = krow.astype(F32)
    bidx = lax.broadcasted_iota(jnp.int32, (nbp, blk), 0)

    gates = {}
    outs = {}

    def unit(c, h, u):
        q = q_ref[0, c * blk:(c + 1) * blk, :]
        hm = (lane >= HEAD_DIM * h) & (lane < HEAD_DIM * (h + 1))
        qh = jnp.where(hm, q, jnp.zeros_like(q))
        kbias = slopes[h] * krow_f
        if c > 0:
            if h == 0:
                gates[c] = _nt(km_stack, q)
            past = bidx < c
            g = sum(gates[c][(3 * h + p) * nbp:(3 * h + p + 1) * nbp, :] for p in range(3))
            g = jnp.where(past, g, NEG_INF)
            rank = jnp.zeros((nbp, blk), F32)
            for n2 in range(c):
                gm = g[n2:n2 + 1, :]
                beats = (gm > g) | ((gm == g) & (bidx > n2))
                rank = rank + jnp.where(beats, 1.0, 0.0)
            drop = jnp.where(past & (rank < n_sel), 0.0, NEG_INF)

        def score(j):
            s = _nt(k_ref[0, j * blk:(j + 1) * blk, :], qh) + kbias
            if j == c:
                return jnp.where(causal, s, NEG_INF)
            return s + (drop[j:j + 1, :] + slopes[h] * float((j - c) * blk))

        def done(out):
            outs[c, h] = out
            if h == MOBA_HEADS - 1:
                full = jnp.concatenate([outs[c, hh] for hh in range(MOBA_HEADS)], axis=0)
                o_ref[0, c * blk:(c + 1) * blk, :] = full.T.astype(BF16)

        vt = lambda j: vt_ref[0, HEAD_DIM * h:HEAD_DIM * (h + 1), j * blk:(j + 1) * blk]
        return _Unit(s_ref.at[u % 3], c + 1, blk, blk, score, vt, done).steps

    _interleave([functools.partial(unit, c, h, MOBA_HEADS * c + h) for c in range(nb) for h in range(MOBA_HEADS)])


def _moba_attention(qb, kb, vbt, slopes):
    B, S, hw = qb.shape
    blk = MOBA_BLOCK
    nb = S // blk
    nbp = -(-nb // 8) * 8
    n_sel = max(1, min(MOBA_TOPK, nb - 1))
    full = pl.BlockSpec((1, S, hw), lambda b: (b, 0, 0))
    return pl.pallas_call(
        functools.partial(_moba_kernel, nb=nb, nbp=nbp, n_sel=n_sel, slopes=slopes),
        grid=(B,),
        in_specs=[full, full, pl.BlockSpec((1, hw, S), lambda b: (b, 0, 0))],
        out_specs=full,
        out_shape=jax.ShapeDtypeStruct((B, S, hw), BF16),
        scratch_shapes=[pltpu.VMEM((3, S, blk), F32)],
        compiler_params=_params("arbitrary"),
        name="moba_attn",
    )(qb, kb, vbt)


INT_MIN = -2 ** 31


def _kth_largest_key(key_ref, hi_ref, lo_ref, rows, n_chunks, tq, k):
    i16, i32 = jnp.int16, jnp.int32
    one, none = jnp.int16(1), jnp.int16(0)

    def count(ref, pred):
        acc = None
        for j in range(n_chunks):
            f = _fold_rows(jnp.where(pred(ref[rows(j), :]), one, none), jnp.add, 16)
            acc = f if acc is None else acc + f
        return jnp.sum(acc.astype(i32), axis=0, keepdims=True)

    zero = jnp.zeros((1, tq), i32)
    hi = jnp.where(count(hi_ref, lambda v: v >= none) >= k, zero, i32(-2 ** 15))

    def step_hi(it, hi):
        cand = hi | lax.shift_left(i32(1), i32(14) - it)
        c16 = cand.astype(i16)
        return jnp.where(count(hi_ref, lambda v: v >= c16) >= k, cand, hi)

    hi = lax.fori_loop(0, 15, step_hi, hi)
    h16 = hi.astype(i16)
    for j in range(n_chunks):
        low = (key_ref[rows(j), :] & i32(0xFFFF)) - i32(2 ** 15)
        lo_ref[rows(j), :] = jnp.where(hi_ref[rows(j), :] == h16, low.astype(i16), i16(-2 ** 15))
    above = count(hi_ref, lambda v: v > h16)

    def step_lo(it, lo):
        cand = lo | lax.shift_left(i32(1), i32(15) - it)
        c16 = (cand - i32(2 ** 15)).astype(i16)
        return jnp.where(above + count(lo_ref, lambda v: v >= c16) >= k, cand, lo)

    lo = lax.fori_loop(0, 16, step_lo, zero)
    return lax.shift_left(hi, 16) | lo


def _dsa_kernel(q_ref, k_ref, vt_ref, iq_ref, ik_ref, iwt_ref, kpos_ref, o_ref, key_ref, hi_ref, lo_ref,
                mb_ref, s_ref, *, tq, ch, per, ncls, n_keep, slopes, idx_scale):
    i = pl.program_id(1)
    hw = DSA_HEADS * HEAD_DIM
    q0 = i * tq
    qpos = q0 + lax.broadcasted_iota(jnp.int32, (1, tq), 1)
    krow = lax.broadcasted_iota(jnp.int32, (ch, 1), 0)
    lane = lax.broadcasted_iota(jnp.int32, (1, LANES), 1)
    lane_h = lax.broadcasted_iota(jnp.int32, (1, hw), 1)
    grp = LANES // IDX_DIM

    def body(c):
        n_chunks = (c + 1) * per * tq // ch
        rows = lambda j: slice(j * ch, (j + 1) * ch)
        iwt = iwt_ref[0]
        qms = []
        for hh in range(IDX_HEADS):
            g, r = divmod(hh, grp)
            iqg = iq_ref[0, :, LANES * g:LANES * (g + 1)]
            qms.append(jnp.where((lane >= IDX_DIM * r) & (lane < IDX_DIM * (r + 1)), iqg, jnp.zeros_like(iqg)))
        for j in range(n_chunks):
            ikc = ik_ref[0, rows(j), :]
            score = None
            for hh in range(IDX_HEADS):
                term = jnp.maximum(_nt(ikc, qms[hh]), 0.0) * iwt[hh:hh + 1, :]
                score = term if score is None else score + term
            score = jnp.where(krow + j * ch <= qpos, score * idx_scale, NEG_INF)
            bits = lax.bitcast_convert_type(score, jnp.int32)
            key = jnp.where(bits >= 0, bits, bits ^ jnp.int32(0x7FFFFFFF))
            key_ref[rows(j), :] = key
            hi_ref[rows(j), :] = lax.shift_right_arithmetic(key, 16).astype(jnp.int16)

        def mask_at_least(thr):
            for j in range(n_chunks):
                keep = (key_ref[rows(j), :] >= thr) & (krow + j * ch <= qpos)
                mb_ref[rows(j), :] = jnp.where(keep, 0.0, NEG_INF)

        def count(pred):
            acc = None
            for j in range(n_chunks):
                f = _fold_rows(jnp.where(pred(key_ref[rows(j), :]), 1.0, 0.0), jnp.add)
                acc = f if acc is None else acc + f
            return jnp.sum(acc, axis=0, keepdims=True)

        if n_chunks * ch <= n_keep:
            mask_at_least(jnp.full((1, tq), INT_MIN, jnp.int32))
        else:
            thr = _kth_largest_key(key_ref, hi_ref, lo_ref, rows, n_chunks, tq, n_keep)
            tied = jnp.max(count(lambda v: v >= thr)) > float(n_keep)
            pl.when(jnp.logical_not(tied))(functools.partial(mask_at_least, thr))

            @pl.when(tied)
            def _():
                need = float(n_keep) - count(lambda v: v > thr)
                tri = jnp.where(lax.broadcasted_iota(jnp.int32, (ch, ch), 1)
                                <= lax.broadcasted_iota(jnp.int32, (ch, ch), 0), 1.0, 0.0).astype(BF16)

                def chunk(j, seen):
                    at = pl.ds(pl.multiple_of(j * ch, ch), ch)
                    key = key_ref[at, :]
                    eq = jnp.where(key == thr, 1.0, 0.0)
                    rank = _mm(tri, eq.astype(BF16)) + seen
                    keep = (key > thr) | ((key == thr) & (rank <= need))
                    keep = keep & (krow + j * ch <= qpos)
                    mb_ref[at, :] = jnp.where(keep, 0.0, NEG_INF)
                    return seen + jnp.sum(eq, axis=0, keepdims=True)

                lax.fori_loop(0, n_chunks, chunk, jnp.zeros((1, tq), F32))

        q = q_ref[0]
        q0f = q0.astype(F32)
        outs = {}

        def unit(h):
            hm = (lane_h >= HEAD_DIM * h) & (lane_h < HEAD_DIM * (h + 1))
            qh = jnp.where(hm, q, jnp.zeros_like(q))

            def score(j):
                kb = slopes[h] * (kpos_ref[rows(j), :] - q0f)
                kb = jnp.concatenate([kb] * (tq // LANES), axis=1)
                return _nt(k_ref[0, rows(j), :], qh) + kb + mb_ref[rows(j), :]

            def done(out):
                outs[h] = out
                if h == DSA_HEADS - 1:
                    full = jnp.concatenate([outs[hh] for hh in range(DSA_HEADS)], axis=0)
                    o_ref[0] = full.T.astype(BF16)

            vt = lambda j: vt_ref[0, HEAD_DIM * h:HEAD_DIM * (h + 1), rows(j)]
            return _Unit(s_ref.at[h], n_chunks, ch, tq, score, vt, done).steps

        _interleave([functools.partial(unit, h) for h in range(DSA_HEADS)])

    for c in range(ncls):
        pl.when((i >= c * per) & (i < (c + 1) * per))(functools.partial(body, c))


def _dsa_attention(qc, kc, vct, iq, ik, iwt, slopes, tq, ncls):
    B, S, hw = qc.shape
    nq = S // tq
    per = nq // ncls
    ch = tq
    n_keep = min(DSA_TOPK, S // 4)
    kpos = jnp.broadcast_to(jnp.arange(S, dtype=F32)[:, None], (S, LANES))
    full = lambda w: pl.BlockSpec((1, S, w), lambda b, i: (b, 0, 0))
    tile = lambda w: pl.BlockSpec((1, tq, w), lambda b, i: (b, i, 0))
    return pl.pallas_call(
        functools.partial(_dsa_kernel, tq=tq, ch=ch, per=per, ncls=ncls, n_keep=n_keep, slopes=slopes,
                          idx_scale=(IDX_DIM * IDX_HEADS) ** -0.5),
        grid=(B, nq),
        in_specs=[tile(hw), full(hw), pl.BlockSpec((1, hw, S), lambda b, i: (b, 0, 0)),
                  tile(IDX_HEADS * IDX_DIM), full(LANES), pl.BlockSpec((1, LANES, tq), lambda b, i: (b, 0, i)),
                  pl.BlockSpec((S, LANES), lambda b, i: (0, 0))],
        out_specs=tile(hw),
        out_shape=jax.ShapeDtypeStruct((B, S, hw), BF16),
        scratch_shapes=[pltpu.VMEM((S, tq), jnp.int32), pltpu.VMEM((S, tq), jnp.int16),
                        pltpu.VMEM((S, tq), jnp.int16), pltpu.VMEM((S, tq), F32),
                        pltpu.VMEM((DSA_HEADS, S, tq), F32)],
        compiler_params=_params("arbitrary", "arbitrary"),
        name="dsa_attn",
    )(qc, kc, vct, iq, ik, iwt, kpos)


def _merge_kernel(ya_ref, yb_ref, yc_ref, gt_ref, x_ref, mod_ref, wa_ref, wb_ref, wc_ref, wo_ref, o_ref):
    d = D_MODEL
    merged = gt_ref[0, :, 0:d].astype(F32) * _mm(ya_ref[0], wa_ref[...])
    merged = merged + gt_ref[0, :, d:2 * d].astype(F32) * _mm(yb_ref[0], wb_ref[...])
    merged = merged + gt_ref[0, :, 2 * d:3 * d].astype(F32) * _mm(yc_ref[0], wc_ref[...])
    o_ref[0] = x_ref[0] + mod_ref[0, 2:3, :] * _mm(merged.astype(BF16), wo_ref[...])


def _merge(ya, yb, yc, gt, x, mod, lw, tm):
    B, S, D = x.shape
    row = lambda w: pl.BlockSpec((1, tm, w), lambda b, i: (b, i, 0))
    consts = [lw["w_br_a"], lw["w_br_b"], lw["w_br_c"], lw["w_out"]]
    return pl.pallas_call(
        _merge_kernel,
        grid=(B, S // tm),
        in_specs=[row(ya.shape[-1]), row(yb.shape[-1]), row(yc.shape[-1]), row(3 * D), row(D),
                  pl.BlockSpec((1, 6, D), lambda b, i: (b, 0, 0))] + [_const_spec(c.shape) for c in consts],
        out_specs=row(D),
        out_shape=jax.ShapeDtypeStruct((B, S, D), F32),
        compiler_params=_params("arbitrary", "arbitrary"),
        name="merge_out",
    )(ya, yb, yc, gt, x, mod, *consts)


def _ffn_kernel(x_ref, mod_ref, n2_ref, wg_ref, wu_ref, wd_ref, o_ref):
    x = x_ref[0]
    h = (_rms(x, n2_ref[...]) * (1.0 + mod_ref[0, 4:5, :]) + mod_ref[0, 3:4, :]).astype(BF16)
    g = _mm(h, wg_ref[...])
    u = _mm(h, wu_ref[...])
    act = (g * jax.nn.sigmoid(g) * u).astype(BF16)
    o_ref[0] = x + mod_ref[0, 5:6, :] * _mm(act, wd_ref[...])


def _ffn(x, mod, lw, tm):
    B, S, D = x.shape
    row = pl.BlockSpec((1, tm, D), lambda b, i: (b, i, 0))
    consts = [lw["n2"], lw["w_g"], lw["w_u"], lw["w_d"]]
    return pl.pallas_call(
        _ffn_kernel,
        grid=(B, S // tm),
        in_specs=[row, pl.BlockSpec((1, 6, D), lambda b, i: (b, 0, 0))] + [_const_spec(c.shape) for c in consts],
        out_specs=row,
        out_shape=jax.ShapeDtypeStruct((B, S, D), F32),
        compiler_params=_params("arbitrary", "arbitrary"),
        name="swiglu",
    )(x, mod, *consts)


def _pad_cols(w, width):
    return jnp.pad(w, ((0, 0), (0, width - w.shape[1])))


def _layer_weights(l, norm1, w_in, g_cq, w_uq, g_ckv, w_ukv, qn_mla, kn_mla, qn_moba, kn_moba, qn_dsa,
                   kn_dsa, w_br_a, w_br_b, w_br_c, w_out, norm2, w_gu, w_down):
    D = D_MODEL
    w = w_in[l]
    o = 0
    cols = {}
    for name, wd in (("cq", MLA_Q_RANK), ("ckv", MLA_KV_RANK), ("kr", MLA_ROPE),
                     ("b", 3 * MOBA_HEADS * HEAD_DIM), ("c", 3 * DSA_HEADS * HEAD_DIM),
                     ("iq", IDX_HEADS * IDX_DIM), ("ik", IDX_DIM), ("iw", IDX_HEADS), ("g", 3 * D)):
        cols[name] = w[:, o:o + wd]
        o += wd
    kr_slot = jnp.pad(cols["kr"], ((0, 0), (MLA_NOPE, LANES - MLA_QK)))
    ik_rep = jnp.tile(cols["ik"], (1, LANES // IDX_DIM))
    hw = MOBA_HEADS * HEAD_DIM
    w_all = jnp.concatenate([cols["cq"], cols["ckv"], kr_slot, cols["b"][:, :2 * hw], cols["c"][:, :2 * hw],
                             cols["iq"], ik_rep, cols["g"]], axis=1).astype(BF16)
    w_t = jnp.concatenate([cols["b"][:, 2 * hw:], cols["c"][:, 2 * hw:], _pad_cols(cols["iw"], LANES)],
                          axis=1).T.astype(BF16)
    wuq = jnp.pad(w_uq[l].reshape(MLA_Q_RANK, MLA_HEADS, MLA_QK), ((0, 0), (0, 0), (0, LANES - MLA_QK)))
    wukv = w_ukv[l].reshape(MLA_KV_RANK, MLA_HEADS, MLA_NOPE + MLA_V)
    wk = jnp.pad(wukv[:, :, :MLA_NOPE], ((0, 0), (0, 0), (0, LANES - MLA_NOPE)))
    wv = wukv[:, :, MLA_NOPE:]
    row = lambda v: v.reshape(1, -1)
    return {
        "n1": row(norm1[l]), "w_all": w_all, "w_t": w_t, "g_cq": row(g_cq[l]),
        "w_uq": wuq.reshape(MLA_Q_RANK, MLA_HEADS * LANES).astype(BF16), "g_ckv": row(g_ckv[l]),
        "w_k": wk.reshape(MLA_KV_RANK, MLA_HEADS * LANES).astype(BF16),
        "w_v": wv.reshape(MLA_KV_RANK, MLA_HEADS * MLA_V).T.astype(BF16),
        "qn_a": row(jnp.pad(qn_mla[l] * MLA_QK ** -0.5, (0, LANES - MLA_QK))), "kn_a": row(jnp.pad(kn_mla[l], (0, LANES - MLA_QK))),
        "qn_b": row(jnp.tile(qn_moba[l] * HEAD_DIM ** -0.5, MOBA_HEADS)), "kn_b": row(jnp.tile(kn_moba[l], MOBA_HEADS)),
        "qn_c": row(jnp.tile(qn_dsa[l] * HEAD_DIM ** -0.5, DSA_HEADS)), "kn_c": row(jnp.tile(kn_dsa[l], DSA_HEADS)),
        "w_br_a": w_br_a[l].astype(BF16), "w_br_b": w_br_b[l].astype(BF16), "w_br_c": w_br_c[l].astype(BF16),
        "w_out": w_out[l].astype(BF16), "n2": row(norm2[l]),
        "w_g": w_gu[l][:, :D_FF].astype(BF16), "w_u": w_gu[l][:, D_FF:].astype(BF16),
        "w_d": w_down[l].astype(BF16),
    }


def _rope_tables(S):
    half = MLA_ROPE // 2
    freqs = ROPE_THETA ** (-jnp.arange(half, dtype=F32) / half)
    ang = jnp.arange(S, dtype=F32)[:, None] * freqs[None, :]
    cos, sin = jnp.cos(ang), jnp.sin(ang)
    zeros = lambda n: jnp.zeros((S, n), F32)
    tail = LANES - MLA_QK
    rc = jnp.concatenate([jnp.ones((S, MLA_NOPE), F32), cos, cos, jnp.ones((S, tail), F32)], axis=1)
    rs1 = jnp.concatenate([zeros(MLA_NOPE), -sin, zeros(half + tail)], axis=1)
    rs2 = jnp.concatenate([zeros(MLA_NOPE + half), sin, zeros(tail)], axis=1)
    return rc, rs1, rs2


def _tiles(S):
    dsa_q = min(256, S)
    return {"proj": min(256, S), "out": min(512, S), "mla_q": min(256, S), "dsa_q": dsa_q,
            "dsa_cls": min(4, S // dsa_q)}


def kernel(x, c, w_ada, b_ada, norm1, w_in, g_cq, w_uq, g_ckv, w_ukv, qn_mla, kn_mla, qn_moba, kn_moba, qn_dsa, kn_dsa, w_br_a, w_br_b, w_br_c, w_out, norm2, w_gu, w_down):
    B, S, D = x.shape
    L = w_ada.shape[0]
    assert D == D_MODEL and S % MOBA_BLOCK == 0
    n_slopes = MOBA_HEADS + DSA_HEADS
    slopes = [2.0 ** (-8.0 * (i + 1) / n_slopes) for i in range(n_slopes)]
    mod_all = _ada(c, w_ada.astype(BF16), b_ada).reshape(L, B, 6, D)
    rope_tabs = _rope_tables(S)
    seg = jnp.arange(MOBA_HEADS * HEAD_DIM) // HEAD_DIM
    bd = (seg[:, None] == seg[None, :]).astype(BF16)
    t = _tiles(S)
    for l in range(L):
        lw = _layer_weights(l, norm1, w_in, g_cq, w_uq, g_ckv, w_ukv, qn_mla, kn_mla, qn_moba, kn_moba,
                            qn_dsa, kn_dsa, w_br_a, w_br_b, w_br_c, w_out, norm2, w_gu, w_down)
        mod = mod_all[l]
        qa, ka, va, qb, kb, vb, qc, kc, vc, iq, ik, iw, gt = _inproj(x, mod, lw, rope_tabs, bd, t["proj"])
        ya = _mla_attention(qa, ka, va, tq=t["mla_q"])
        yb = _moba_attention(qb, kb, vb, tuple(slopes[0::2]))
        yc = _dsa_attention(qc, kc, vc, iq, ik, iw, tuple(slopes[1::2]), tq=t["dsa_q"], ncls=t["dsa_cls"])
        x = _merge(ya, yb, yc, gt, x, mod, lw, t["out"])
        x = _ffn(x, mod, lw, t["out"])
    return x
```

```python
import functools

import jax
import jax.numpy as jnp
from jax import lax
from jax.experimental import pallas as pl
from jax.experimental.pallas import tpu as pltpu

F32 = jnp.float32
BF16 = jnp.bfloat16

D_MODEL = 1024
HEAD_DIM = 64
RMS_EPS = 1e-6
MLA_HEADS = 8
MLA_NOPE = 64
MLA_ROPE = 32
MLA_QK = MLA_NOPE + MLA_ROPE
MLA_V = 64
MLA_Q_RANK = 768
MLA_KV_RANK = 256
ROPE_THETA = 10000.0
MOBA_HEADS = 4
MOBA_BLOCK = 256
MOBA_TOPK = 3
DSA_HEADS = 4
IDX_HEADS = 16
IDX_DIM = 32
DSA_TOPK = 256
D_FF = 2816

LANES = 128
VMEM_LIMIT = 56 * 1024 * 1024

OFF_CQ = 0
OFF_CKV = OFF_CQ + MLA_Q_RANK
OFF_KR = OFF_CKV + MLA_KV_RANK
OFF_B = OFF_KR + LANES
OFF_C = OFF_B + 2 * MOBA_HEADS * HEAD_DIM
OFF_IQ = OFF_C + 2 * DSA_HEADS * HEAD_DIM
OFF_IK = OFF_IQ + IDX_HEADS * IDX_DIM
N_ALL = OFF_IK + LANES
ROW_VB = 0
ROW_VC = ROW_VB + MOBA_HEADS * HEAD_DIM
ROW_IW = ROW_VC + DSA_HEADS * HEAD_DIM
N_T = ROW_IW + LANES


def _nt(a, b):
    return lax.dot_general(a, b, (((1,), (1,)), ((), ())), preferred_element_type=F32)


def _mm(a, b):
    return jnp.dot(a, b, preferred_element_type=F32)


def _split3(x):
    hi = x.astype(BF16)
    r = x - hi.astype(F32)
    mid = r.astype(BF16)
    lo = (r - mid.astype(F32)).astype(BF16)
    return hi, mid, lo


def _rms(x, g):
    return x * lax.rsqrt(jnp.mean(x * x, axis=-1, keepdims=True) + RMS_EPS) * g


def _const_spec(shape):
    nd = len(shape)
    return pl.BlockSpec(shape, lambda *_: (0,) * nd, pipeline_mode=pl.Buffered(1))


def _params(*sem):
    return pltpu.CompilerParams(dimension_semantics=sem, vmem_limit_bytes=VMEM_LIMIT)


def _ada_kernel(c_ref, w_ref, b_ref, o_ref):
    c = c_ref[...]
    cond = (c * jax.nn.sigmoid(c)).astype(BF16)
    o_ref[0] = _mm(cond, w_ref[0]) + b_ref[0]


def _ada(c, w_ada, b_ada):
    L, D, N = w_ada.shape
    B = c.shape[0]
    tn = 1024
    return pl.pallas_call(
        _ada_kernel,
        grid=(L, N // tn),
        in_specs=[
            pl.BlockSpec((B, D), lambda l, n: (0, 0)),
            pl.BlockSpec((1, D, tn), lambda l, n: (l, 0, n)),
            pl.BlockSpec((1, 1, tn), lambda l, n: (l, 0, n)),
        ],
        out_specs=pl.BlockSpec((1, B, tn), lambda l, n: (l, 0, n)),
        out_shape=jax.ShapeDtypeStruct((L, B, N), F32),
        compiler_params=_params("arbitrary", "arbitrary"),
        name="ada_mod",
    )(c, w_ada, b_ada.reshape(L, 1, N))


def _rope(x, c, s1, s2):
    return x * c + pltpu.roll(x, LANES - 16, 1) * s1 + pltpu.roll(x, 16, 1) * s2


def _inproj_kernel(x_ref, mod_ref, n1_ref, w_ref, wt_ref, gcq_ref, wuq_ref, gckv_ref, wk_ref, wv_ref,
                   qna_ref, kna_ref, rc_ref, rs1_ref, rs2_ref, bd_ref,
                   qnb_ref, knb_ref, qnc_ref, knc_ref,
                   qa_ref, ka_ref, va_ref, qb_ref, kb_ref, vb_ref, qc_ref, kc_ref, vc_ref,
                   iq_ref, ik_ref, iw_ref):
    x = x_ref[0]
    sh1 = mod_ref[0, 0:1, :]
    sc1 = mod_ref[0, 1:2, :]
    hb = (_rms(x, n1_ref[...]) * (1.0 + sc1) + sh1).astype(BF16)

    a = _mm(hb, w_ref[:, OFF_CQ:OFF_B])
    cqn = _rms(a[:, OFF_CQ:OFF_CKV], gcq_ref[...]).astype(BF16)
    ckvn = _rms(a[:, OFF_CKV:OFF_KR], gckv_ref[...]).astype(BF16)
    kr = a[:, OFF_KR:OFF_B]
    qraw = _mm(cqn, wuq_ref[...])
    kraw = _mm(ckvn, wk_ref[...])
    va_ref[0] = _nt(wv_ref[...], ckvn).astype(BF16)
    rc, rs1, rs2 = rc_ref[...], rs1_ref[...], rs2_ref[...]
    qna, kna = qna_ref[...], kna_ref[...]
    inv_qk = 1.0 / MLA_QK
    for h in range(MLA_HEADS):
        sl = slice(LANES * h, LANES * (h + 1))
        qh = qraw[:, sl]
        qh = qh * lax.rsqrt(jnp.sum(qh * qh, axis=-1, keepdims=True) * inv_qk + RMS_EPS) * qna
        qa_ref[0, :, sl] = _rope(qh, rc, rs1, rs2).astype(BF16)
        kh = kraw[:, sl] + kr
        kh = kh * lax.rsqrt(jnp.sum(kh * kh, axis=-1, keepdims=True) * inv_qk + RMS_EPS) * kna
        ka_ref[0, :, sl] = _rope(kh, rc, rs1, rs2).astype(BF16)

    bd = bd_ref[...]

    def segnorm(v, g):
        ss = sum(_mm(part, bd) for part in _split3(v * v))
        return v * lax.rsqrt(ss * (1.0 / HEAD_DIM) + RMS_EPS) * g

    hw = MOBA_HEADS * HEAD_DIM
    pb = _mm(hb, w_ref[:, OFF_B:OFF_C])
    qb_ref[0] = segnorm(pb[:, 0:hw], qnb_ref[...]).astype(BF16)
    kb_ref[0] = segnorm(pb[:, hw:2 * hw], knb_ref[...]).astype(BF16)
    pc = _mm(hb, w_ref[:, OFF_C:OFF_IQ])
    qc_ref[0] = segnorm(pc[:, 0:hw], qnc_ref[...]).astype(BF16)
    kc_ref[0] = segnorm(pc[:, hw:2 * hw], knc_ref[...]).astype(BF16)

    pt = _nt(wt_ref[...], hb)
    vb_ref[0] = pt[ROW_VB:ROW_VC, :].astype(BF16)
    vc_ref[0] = pt[ROW_VC:ROW_IW, :].astype(BF16)
    iw_ref[0] = pt[ROW_IW:N_T, :]

    pi = _mm(hb, w_ref[:, OFF_IQ:N_ALL])
    iq_ref[0] = pi[:, 0:OFF_IK - OFF_IQ].astype(BF16)
    ik_ref[0] = pi[:, OFF_IK - OFF_IQ:N_ALL - OFF_IQ].astype(BF16)


def _inproj(x, mod, lw, rope_tabs, bd, tm):
    B, S, D = x.shape
    hw = MOBA_HEADS * HEAD_DIM
    row = lambda w: pl.BlockSpec((1, tm, w), lambda b, i: (b, i, 0))
    tab = pl.BlockSpec((tm, LANES), lambda b, i: (i, 0))
    consts = [lw["n1"], lw["w_all"], lw["w_t"], lw["g_cq"], lw["w_uq"], lw["g_ckv"], lw["w_k"], lw["w_v"],
              lw["qn_a"], lw["kn_a"]]
    consts2 = [bd, lw["qn_b"], lw["kn_b"], lw["qn_c"], lw["kn_c"]]
    widths = [MLA_HEADS * LANES, MLA_HEADS * LANES, MLA_HEADS * MLA_V, hw, hw, hw, hw, hw, hw,
              IDX_HEADS * IDX_DIM, LANES, LANES]
    dtypes = [BF16] * 11 + [F32]
    transposed = [False, False, True] * 3 + [False, False, True]
    col = lambda w: pl.BlockSpec((1, w, tm), lambda b, i: (b, 0, i))
    return pl.pallas_call(
        _inproj_kernel,
        grid=(B, S // tm),
        in_specs=[row(D), pl.BlockSpec((1, 6, D), lambda b, i: (b, 0, 0))]
        + [_const_spec(c.shape) for c in consts] + [tab, tab, tab]
        + [_const_spec(c.shape) for c in consts2],
        out_specs=[col(w) if tr else row(w) for w, tr in zip(widths, transposed)],
        out_shape=[jax.ShapeDtypeStruct((B, w, S) if tr else (B, S, w), dt)
                   for w, dt, tr in zip(widths, dtypes, transposed)],
        compiler_params=_params("arbitrary", "arbitrary"),
        name="in_proj",
    )(x, mod, *consts, *rope_tabs, *consts2)


NEG_INF = float("-inf")


def _fold_rows(x, op, n=8):
    out = x[0:n, :]
    for t in range(1, x.shape[0] // n):
        out = op(out, x[n * t:n * (t + 1), :])
    return out


def _interleave(units):
    pending = []
    for make in units:
        first, second = make()
        for t in range(max(len(first), len(pending))):
            if t < len(first):
                first[t]()
            if t < len(pending):
                pending[t]()
        pending = second
    for step in pending:
        step()


class _Unit:
    def __init__(self, logits, n_chunks, ch, tq, score, vt_chunk, done):
        self.m8 = jnp.full((8, tq), NEG_INF, F32)
        self.l8 = None
        self.acc = None

        def first(j):
            s = score(j)
            logits[j * ch:(j + 1) * ch, :] = s
            self.m8 = jnp.maximum(self.m8, _fold_rows(s, jnp.maximum))

        def second(j):
            if j == 0:
                self.m = jnp.max(self.m8, axis=0, keepdims=True)
            p = jnp.exp(logits[j * ch:(j + 1) * ch, :] - self.m)
            f = _fold_rows(p, jnp.add)
            pv = _mm(vt_chunk(j), p.astype(BF16))
            self.l8 = f if self.l8 is None else self.l8 + f
            self.acc = pv if self.acc is None else self.acc + pv
            if j == n_chunks - 1:
                done(self.acc / jnp.sum(self.l8, axis=0, keepdims=True))

        self.steps = ([functools.partial(first, j) for j in range(n_chunks)],
                      [functools.partial(second, j) for j in range(n_chunks)])


def _mla_kernel(q_ref, k_ref, vt_ref, o_ref, s_ref, *, tq, nq):
    krow = lax.broadcasted_iota(jnp.int32, (tq, tq), 0)
    qcol = lax.broadcasted_iota(jnp.int32, (tq, tq), 1)
    causal = krow <= qcol
    outs = {}

    def unit(c, hh, u):
        sl = slice(LANES * hh, LANES * (hh + 1))
        q = q_ref[0, c * tq:(c + 1) * tq, sl]

        def score(j):
            s = _nt(k_ref[0, j * tq:(j + 1) * tq, sl], q)
            return jnp.where(causal, s, NEG_INF) if j == c else s

        def done(out):
            outs[c, hh] = out
            if hh == 1:
                pair = jnp.concatenate([outs[c, 0], outs[c, 1]], axis=0)
                o_ref[0, c * tq:(c + 1) * tq, :] = pair.T.astype(BF16)

        vt = lambda j: vt_ref[0, MLA_V * hh:MLA_V * (hh + 1), j * tq:(j + 1) * tq]
        return _Unit(s_ref.at[u % 3], c + 1, tq, tq, score, vt, done).steps

    _interleave([functools.partial(unit, c, hh, 2 * c + hh) for c in range(nq) for hh in range(2)])


def _mla_attention(qa, ka, vat, tq):
    B, S, _ = qa.shape
    pairs = MLA_HEADS // 2
    nq = S // tq
    wide = pl.BlockSpec((1, S, 2 * LANES), lambda b, h: (b, 0, h))
    return pl.pallas_call(
        functools.partial(_mla_kernel, tq=tq, nq=nq),
        grid=(B, pairs),
        in_specs=[wide, wide, pl.BlockSpec((1, 2 * MLA_V, S), lambda b, h: (b, h, 0))],
        out_specs=pl.BlockSpec((1, S, 2 * MLA_V), lambda b, h: (b, 0, h)),
        out_shape=jax.ShapeDtypeStruct((B, S, MLA_HEADS * MLA_V), BF16),
        scratch_shapes=[pltpu.VMEM((3, S, tq), F32)],
        compiler_params=_params("arbitrary", "arbitrary"),
        name="mla_attn",
    )(qa, ka, vat)


def _moba_kernel(q_ref, k_ref, vt_ref, o_ref, s_ref, *, nb, nbp, n_sel, slopes):
    blk = MOBA_BLOCK
    hw = MOBA_HEADS * HEAD_DIM
    lane = lax.broadcasted_iota(jnp.int32, (1, hw), 1)

    means = [jnp.mean(k_ref[0, n * blk:(n + 1) * blk, :].astype(F32), axis=0, keepdims=True)
             for n in range(nb)]
    km = jnp.concatenate(means + [jnp.zeros((1, hw), F32)] * (nbp - nb), axis=0)
    pieces = []
    for h in range(MOBA_HEADS):
        hm = (lane >= HEAD_DIM * h) & (lane < HEAD_DIM * (h + 1))
        pieces += [p.astype(F32) for p in _split3(jnp.where(hm, km, 0.0))]
    km_stack = jnp.concatenate(pieces, axis=0).astype(BF16)

    krow = lax.broadcasted_iota(jnp.int32, (blk, blk), 0)
    qcol = lax.broadcasted_iota(jnp.int32, (blk, blk), 1)
    causal = krow <= qcol
    krow_f = krow.astype(F32)
    bidx = lax.broadcasted_iota(jnp.int32, (nbp, blk), 0)

    gates = {}
    outs = {}

    def unit(c, h, u):
        q = q_ref[0, c * blk:(c + 1) * blk, :]
        hm = (lane >= HEAD_DIM * h) & (lane < HEAD_DIM * (h + 1))
        qh = jnp.where(hm, q, jnp.zeros_like(q))
        kbias = slopes[h] * krow_f
        if c > 0:
            if h == 0:
                gates[c] = _nt(km_stack, q)
            past = bidx < c
            g = sum(gates[c][(3 * h + p) * nbp:(3 * h + p + 1) * nbp, :] for p in range(3))
            g = jnp.where(past, g, NEG_INF)
            rank = jnp.zeros((nbp, blk), F32)
            for n2 in range(c):
                gm = g[n2:n2 + 1, :]
                beats = (gm > g) | ((gm == g) & (bidx > n2))
                rank = rank + jnp.where(beats, 1.0, 0.0)
            drop = jnp.where(past & (rank < n_sel), 0.0, NEG_INF)

        def score(j):
            s = _nt(k_ref[0, j * blk:(j + 1) * blk, :], qh) + kbias
            if j == c:
                return jnp.where(causal, s, NEG_INF)
            return s + (drop[j:j + 1, :] + slopes[h] * float((j - c) * blk))

        def done(out):
            outs[c, h] = out
            if h == MOBA_HEADS - 1:
                full = jnp.concatenate([outs[c, hh] for hh in range(MOBA_HEADS)], axis=0)
                o_ref[0, c * blk:(c + 1) * blk, :] = full.T.astype(BF16)

        vt = lambda j: vt_ref[0, HEAD_DIM * h:HEAD_DIM * (h + 1), j * blk:(j + 1) * blk]
        return _Unit(s_ref.at[u % 3], c + 1, blk, blk, score, vt, done).steps

    _interleave([functools.partial(unit, c, h, MOBA_HEADS * c + h) for c in range(nb) for h in range(MOBA_HEADS)])


def _moba_attention(qb, kb, vbt, slopes):
    B, S, hw = qb.shape
    blk = MOBA_BLOCK
    nb = S // blk
    nbp = -(-nb // 8) * 8
    n_sel = max(1, min(MOBA_TOPK, nb - 1))
    full = pl.BlockSpec((1, S, hw), lambda b: (b, 0, 0))
    return pl.pallas_call(
        functools.partial(_moba_kernel, nb=nb, nbp=nbp, n_sel=n_sel, slopes=slopes),
        grid=(B,),
        in_specs=[full, full, pl.BlockSpec((1, hw, S), lambda b: (b, 0, 0))],
        out_specs=full,
        out_shape=jax.ShapeDtypeStruct((B, S, hw), BF16),
        scratch_shapes=[pltpu.VMEM((3, S, blk), F32)],
        compiler_params=_params("arbitrary"),
        name="moba_attn",
    )(qb, kb, vbt)


INT_MIN = -2 ** 31


def _kth_largest_key(key_ref, hi_ref, lo_ref, rows, n_chunks, tq, k):
    i16, i32 = jnp.int16, jnp.int32
    one, none = jnp.int16(1), jnp.int16(0)

    def count(ref, pred):
        acc = None
        for j in range(n_chunks):
            f = _fold_rows(jnp.where(pred(ref[rows(j), :]), one, none), jnp.add, 16)
            acc = f if acc is None else acc + f
        return jnp.sum(acc.astype(i32), axis=0, keepdims=True)

    zero = jnp.zeros((1, tq), i32)
    hi = jnp.where(count(hi_ref, lambda v: v >= none) >= k, zero, i32(-2 ** 15))

    def step_hi(it, hi):
        cand = hi | lax.shift_left(i32(1), i32(14) - it)
        c16 = cand.astype(i16)
        return jnp.where(count(hi_ref, lambda v: v >= c16) >= k, cand, hi)

    hi = lax.fori_loop(0, 15, step_hi, hi)
    h16 = hi.astype(i16)
    for j in range(n_chunks):
        low = (key_ref[rows(j), :] & i32(0xFFFF)) - i32(2 ** 15)
        lo_ref[rows(j), :] = jnp.where(hi_ref[rows(j), :] == h16, low.astype(i16), i16(-2 ** 15))
    above = count(hi_ref, lambda v: v > h16)

    def step_lo(it, lo):
        cand = lo | lax.shift_left(i32(1), i32(15) - it)
        c16 = (cand - i32(2 ** 15)).astype(i16)
        return jnp.where(above + count(lo_ref, lambda v: v >= c16) >= k, cand, lo)

    lo = lax.fori_loop(0, 16, step_lo, zero)
    return lax.shift_left(hi, 16) | lo


def _dsa_kernel(q_ref, k_ref, vt_ref, iq_ref, ik_ref, iwt_ref, kpos_ref, o_ref, key_ref, hi_ref, lo_ref,
                mb_ref, s_ref, *, tq, ch, per, ncls, n_keep, slopes, idx_scale):
    i = pl.program_id(1)
    hw = DSA_HEADS * HEAD_DIM
    q0 = i * tq
    qpos = q0 + lax.broadcasted_iota(jnp.int32, (1, tq), 1)
    krow = lax.broadcasted_iota(jnp.int32, (ch, 1), 0)
    lane = lax.broadcasted_iota(jnp.int32, (1, LANES), 1)
    lane_h = lax.broadcasted_iota(jnp.int32, (1, hw), 1)
    grp = LANES // IDX_DIM

    def body(c):
        n_chunks = (c + 1) * per * tq // ch
        rows = lambda j: slice(j * ch, (j + 1) * ch)
        iwt = iwt_ref[0]
        qms = []
        for hh in range(IDX_HEADS):
            g, r = divmod(hh, grp)
            iqg = iq_ref[0, :, LANES * g:LANES * (g + 1)]
            qms.append(jnp.where((lane >= IDX_DIM * r) & (lane < IDX_DIM * (r + 1)), iqg, jnp.zeros_like(iqg)))
        for j in range(n_chunks):
            ikc = ik_ref[0, rows(j), :]
            score = None
            for hh in range(IDX_HEADS):
                term = jnp.maximum(_nt(ikc, qms[hh]), 0.0) * iwt[hh:hh + 1, :]
                score = term if score is None else score + term
            score = jnp.where(krow + j * ch <= qpos, score * idx_scale, NEG_INF)
            bits = lax.bitcast_convert_type(score, jnp.int32)
            key = jnp.where(bits >= 0, bits, bits ^ jnp.int32(0x7FFFFFFF))
            key_ref[rows(j), :] = key
            hi_ref[rows(j), :] = lax.shift_right_arithmetic(key, 16).astype(jnp.int16)

        def mask_at_least(thr):
            for j in range(n_chunks):
                keep = (key_ref[rows(j), :] >= thr) & (krow + j * ch <= qpos)
                mb_ref[rows(j), :] = jnp.where(keep, 0.0, NEG_INF)

        def count(pred):
            acc = None
            for j in range(n_chunks):
                f = _fold_rows(jnp.where(pred(key_ref[rows(j), :]), 1.0, 0.0), jnp.add)
                acc = f if acc is None else acc + f
            return jnp.sum(acc, axis=0, keepdims=True)

        if n_chunks * ch <= n_keep:
            mask_at_least(jnp.full((1, tq), INT_MIN, jnp.int32))
        else:
            thr = _kth_largest_key(key_ref, hi_ref, lo_ref, rows, n_chunks, tq, n_keep)
            tied = jnp.max(count(lambda v: v >= thr)) > float(n_keep)
            pl.when(jnp.logical_not(tied))(functools.partial(mask_at_least, thr))

            @pl.when(tied)
            def _():
                need = float(n_keep) - count(lambda v: v > thr)
                tri = jnp.where(lax.broadcasted_iota(jnp.int32, (ch, ch), 1)
                                <= lax.broadcasted_iota(jnp.int32, (ch, ch), 0), 1.0, 0.0).astype(BF16)

                def chunk(j, seen):
                    at = pl.ds(pl.multiple_of(j * ch, ch), ch)
                    key = key_ref[at, :]
                    eq = jnp.where(key == thr, 1.0, 0.0)
                    rank = _mm(tri, eq.astype(BF16)) + seen
                    keep = (key > thr) | ((key == thr) & (rank <= need))
                    keep = keep & (krow + j * ch <= qpos)
                    mb_ref[at, :] = jnp.where(keep, 0.0, NEG_INF)
                    return seen + jnp.sum(eq, axis=0, keepdims=True)

                lax.fori_loop(0, n_chunks, chunk, jnp.zeros((1, tq), F32))

        q = q_ref[0]
        q0f = q0.astype(F32)
        outs = {}

        def unit(h):
            hm = (lane_h >= HEAD_DIM * h) & (lane_h < HEAD_DIM * (h + 1))
            qh = jnp.where(hm, q, jnp.zeros_like(q))

            def score(j):
                kb = slopes[h] * (kpos_ref[rows(j), :] - q0f)
                kb = jnp.concatenate([kb] * (tq // LANES), axis=1)
                return _nt(k_ref[0, rows(j), :], qh) + kb + mb_ref[rows(j), :]

            def done(out):
                outs[h] = out
                if h == DSA_HEADS - 1:
                    full = jnp.concatenate([outs[hh] for hh in range(DSA_HEADS)], axis=0)
                    o_ref[0] = full.T.astype(BF16)

            vt = lambda j: vt_ref[0, HEAD_DIM * h:HEAD_DIM * (h + 1), rows(j)]
            return _Unit(s_ref.at[h], n_chunks, ch, tq, score, vt, done).steps

        _interleave([functools.partial(unit, h) for h in range(DSA_HEADS)])

    for c in range(ncls):
        pl.when((i >= c * per) & (i < (c + 1) * per))(functools.partial(body, c))


def _dsa_attention(qc, kc, vct, iq, ik, iwt, slopes, tq, ncls):
    B, S, hw = qc.shape
    nq = S // tq
    per = nq // ncls
    ch = tq
    n_keep = min(DSA_TOPK, S // 4)
    kpos = jnp.broadcast_to(jnp.arange(S, dtype=F32)[:, None], (S, LANES))
    full = lambda w: pl.BlockSpec((1, S, w), lambda b, i: (b, 0, 0))
    tile = lambda w: pl.BlockSpec((1, tq, w), lambda b, i: (b, i, 0))
    return pl.pallas_call(
        functools.partial(_dsa_kernel, tq=tq, ch=ch, per=per, ncls=ncls, n_keep=n_keep, slopes=slopes,
                          idx_scale=(IDX_DIM * IDX_HEADS) ** -0.5),
        grid=(B, nq),
        in_specs=[tile(hw), full(hw), pl.BlockSpec((1, hw, S), lambda b, i: (b, 0, 0)),
                  tile(IDX_HEADS * IDX_DIM), full(LANES), pl.BlockSpec((1, LANES, tq), lambda b, i: (b, 0, i)),
                  pl.BlockSpec((S, LANES), lambda b, i: (0, 0))],
        out_specs=tile(hw),
        out_shape=jax.ShapeDtypeStruct((B, S, hw), BF16),
        scratch_shapes=[pltpu.VMEM((S, tq), jnp.int32), pltpu.VMEM((S, tq), jnp.int16),
                        pltpu.VMEM((S, tq), jnp.int16), pltpu.VMEM((S, tq), F32),
                        pltpu.VMEM((DSA_HEADS, S, tq), F32)],
        compiler_params=_params("arbitrary", "arbitrary"),
        name="dsa_attn",
    )(qc, kc, vct, iq, ik, iwt, kpos)


def _post_kernel(ya_ref, yb_ref, yc_ref, x_ref, mod_ref, n1_ref, wgt_ref, wa_ref, wb_ref, wc_ref, wo_ref,
                 n2_ref, wg_ref, wu_ref, wd_ref, o_ref):
    d = D_MODEL
    x = x_ref[0]
    hb = (_rms(x, n1_ref[...]) * (1.0 + mod_ref[0, 1:2, :]) + mod_ref[0, 0:1, :]).astype(BF16)
    merged = jax.nn.sigmoid(_mm(hb, wgt_ref[:, 0:d])) * _mm(ya_ref[0], wa_ref[...])
    merged = merged + jax.nn.sigmoid(_mm(hb, wgt_ref[:, d:2 * d])) * _mm(yb_ref[0], wb_ref[...])
    merged = merged + jax.nn.sigmoid(_mm(hb, wgt_ref[:, 2 * d:3 * d])) * _mm(yc_ref[0], wc_ref[...])
    x1 = x + mod_ref[0, 2:3, :] * _mm(merged.astype(BF16), wo_ref[...])
    h2 = (_rms(x1, n2_ref[...]) * (1.0 + mod_ref[0, 4:5, :]) + mod_ref[0, 3:4, :]).astype(BF16)
    g = _mm(h2, wg_ref[...])
    u = _mm(h2, wu_ref[...])
    act = (g * jax.nn.sigmoid(g) * u).astype(BF16)
    o_ref[0] = x1 + mod_ref[0, 5:6, :] * _mm(act, wd_ref[...])


def _post(ya, yb, yc, x, mod, lw, tm):
    B, S, D = x.shape
    row = lambda w: pl.BlockSpec((1, tm, w), lambda b, i: (b, i, 0))
    consts = [lw["n1"], lw["w_gate"], lw["w_br_a"], lw["w_br_b"], lw["w_br_c"], lw["w_out"],
              lw["n2"], lw["w_g"], lw["w_u"], lw["w_d"]]
    return pl.pallas_call(
        _post_kernel,
        grid=(B, S // tm),
        in_specs=[row(ya.shape[-1]), row(yb.shape[-1]), row(yc.shape[-1]), row(D),
                  pl.BlockSpec((1, 6, D), lambda b, i: (b, 0, 0))] + [_const_spec(c.shape) for c in consts],
        out_specs=row(D),
        out_shape=jax.ShapeDtypeStruct((B, S, D), F32),
        compiler_params=_params("arbitrary", "arbitrary"),
        name="merge_swiglu",
    )(ya, yb, yc, x, mod, *consts)


def _pad_cols(w, width):
    return jnp.pad(w, ((0, 0), (0, width - w.shape[1])))


def _layer_weights(l, norm1, w_in, g_cq, w_uq, g_ckv, w_ukv, qn_mla, kn_mla, qn_moba, kn_moba, qn_dsa,
                   kn_dsa, w_br_a, w_br_b, w_br_c, w_out, norm2, w_gu, w_down):
    D = D_MODEL
    w = w_in[l]
    o = 0
    cols = {}
    for name, wd in (("cq", MLA_Q_RANK), ("ckv", MLA_KV_RANK), ("kr", MLA_ROPE),
                     ("b", 3 * MOBA_HEADS * HEAD_DIM), ("c", 3 * DSA_HEADS * HEAD_DIM),
                     ("iq", IDX_HEADS * IDX_DIM), ("ik", IDX_DIM), ("iw", IDX_HEADS), ("g", 3 * D)):
        cols[name] = w[:, o:o + wd]
        o += wd
    kr_slot = jnp.pad(cols["kr"], ((0, 0), (MLA_NOPE, LANES - MLA_QK)))
    ik_rep = jnp.tile(cols["ik"], (1, LANES // IDX_DIM))
    hw = MOBA_HEADS * HEAD_DIM
    w_all = jnp.concatenate([cols["cq"], cols["ckv"], kr_slot, cols["b"][:, :2 * hw], cols["c"][:, :2 * hw],
                             cols["iq"], ik_rep], axis=1).astype(BF16)
    w_t = jnp.concatenate([cols["b"][:, 2 * hw:], cols["c"][:, 2 * hw:], _pad_cols(cols["iw"], LANES)],
                          axis=1).T.astype(BF16)
    wuq = jnp.pad(w_uq[l].reshape(MLA_Q_RANK, MLA_HEADS, MLA_QK), ((0, 0), (0, 0), (0, LANES - MLA_QK)))
    wukv = w_ukv[l].reshape(MLA_KV_RANK, MLA_HEADS, MLA_NOPE + MLA_V)
    wk = jnp.pad(wukv[:, :, :MLA_NOPE], ((0, 0), (0, 0), (0, LANES - MLA_NOPE)))
    wv = wukv[:, :, MLA_NOPE:]
    row = lambda v: v.reshape(1, -1)
    return {
        "n1": row(norm1[l]), "w_all": w_all, "w_t": w_t, "w_gate": cols["g"].astype(BF16), "g_cq": row(g_cq[l]),
        "w_uq": wuq.reshape(MLA_Q_RANK, MLA_HEADS * LANES).astype(BF16), "g_ckv": row(g_ckv[l]),
        "w_k": wk.reshape(MLA_KV_RANK, MLA_HEADS * LANES).astype(BF16),
        "w_v": wv.reshape(MLA_KV_RANK, MLA_HEADS * MLA_V).T.astype(BF16),
        "qn_a": row(jnp.pad(qn_mla[l] * MLA_QK ** -0.5, (0, LANES - MLA_QK))), "kn_a": row(jnp.pad(kn_mla[l], (0, LANES - MLA_QK))),
        "qn_b": row(jnp.tile(qn_moba[l] * HEAD_DIM ** -0.5, MOBA_HEADS)), "kn_b": row(jnp.tile(kn_moba[l], MOBA_HEADS)),
        "qn_c": row(jnp.tile(qn_dsa[l] * HEAD_DIM ** -0.5, DSA_HEADS)), "kn_c": row(jnp.tile(kn_dsa[l], DSA_HEADS)),
        "w_br_a": w_br_a[l].astype(BF16), "w_br_b": w_br_b[l].astype(BF16), "w_br_c": w_br_c[l].astype(BF16),
        "w_out": w_out[l].astype(BF16), "n2": row(norm2[l]),
        "w_g": w_gu[l][:, :D_FF].astype(BF16), "w_u": w_gu[l][:, D_FF:].astype(BF16),
        "w_d": w_down[l].astype(BF16),
    }


def _rope_tables(S):
    half = MLA_ROPE // 2
    freqs = ROPE_THETA ** (-jnp.arange(half, dtype=F32) / half)
    ang = jnp.arange(S, dtype=F32)[:, None] * freqs[None, :]
    cos, sin = jnp.cos(ang), jnp.sin(ang)
    zeros = lambda n: jnp.zeros((S, n), F32)
    tail = LANES - MLA_QK
    rc = jnp.concatenate([jnp.ones((S, MLA_NOPE), F32), cos, cos, jnp.ones((S, tail), F32)], axis=1)
    rs1 = jnp.concatenate([zeros(MLA_NOPE), -sin, zeros(half + tail)], axis=1)
    rs2 = jnp.concatenate([zeros(MLA_NOPE + half), sin, zeros(tail)], axis=1)
    return rc, rs1, rs2


def _tiles(S):
    dsa_q = min(256, S)
    return {"proj": min(512, S), "out": min(256, S), "mla_q": min(256, S), "dsa_q": dsa_q,
            "dsa_cls": min(4, S // dsa_q)}


def kernel(x, c, w_ada, b_ada, norm1, w_in, g_cq, w_uq, g_ckv, w_ukv, qn_mla, kn_mla, qn_moba, kn_moba, qn_dsa, kn_dsa, w_br_a, w_br_b, w_br_c, w_out, norm2, w_gu, w_down):
    B, S, D = x.shape
    L = w_ada.shape[0]
    assert D == D_MODEL and S % MOBA_BLOCK == 0
    n_slopes = MOBA_HEADS + DSA_HEADS
    slopes = [2.0 ** (-8.0 * (i + 1) / n_slopes) for i in range(n_slopes)]
    mod_all = _ada(c, w_ada.astype(BF16), b_ada).reshape(L, B, 6, D)
    rope_tabs = _rope_tables(S)
    seg = jnp.arange(MOBA_HEADS * HEAD_DIM) // HEAD_DIM
    bd = (seg[:, None] == seg[None, :]).astype(BF16)
    t = _tiles(S)
    for l in range(L):
        lw = _layer_weights(l, norm1, w_in, g_cq, w_uq, g_ckv, w_ukv, qn_mla, kn_mla, qn_moba, kn_moba,
                            qn_dsa, kn_dsa, w_br_a, w_br_b, w_br_c, w_out, norm2, w_gu, w_down)
        mod = mod_all[l]
        qa, ka, va, qb, kb, vb, qc, kc, vc, iq, ik, iw = _inproj(x, mod, lw, rope_tabs, bd, t["proj"])
        ya = _mla_attention(qa, ka, va, tq=t["mla_q"])
        yb = _moba_attention(qb, kb, vb, tuple(slopes[0::2]))
        yc = _dsa_attention(qc, kc, vc, iq, ik, iw, tuple(slopes[1::2]), tq=t["dsa_q"], ncls=t["dsa_cls"])
        x = _post(ya, yb, yc, x, mod, lw, t["out"])
    return x
```

```python
import functools

import jax
import jax.numpy as jnp
from jax import lax
from jax.experimental import pallas as pl
from jax.experimental.pallas import tpu as pltpu

F32 = jnp.float32
BF16 = jnp.bfloat16

D_MODEL = 1024
HEAD_DIM = 64
RMS_EPS = 1e-6
MLA_HEADS = 8
MLA_NOPE = 64
MLA_ROPE = 32
MLA_QK = MLA_NOPE + MLA_ROPE
MLA_V = 64
MLA_Q_RANK = 768
MLA_KV_RANK = 256
ROPE_THETA = 10000.0
MOBA_HEADS = 4
MOBA_BLOCK = 256
MOBA_TOPK = 3
DSA_HEADS = 4
IDX_HEADS = 16
IDX_DIM = 32
DSA_TOPK = 256
D_FF = 2816

LANES = 128
VMEM_LIMIT = 56 * 1024 * 1024

OFF_CQ = 0
OFF_CKV = OFF_CQ + MLA_Q_RANK
OFF_KR = OFF_CKV + MLA_KV_RANK
OFF_B = OFF_KR + LANES
OFF_C = OFF_B + 2 * MOBA_HEADS * HEAD_DIM
OFF_IQ = OFF_C + 2 * DSA_HEADS * HEAD_DIM
OFF_IK = OFF_IQ + IDX_HEADS * IDX_DIM
OFF_G = OFF_IK + LANES
N_ALL = OFF_G + 3 * D_MODEL
ROW_VB = 0
ROW_VC = ROW_VB + MOBA_HEADS * HEAD_DIM
ROW_IW = ROW_VC + DSA_HEADS * HEAD_DIM
N_T = ROW_IW + LANES


def _nt(a, b):
    return lax.dot_general(a, b, (((1,), (1,)), ((), ())), preferred_element_type=F32)


def _mm(a, b):
    return jnp.dot(a, b, preferred_element_type=F32)


def _split3(x):
    hi = x.astype(BF16)
    r = x - hi.astype(F32)
    mid = r.astype(BF16)
    lo = (r - mid.astype(F32)).astype(BF16)
    return hi, mid, lo


def _rms(x, g):
    return x * lax.rsqrt(jnp.mean(x * x, axis=-1, keepdims=True) + RMS_EPS) * g


def _const_spec(shape):
    nd = len(shape)
    return pl.BlockSpec(shape, lambda *_: (0,) * nd, pipeline_mode=pl.Buffered(1))


def _layer_spec(shape, l, part=0):
    rest = (0,) * (len(shape) - 2) + (part,)
    return pl.BlockSpec((1,) + tuple(shape[1:]), lambda *_: (l,) + rest, pipeline_mode=pl.Buffered(1))


def _params(*sem):
    return pltpu.CompilerParams(dimension_semantics=sem, vmem_limit_bytes=VMEM_LIMIT)


def _ada_kernel(c_ref, w_ref, b_ref, o_ref):
    c = c_ref[...]
    cond = (c * jax.nn.sigmoid(c)).astype(BF16)
    o_ref[0] = _mm(cond, w_ref[0]) + b_ref[0]


def _ada(c, w_ada, b_ada):
    L, D, N = w_ada.shape
    B = c.shape[0]
    tn = 1024
    return pl.pallas_call(
        _ada_kernel,
        grid=(L, N // tn),
        in_specs=[
            pl.BlockSpec((B, D), lambda l, n: (0, 0)),
            pl.BlockSpec((1, D, tn), lambda l, n: (l, 0, n)),
            pl.BlockSpec((1, 1, tn), lambda l, n: (l, 0, n)),
        ],
        out_specs=pl.BlockSpec((1, B, tn), lambda l, n: (l, 0, n)),
        out_shape=jax.ShapeDtypeStruct((L, B, N), F32),
        compiler_params=_params("arbitrary", "arbitrary"),
        name="ada_mod",
    )(c, w_ada, b_ada.reshape(L, 1, N))


def _rope(x, c, s1, s2):
    return x * c + pltpu.roll(x, LANES - 16, 1) * s1 + pltpu.roll(x, 16, 1) * s2


def _inproj_kernel(x_ref, mod_ref, n1_ref, w_ref, wt_ref, gcq_ref, wuq_ref, gckv_ref, wk_ref, wv_ref,
                   qna_ref, kna_ref, rc_ref, rs1_ref, rs2_ref, bd_ref,
                   qnb_ref, knb_ref, qnc_ref, knc_ref,
                   qa_ref, ka_ref, va_ref, qb_ref, kb_ref, vb_ref, qc_ref, kc_ref, vc_ref,
                   iq_ref, ik_ref, iw_ref, gt_ref):
    x = x_ref[0]
    sh1 = mod_ref[0, 0:1, :]
    sc1 = mod_ref[0, 1:2, :]
    hb = (_rms(x, n1_ref[0]) * (1.0 + sc1) + sh1).astype(BF16)

    a = _mm(hb, w_ref[0, :, OFF_CQ:OFF_B])
    cqn = _rms(a[:, OFF_CQ:OFF_CKV], gcq_ref[0]).astype(BF16)
    ckvn = _rms(a[:, OFF_CKV:OFF_KR], gckv_ref[0]).astype(BF16)
    kr = a[:, OFF_KR:OFF_B]
    qraw = _mm(cqn, wuq_ref[0])
    kraw = _mm(ckvn, wk_ref[0])
    va_ref[0] = _nt(wv_ref[0], ckvn).astype(BF16)
    rc, rs1, rs2 = rc_ref[...], rs1_ref[...], rs2_ref[...]
    qna, kna = qna_ref[0], kna_ref[0]
    inv_qk = 1.0 / MLA_QK
    for h in range(MLA_HEADS):
        sl = slice(LANES * h, LANES * (h + 1))
        qh = qraw[:, sl]
        qh = qh * lax.rsqrt(jnp.sum(qh * qh, axis=-1, keepdims=True) * inv_qk + RMS_EPS) * qna
        qa_ref[0, :, sl] = _rope(qh, rc, rs1, rs2).astype(BF16)
        kh = kraw[:, sl] + kr
        kh = kh * lax.rsqrt(jnp.sum(kh * kh, axis=-1, keepdims=True) * inv_qk + RMS_EPS) * kna
        ka_ref[0, :, sl] = _rope(kh, rc, rs1, rs2).astype(BF16)

    bd = bd_ref[...]

    def segnorm(v, g):
        ss = sum(_mm(part, bd) for part in _split3(v * v))
        return v * lax.rsqrt(ss * (1.0 / HEAD_DIM) + RMS_EPS) * g

    hw = MOBA_HEADS * HEAD_DIM
    pb = _mm(hb, w_ref[0, :, OFF_B:OFF_C])
    qb_ref[0] = segnorm(pb[:, 0:hw], qnb_ref[0]).astype(BF16)
    kb_ref[0] = segnorm(pb[:, hw:2 * hw], knb_ref[0]).astype(BF16)
    pc = _mm(hb, w_ref[0, :, OFF_C:OFF_IQ])
    qc_ref[0] = segnorm(pc[:, 0:hw], qnc_ref[0]).astype(BF16)
    kc_ref[0] = segnorm(pc[:, hw:2 * hw], knc_ref[0]).astype(BF16)

    pt = _nt(wt_ref[0], hb)
    vb_ref[0] = pt[ROW_VB:ROW_VC, :].astype(BF16)
    vc_ref[0] = pt[ROW_VC:ROW_IW, :].astype(BF16)
    iw_ref[0] = pt[ROW_IW:N_T, :]

    pi = _mm(hb, w_ref[0, :, OFF_IQ:OFF_G])
    iq_ref[0] = pi[:, 0:OFF_IK - OFF_IQ].astype(BF16)
    ik_ref[0] = pi[:, OFF_IK - OFF_IQ:OFF_G - OFF_IQ].astype(BF16)
    gt_ref[0] = jax.nn.sigmoid(_mm(hb, w_ref[0, :, OFF_G:N_ALL])).astype(BF16)


def _inproj(x, mod, lw, l, rope_tabs, bd, tm):
    B, S, D = x.shape
    hw = MOBA_HEADS * HEAD_DIM
    row = lambda w: pl.BlockSpec((1, tm, w), lambda b, i: (b, i, 0))
    tab = pl.BlockSpec((tm, LANES), lambda b, i: (i, 0))
    consts = [lw["n1"], lw["w_all"], lw["w_t"], lw["g_cq"], lw["w_uq"], lw["g_ckv"], lw["w_k"], lw["w_v"],
              lw["qn_a"], lw["kn_a"]]
    consts2 = [lw["qn_b"], lw["kn_b"], lw["qn_c"], lw["kn_c"]]
    widths = [MLA_HEADS * LANES, MLA_HEADS * LANES, MLA_HEADS * MLA_V, hw, hw, hw, hw, hw, hw,
              IDX_HEADS * IDX_DIM, LANES, LANES, 3 * D_MODEL]
    dtypes = [BF16] * 11 + [F32, BF16]
    transposed = [False, False, True] * 3 + [False, False, True, False]
    col = lambda w: pl.BlockSpec((1, w, tm), lambda b, i: (b, 0, i))
    return pl.pallas_call(
        _inproj_kernel,
        grid=(B, S // tm),
        in_specs=[row(D), pl.BlockSpec((1, 6, D), lambda b, i: (b, 0, 0))]
        + [_layer_spec(c.shape, l) for c in consts] + [tab, tab, tab, _const_spec(bd.shape)]
        + [_layer_spec(c.shape, l) for c in consts2],
        out_specs=[col(w) if tr else row(w) for w, tr in zip(widths, transposed)],
        out_shape=[jax.ShapeDtypeStruct((B, w, S) if tr else (B, S, w), dt)
                   for w, dt, tr in zip(widths, dtypes, transposed)],
        compiler_params=_params("arbitrary", "arbitrary"),
        name="in_proj",
    )(x, mod, *consts, *rope_tabs, bd, *consts2)


NEG_INF = float("-inf")


def _fold_rows(x, op, n=8):
    out = x[0:n, :]
    for t in range(1, x.shape[0] // n):
        out = op(out, x[n * t:n * (t + 1), :])
    return out


def _interleave(units):
    pending = []
    for make in units:
        first, second = make()
        for t in range(max(len(first), len(pending))):
            if t < len(first):
                first[t]()
            if t < len(pending):
                pending[t]()
        pending = second
    for step in pending:
        step()


class _Unit:
    def __init__(self, logits, n_chunks, ch, tq, score, vt_chunk, done):
        self.m8 = jnp.full((8, tq), NEG_INF, F32)
        self.l8 = None
        self.acc = None

        def first(j):
            s = score(j)
            logits[j * ch:(j + 1) * ch, :] = s
            self.m8 = jnp.maximum(self.m8, _fold_rows(s, jnp.maximum))

        def second(j):
            if j == 0:
                self.m = jnp.max(self.m8, axis=0, keepdims=True)
            p = jnp.exp(logits[j * ch:(j + 1) * ch, :] - self.m)
            f = _fold_rows(p, jnp.add)
            pv = _mm(vt_chunk(j), p.astype(BF16))
            self.l8 = f if self.l8 is None else self.l8 + f
            self.acc = pv if self.acc is None else self.acc + pv
            if j == n_chunks - 1:
                done(self.acc / jnp.sum(self.l8, axis=0, keepdims=True))

        self.steps = ([functools.partial(first, j) for j in range(n_chunks)],
                      [functools.partial(second, j) for j in range(n_chunks)])


def _mla_kernel(q_ref, k_ref, vt_ref, o_ref, s_ref, *, tq, nq):
    krow = lax.broadcasted_iota(jnp.int32, (tq, tq), 0)
    qcol = lax.broadcasted_iota(jnp.int32, (tq, tq), 1)
    causal = krow <= qcol
    outs = {}

    def unit(c, hh, u):
        sl = slice(LANES * hh, LANES * (hh + 1))
        q = q_ref[0, c * tq:(c + 1) * tq, sl]

        def score(j):
            s = _nt(k_ref[0, j * tq:(j + 1) * tq, sl], q)
            return jnp.where(causal, s, NEG_INF) if j == c else s

        def done(out):
            outs[c, hh] = out
            if hh == 1:
                pair = jnp.concatenate([outs[c, 0], outs[c, 1]], axis=0)
                o_ref[0, c * tq:(c + 1) * tq, :] = pair.T.astype(BF16)

        vt = lambda j: vt_ref[0, MLA_V * hh:MLA_V * (hh + 1), j * tq:(j + 1) * tq]
        return _Unit(s_ref.at[u % 3], c + 1, tq, tq, score, vt, done).steps

    _interleave([functools.partial(unit, c, hh, 2 * c + hh) for c in range(nq) for hh in range(2)])


def _mla_attention(qa, ka, vat, tq):
    B, S, _ = qa.shape
    pairs = MLA_HEADS // 2
    nq = S // tq
    wide = pl.BlockSpec((1, S, 2 * LANES), lambda b, h: (b, 0, h))
    return pl.pallas_call(
        functools.partial(_mla_kernel, tq=tq, nq=nq),
        grid=(B, pairs),
        in_specs=[wide, wide, pl.BlockSpec((1, 2 * MLA_V, S), lambda b, h: (b, h, 0))],
        out_specs=pl.BlockSpec((1, S, 2 * MLA_V), lambda b, h: (b, 0, h)),
        out_shape=jax.ShapeDtypeStruct((B, S, MLA_HEADS * MLA_V), BF16),
        scratch_shapes=[pltpu.VMEM((3, S, tq), F32)],
        compiler_params=_params("arbitrary", "arbitrary"),
        name="mla_attn",
    )(qa, ka, vat)


def _moba_kernel(q_ref, k_ref, vt_ref, o_ref, s_ref, *, nb, nbp, n_sel, slopes):
    blk = MOBA_BLOCK
    hw = MOBA_HEADS * HEAD_DIM
    lane = lax.broadcasted_iota(jnp.int32, (1, hw), 1)

    means = [jnp.mean(k_ref[0, n * blk:(n + 1) * blk, :].astype(F32), axis=0, keepdims=True)
             for n in range(nb)]
    km = jnp.concatenate(means + [jnp.zeros((1, hw), F32)] * (nbp - nb), axis=0)
    pieces = []
    for h in range(MOBA_HEADS):
        hm = (lane >= HEAD_DIM * h) & (lane < HEAD_DIM * (h + 1))
        pieces += [p.astype(F32) for p in _split3(jnp.where(hm, km, 0.0))]
    km_stack = jnp.concatenate(pieces, axis=0).astype(BF16)

    krow = lax.broadcasted_iota(jnp.int32, (blk, blk), 0)
    qcol = lax.broadcasted_iota(jnp.int32, (blk, blk), 1)
    causal = krow <= qcol
    krow_f = krow.astype(F32)
    bidx = lax.broadcasted_iota(jnp.int32, (nbp, blk), 0)

    gates = {}
    outs = {}

    def unit(c, h, u):
        q = q_ref[0, c * blk:(c + 1) * blk, :]
        hm = (lane >= HEAD_DIM * h) & (lane < HEAD_DIM * (h + 1))
        qh = jnp.where(hm, q, jnp.zeros_like(q))
        kbias = slopes[h] * krow_f
        if c > 0:
            if h == 0:
                gates[c] = _nt(km_stack, q)
            past = bidx < c
            g = sum(gates[c][(3 * h + p) * nbp:(3 * h + p + 1) * nbp, :] for p in range(3))
            g = jnp.where(past, g, NEG_INF)
            rank = jnp.zeros((nbp, blk), F32)
            for n2 in range(c):
                gm = g[n2:n2 + 1, :]
                beats = (gm > g) | ((gm == g) & (bidx > n2))
                rank = rank + jnp.where(beats, 1.0, 0.0)
            drop = jnp.where(past & (rank < n_sel), 0.0, NEG_INF)

        def score(j):
            s = _nt(k_ref[0, j * blk:(j + 1) * blk, :], qh) + kbias
            if j == c:
                return jnp.where(causal, s, NEG_INF)
            return s + (drop[j:j + 1, :] + slopes[h] * float((j - c) * blk))

        def done(out):
            outs[c, h] = out
            if h == MOBA_HEADS - 1:
                full = jnp.concatenate([outs[c, hh] for hh in range(MOBA_HEADS)], axis=0)
                o_ref[0, c * blk:(c + 1) * blk, :] = full.T.astype(BF16)

        vt = lambda j: vt_ref[0, HEAD_DIM * h:HEAD_DIM * (h + 1), j * blk:(j + 1) * blk]
        return _Unit(s_ref.at[u % 3], c + 1, blk, blk, score, vt, done).steps

    _interleave([functools.partial(unit, c, h, MOBA_HEADS * c + h) for c in range(nb) for h in range(MOBA_HEADS)])


def _moba_attention(qb, kb, vbt, slopes):
    B, S, hw = qb.shape
    blk = MOBA_BLOCK
    nb = S // blk
    nbp = -(-nb // 8) * 8
    n_sel = max(1, min(MOBA_TOPK, nb - 1))
    full = pl.BlockSpec((1, S, hw), lambda b: (b, 0, 0))
    return pl.pallas_call(
        functools.partial(_moba_kernel, nb=nb, nbp=nbp, n_sel=n_sel, slopes=slopes),
        grid=(B,),
        in_specs=[full, full, pl.BlockSpec((1, hw, S), lambda b: (b, 0, 0))],
        out_specs=full,
        out_shape=jax.ShapeDtypeStruct((B, S, hw), BF16),
        scratch_shapes=[pltpu.VMEM((3, S, blk), F32)],
        compiler_params=_params("arbitrary"),
        name="moba_attn",
    )(qb, kb, vbt)


INT_MIN = -2 ** 31


def _kth_largest_key(key_ref, hi_ref, lo_ref, rows, n_chunks, tq, k):
    i16, i32 = jnp.int16, jnp.int32
    one, none = jnp.int16(1), jnp.int16(0)

    def count(ref, pred):
        acc = None
        for j in range(n_chunks):
            f = _fold_rows(jnp.where(pred(ref[rows(j), :]), one, none), jnp.add, 16)
            acc = f if acc is None else acc + f
        return jnp.sum(acc.astype(i32), axis=0, keepdims=True)

    zero = jnp.zeros((1, tq), i32)
    hi = jnp.where(count(hi_ref, lambda v: v >= none) >= k, zero, i32(-2 ** 15))

    def step_hi(it, hi):
        cand = hi | lax.shift_left(i32(1), i32(14) - it)
        c16 = cand.astype(i16)
        return jnp.where(count(hi_ref, lambda v: v >= c16) >= k, cand, hi)

    hi = lax.fori_loop(0, 15, step_hi, hi)
    h16 = hi.astype(i16)
    for j in range(n_chunks):
        low = (key_ref[rows(j), :] & i32(0xFFFF)) - i32(2 ** 15)
        lo_ref[rows(j), :] = jnp.where(hi_ref[rows(j), :] == h16, low.astype(i16), i16(-2 ** 15))
    above = count(hi_ref, lambda v: v > h16)

    def step_lo(it, lo):
        cand = lo | lax.shift_left(i32(1), i32(15) - it)
        c16 = (cand - i32(2 ** 15)).astype(i16)
        return jnp.where(above + count(lo_ref, lambda v: v >= c16) >= k, cand, lo)

    lo = lax.fori_loop(0, 16, step_lo, zero)
    return lax.shift_left(hi, 16) | lo


def _dsa_kernel(q_ref, k_ref, vt_ref, iq_ref, ik_ref, iwt_ref, kpos_ref, o_ref, key_ref, hi_ref, lo_ref,
                mb_ref, s_ref, *, tq, ch, per, ncls, n_keep, slopes, idx_scale):
    i = pl.program_id(1)
    hw = DSA_HEADS * HEAD_DIM
    q0 = i * tq
    qpos = q0 + lax.broadcasted_iota(jnp.int32, (1, tq), 1)
    krow = lax.broadcasted_iota(jnp.int32, (ch, 1), 0)
    lane = lax.broadcasted_iota(jnp.int32, (1, LANES), 1)
    lane_h = lax.broadcasted_iota(jnp.int32, (1, hw), 1)
    grp = LANES // IDX_DIM

    def body(c):
        n_chunks = (c + 1) * per * tq // ch
        rows = lambda j: slice(j * ch, (j + 1) * ch)
        iwt = iwt_ref[0]
        qms = []
        for hh in range(IDX_HEADS):
            g, r = divmod(hh, grp)
            iqg = iq_ref[0, :, LANES * g:LANES * (g + 1)]
            qms.append(jnp.where((lane >= IDX_DIM * r) & (lane < IDX_DIM * (r + 1)), iqg, jnp.zeros_like(iqg)))
        for j in range(n_chunks):
            ikc = ik_ref[0, rows(j), :]
            score = None
            for hh in range(IDX_HEADS):
                term = jnp.maximum(_nt(ikc, qms[hh]), 0.0) * iwt[hh:hh + 1, :]
                score = term if score is None else score + term
            score = jnp.where(krow + j * ch <= qpos, score * idx_scale, NEG_INF)
            bits = lax.bitcast_convert_type(score, jnp.int32)
            key = jnp.where(bits >= 0, bits, bits ^ jnp.int32(0x7FFFFFFF))
            key_ref[rows(j), :] = key
            hi_ref[rows(j), :] = lax.shift_right_arithmetic(key, 16).astype(jnp.int16)

        def mask_at_least(thr):
            for j in range(n_chunks):
                keep = (key_ref[rows(j), :] >= thr) & (krow + j * ch <= qpos)
                mb_ref[rows(j), :] = jnp.where(keep, 0.0, NEG_INF)

        def count(pred):
            acc = None
            for j in range(n_chunks):
                f = _fold_rows(jnp.where(pred(key_ref[rows(j), :]), 1.0, 0.0), jnp.add)
                acc = f if acc is None else acc + f
            return jnp.sum(acc, axis=0, keepdims=True)

        if n_chunks * ch <= n_keep:
            mask_at_least(jnp.full((1, tq), INT_MIN, jnp.int32))
        else:
            thr = _kth_largest_key(key_ref, hi_ref, lo_ref, rows, n_chunks, tq, n_keep)
            tied = jnp.max(count(lambda v: v >= thr)) > float(n_keep)
            pl.when(jnp.logical_not(tied))(functools.partial(mask_at_least, thr))

            @pl.when(tied)
            def _():
                need = float(n_keep) - count(lambda v: v > thr)
                tri = jnp.where(lax.broadcasted_iota(jnp.int32, (ch, ch), 1)
                                <= lax.broadcasted_iota(jnp.int32, (ch, ch), 0), 1.0, 0.0).astype(BF16)

                def chunk(j, seen):
                    at = pl.ds(pl.multiple_of(j * ch, ch), ch)
                    key = key_ref[at, :]
                    eq = jnp.where(key == thr, 1.0, 0.0)
                    rank = _mm(tri, eq.astype(BF16)) + seen
                    keep = (key > thr) | ((key == thr) & (rank <= need))
                    keep = keep & (krow + j * ch <= qpos)
                    mb_ref[at, :] = jnp.where(keep, 0.0, NEG_INF)
                    return seen + jnp.sum(eq, axis=0, keepdims=True)

                lax.fori_loop(0, n_chunks, chunk, jnp.zeros((1, tq), F32))

        q = q_ref[0]
        q0f = q0.astype(F32)
        outs = {}

        def unit(h):
            hm = (lane_h >= HEAD_DIM * h) & (lane_h < HEAD_DIM * (h + 1))
            qh = jnp.where(hm, q, jnp.zeros_like(q))

            def score(j):
                kb = slopes[h] * (kpos_ref[rows(j), :] - q0f)
                kb = jnp.concatenate([kb] * (tq // LANES), axis=1)
                return _nt(k_ref[0, rows(j), :], qh) + kb + mb_ref[rows(j), :]

            def done(out):
                outs[h] = out
                if h == DSA_HEADS - 1:
                    full = jnp.concatenate([outs[hh] for hh in range(DSA_HEADS)], axis=0)
                    o_ref[0] = full.T.astype(BF16)

            vt = lambda j: vt_ref[0, HEAD_DIM * h:HEAD_DIM * (h + 1), rows(j)]
            return _Unit(s_ref.at[h], n_chunks, ch, tq, score, vt, done).steps

        _interleave([functools.partial(unit, h) for h in range(DSA_HEADS)])

    for c in range(ncls):
        pl.when((i >= c * per) & (i < (c + 1) * per))(functools.partial(body, c))


def _dsa_attention(qc, kc, vct, iq, ik, iwt, slopes, tq, ncls):
    B, S, hw = qc.shape
    nq = S // tq
    per = nq // ncls
    ch = tq
    n_keep = min(DSA_TOPK, S // 4)
    kpos = jnp.broadcast_to(jnp.arange(S, dtype=F32)[:, None], (S, LANES))
    full = lambda w: pl.BlockSpec((1, S, w), lambda b, i: (b, 0, 0))
    tile = lambda w: pl.BlockSpec((1, tq, w), lambda b, i: (b, i, 0))
    return pl.pallas_call(
        functools.partial(_dsa_kernel, tq=tq, ch=ch, per=per, ncls=ncls, n_keep=n_keep, slopes=slopes,
                          idx_scale=(IDX_DIM * IDX_HEADS) ** -0.5),
        grid=(B, nq),
        in_specs=[tile(hw), full(hw), pl.BlockSpec((1, hw, S), lambda b, i: (b, 0, 0)),
                  tile(IDX_HEADS * IDX_DIM), full(LANES), pl.BlockSpec((1, LANES, tq), lambda b, i: (b, 0, i)),
                  pl.BlockSpec((S, LANES), lambda b, i: (0, 0))],
        out_specs=tile(hw),
        out_shape=jax.ShapeDtypeStruct((B, S, hw), BF16),
        scratch_shapes=[pltpu.VMEM((S, tq), jnp.int32), pltpu.VMEM((S, tq), jnp.int16),
                        pltpu.VMEM((S, tq), jnp.int16), pltpu.VMEM((S, tq), F32),
                        pltpu.VMEM((DSA_HEADS, S, tq), F32)],
        compiler_params=_params("arbitrary", "arbitrary"),
        name="dsa_attn",
    )(qc, kc, vct, iq, ik, iwt, kpos)


def _merge_kernel(ya_ref, yb_ref, yc_ref, gt_ref, x_ref, mod_ref, wa_ref, wb_ref, wc_ref, wo_ref, o_ref):
    d = D_MODEL
    merged = gt_ref[0, :, 0:d].astype(F32) * _mm(ya_ref[0], wa_ref[0])
    merged = merged + gt_ref[0, :, d:2 * d].astype(F32) * _mm(yb_ref[0], wb_ref[0])
    merged = merged + gt_ref[0, :, 2 * d:3 * d].astype(F32) * _mm(yc_ref[0], wc_ref[0])
    o_ref[0] = x_ref[0] + mod_ref[0, 2:3, :] * _mm(merged.astype(BF16), wo_ref[0])


def _merge(ya, yb, yc, gt, x, mod, lw, l, tm):
    B, S, D = x.shape
    row = lambda w: pl.BlockSpec((1, tm, w), lambda b, i: (b, i, 0))
    consts = [lw["w_br_a"], lw["w_br_b"], lw["w_br_c"], lw["w_out"]]
    return pl.pallas_call(
        _merge_kernel,
        grid=(B, S // tm),
        in_specs=[row(ya.shape[-1]), row(yb.shape[-1]), row(yc.shape[-1]), row(3 * D), row(D),
                  pl.BlockSpec((1, 6, D), lambda b, i: (b, 0, 0))] + [_layer_spec(c.shape, l) for c in consts],
        out_specs=row(D),
        out_shape=jax.ShapeDtypeStruct((B, S, D), F32),
        compiler_params=_params("arbitrary", "arbitrary"),
        name="merge_out",
    )(ya, yb, yc, gt, x, mod, *consts)


def _ffn_kernel(x_ref, mod_ref, n2_ref, wg_ref, wu_ref, wd_ref, o_ref):
    x = x_ref[0]
    h = (_rms(x, n2_ref[0]) * (1.0 + mod_ref[0, 4:5, :]) + mod_ref[0, 3:4, :]).astype(BF16)
    g = _mm(h, wg_ref[0])
    u = _mm(h, wu_ref[0])
    act = (g * jax.nn.sigmoid(g) * u).astype(BF16)
    o_ref[0] = x + mod_ref[0, 5:6, :] * _mm(act, wd_ref[0])


def _ffn(x, mod, lw, l, tm):
    B, S, D = x.shape
    row = pl.BlockSpec((1, tm, D), lambda b, i: (b, i, 0))
    gu = lw["w_gu"]
    half = (gu.shape[0], gu.shape[1], gu.shape[2] // 2)
    consts = [lw["n2"], gu, gu, lw["w_d"]]
    return pl.pallas_call(
        _ffn_kernel,
        grid=(B, S // tm),
        in_specs=[row, pl.BlockSpec((1, 6, D), lambda b, i: (b, 0, 0)), _layer_spec(lw["n2"].shape, l),
                  _layer_spec(half, l, 0), _layer_spec(half, l, 1), _layer_spec(lw["w_d"].shape, l)],
        out_specs=row,
        out_shape=jax.ShapeDtypeStruct((B, S, D), F32),
        compiler_params=_params("arbitrary", "arbitrary"),
        name="swiglu",
    )(x, mod, *consts)


def _pack_weights(norm1, w_in, g_cq, w_uq, g_ckv, w_ukv, qn_mla, kn_mla, qn_moba, kn_moba, qn_dsa, kn_dsa,
                  w_br_a, w_br_b, w_br_c, w_out, norm2, w_gu, w_down):
    L = w_in.shape[0]
    o = 0
    cols = {}
    for name, wd in (("cq", MLA_Q_RANK), ("ckv", MLA_KV_RANK), ("kr", MLA_ROPE),
                     ("b", 3 * MOBA_HEADS * HEAD_DIM), ("c", 3 * DSA_HEADS * HEAD_DIM),
                     ("iq", IDX_HEADS * IDX_DIM), ("ik", IDX_DIM), ("iw", IDX_HEADS), ("g", 3 * D_MODEL)):
        cols[name] = w_in[:, :, o:o + wd]
        o += wd
    pad_last = lambda v, lo, hi: jnp.pad(v, ((0, 0),) * (v.ndim - 1) + ((lo, hi),))
    kr_slot = pad_last(cols["kr"], MLA_NOPE, LANES - MLA_QK)
    ik_rep = jnp.tile(cols["ik"], (1, 1, LANES // IDX_DIM))
    hw = MOBA_HEADS * HEAD_DIM
    w_all = jnp.concatenate([cols["cq"], cols["ckv"], kr_slot, cols["b"][:, :, :2 * hw], cols["c"][:, :, :2 * hw],
                             cols["iq"], ik_rep, cols["g"]], axis=2).astype(BF16)
    w_t = jnp.concatenate([cols["b"][:, :, 2 * hw:], cols["c"][:, :, 2 * hw:],
                           pad_last(cols["iw"], 0, LANES - IDX_HEADS)], axis=2)
    wuq = pad_last(w_uq.reshape(L, MLA_Q_RANK, MLA_HEADS, MLA_QK), 0, LANES - MLA_QK)
    wukv = w_ukv.reshape(L, MLA_KV_RANK, MLA_HEADS, MLA_NOPE + MLA_V)
    wk = pad_last(wukv[..., :MLA_NOPE], 0, LANES - MLA_NOPE)
    wv = wukv[..., MLA_NOPE:].reshape(L, MLA_KV_RANK, MLA_HEADS * MLA_V)
    row = lambda v: v.reshape(L, 1, -1)
    t = lambda v: jnp.swapaxes(v, 1, 2)
    return {
        "n1": row(norm1), "w_all": w_all, "w_t": t(w_t).astype(BF16), "g_cq": row(g_cq),
        "w_uq": wuq.reshape(L, MLA_Q_RANK, MLA_HEADS * LANES).astype(BF16), "g_ckv": row(g_ckv),
        "w_k": wk.reshape(L, MLA_KV_RANK, MLA_HEADS * LANES).astype(BF16), "w_v": t(wv).astype(BF16),
        "qn_a": row(pad_last(qn_mla * MLA_QK ** -0.5, 0, LANES - MLA_QK)),
        "kn_a": row(pad_last(kn_mla, 0, LANES - MLA_QK)),
        "qn_b": row(jnp.tile(qn_moba * HEAD_DIM ** -0.5, (1, MOBA_HEADS))), "kn_b": row(jnp.tile(kn_moba, (1, MOBA_HEADS))),
        "qn_c": row(jnp.tile(qn_dsa * HEAD_DIM ** -0.5, (1, DSA_HEADS))), "kn_c": row(jnp.tile(kn_dsa, (1, DSA_HEADS))),
        "w_br_a": w_br_a.astype(BF16), "w_br_b": w_br_b.astype(BF16), "w_br_c": w_br_c.astype(BF16),
        "w_out": w_out.astype(BF16), "n2": row(norm2), "w_gu": w_gu.astype(BF16), "w_d": w_down.astype(BF16),
    }


def _rope_tables(S):
    half = MLA_ROPE // 2
    freqs = ROPE_THETA ** (-jnp.arange(half, dtype=F32) / half)
    ang = jnp.arange(S, dtype=F32)[:, None] * freqs[None, :]
    cos, sin = jnp.cos(ang), jnp.sin(ang)
    zeros = lambda n: jnp.zeros((S, n), F32)
    tail = LANES - MLA_QK
    rc = jnp.concatenate([jnp.ones((S, MLA_NOPE), F32), cos, cos, jnp.ones((S, tail), F32)], axis=1)
    rs1 = jnp.concatenate([zeros(MLA_NOPE), -sin, zeros(half + tail)], axis=1)
    rs2 = jnp.concatenate([zeros(MLA_NOPE + half), sin, zeros(tail)], axis=1)
    return rc, rs1, rs2


def _tiles(S):
    dsa_q = min(256, S)
    return {"proj": min(256, S), "out": min(512, S), "mla_q": min(256, S), "dsa_q": dsa_q,
            "dsa_cls": min(4, S // dsa_q)}


def kernel(x, c, w_ada, b_ada, norm1, w_in, g_cq, w_uq, g_ckv, w_ukv, qn_mla, kn_mla, qn_moba, kn_moba, qn_dsa, kn_dsa, w_br_a, w_br_b, w_br_c, w_out, norm2, w_gu, w_down):
    B, S, D = x.shape
    L = w_ada.shape[0]
    assert D == D_MODEL and S % MOBA_BLOCK == 0
    n_slopes = MOBA_HEADS + DSA_HEADS
    slopes = [2.0 ** (-8.0 * (i + 1) / n_slopes) for i in range(n_slopes)]
    mod_all = _ada(c, w_ada.astype(BF16), b_ada).reshape(L, B, 6, D)
    rope_tabs = _rope_tables(S)
    seg = jnp.arange(MOBA_HEADS * HEAD_DIM) // HEAD_DIM
    bd = (seg[:, None] == seg[None, :]).astype(BF16)
    t = _tiles(S)
    lw = _pack_weights(norm1, w_in, g_cq, w_uq, g_ckv, w_ukv, qn_mla, kn_mla, qn_moba, kn_moba, qn_dsa, kn_dsa,
                       w_br_a, w_br_b, w_br_c, w_out, norm2, w_gu, w_down)
    for l in range(L):
        mod = mod_all[l]
        qa, ka, va, qb, kb, vb, qc, kc, vc, iq, ik, iw, gt = _inproj(x, mod, lw, l, rope_tabs, bd, t["proj"])
        ya = _mla_attention(qa, ka, va, tq=t["mla_q"])
        yb = _moba_attention(qb, kb, vb, tuple(slopes[0::2]))
        yc = _dsa_attention(qc, kc, vc, iq, ik, iw, tuple(slopes[1::2]), tq=t["dsa_q"], ncls=t["dsa_cls"])
        x = _merge(ya, yb, yc, gt, x, mod, lw, l, t["out"])
        x = _ffn(x, mod, lw, l, t["out"])
    return x
```

```python
import functools

import jax
import jax.numpy as jnp
from jax import lax
from jax.experimental import pallas as pl
from jax.experimental.pallas import tpu as pltpu

F32 = jnp.float32
BF16 = jnp.bfloat16

D_MODEL = 1024
HEAD_DIM = 64
RMS_EPS = 1e-6
MLA_HEADS = 8
MLA_NOPE = 64
MLA_ROPE = 32
MLA_QK = MLA_NOPE + MLA_ROPE
MLA_V = 64
MLA_Q_RANK = 768
MLA_KV_RANK = 256
ROPE_THETA = 10000.0
MOBA_HEADS = 4
MOBA_BLOCK = 256
MOBA_TOPK = 3
DSA_HEADS = 4
IDX_HEADS = 16
IDX_DIM = 32
DSA_TOPK = 256

LANES = 128
VMEM_LIMIT = 56 * 1024 * 1024

OFF_CQ = 0
OFF_CKV = OFF_CQ + MLA_Q_RANK
OFF_KR = OFF_CKV + MLA_KV_RANK
OFF_B = OFF_KR + LANES
OFF_C = OFF_B + 2 * MOBA_HEADS * HEAD_DIM
OFF_IQ = OFF_C + 2 * DSA_HEADS * HEAD_DIM
OFF_IK = OFF_IQ + IDX_HEADS * IDX_DIM
OFF_G = OFF_IK + LANES
N_ALL = OFF_G + 3 * D_MODEL
ROW_VB = 0
ROW_VC = ROW_VB + MOBA_HEADS * HEAD_DIM
ROW_IW = ROW_VC + DSA_HEADS * HEAD_DIM
N_T = ROW_IW + LANES


def _nt(a, b):
    return lax.dot_general(a, b, (((1,), (1,)), ((), ())), preferred_element_type=F32)


def _mm(a, b):
    return jnp.dot(a, b, preferred_element_type=F32)


def _split3(x):
    hi = x.astype(BF16)
    r = x - hi.astype(F32)
    mid = r.astype(BF16)
    lo = (r - mid.astype(F32)).astype(BF16)
    return hi, mid, lo


def _rms(x, g):
    return x * lax.rsqrt(jnp.mean(x * x, axis=-1, keepdims=True) + RMS_EPS) * g


def _const_spec(shape):
    nd = len(shape)
    return pl.BlockSpec(shape, lambda *_: (0,) * nd, pipeline_mode=pl.Buffered(1))


def _layer_spec(shape, l, part=0):
    rest = (0,) * (len(shape) - 2) + (part,)
    return pl.BlockSpec((1,) + tuple(shape[1:]), lambda *_: (l,) + rest, pipeline_mode=pl.Buffered(1))


def _params(*sem):
    return pltpu.CompilerParams(dimension_semantics=sem, vmem_limit_bytes=VMEM_LIMIT)


def _ada_kernel(c_ref, w_ref, b_ref, o_ref):
    c = c_ref[...]
    cond = (c * jax.nn.sigmoid(c)).astype(BF16)
    o_ref[0] = _mm(cond, w_ref[0].astype(BF16)) + b_ref[0]


def _ada(c, w_ada, b_ada):
    L, D, N = w_ada.shape
    B = c.shape[0]
    tn = 1024
    return pl.pallas_call(
        _ada_kernel,
        grid=(L, N // tn),
        in_specs=[
            pl.BlockSpec((B, D), lambda l, n: (0, 0)),
            pl.BlockSpec((1, D, tn), lambda l, n: (l, 0, n)),
            pl.BlockSpec((1, 1, tn), lambda l, n: (l, 0, n)),
        ],
        out_specs=pl.BlockSpec((1, B, tn), lambda l, n: (l, 0, n)),
        out_shape=jax.ShapeDtypeStruct((L, B, N), F32),
        compiler_params=_params("arbitrary", "arbitrary"),
        name="ada_mod",
    )(c, w_ada, b_ada.reshape(L, 1, N))


def _rope(x, c, s1, s2):
    return x * c + pltpu.roll(x, LANES - 16, 1) * s1 + pltpu.roll(x, 16, 1) * s2


def _inproj_kernel(x_ref, mod_ref, n1_ref, w_ref, wt_ref, gcq_ref, wuq_ref, gckv_ref, wk_ref, wv_ref,
                   qna_ref, kna_ref, rc_ref, rs1_ref, rs2_ref, bd_ref,
                   qnb_ref, knb_ref, qnc_ref, knc_ref,
                   qa_ref, ka_ref, va_ref, qb_ref, kb_ref, vb_ref, qc_ref, kc_ref, vc_ref,
                   iq_ref, ik_ref, iw_ref, gt_ref):
    x = x_ref[0]
    sh1 = mod_ref[0, 0:1, :]
    sc1 = mod_ref[0, 1:2, :]
    hb = (_rms(x, n1_ref[0]) * (1.0 + sc1) + sh1).astype(BF16)

    a = _mm(hb, w_ref[0, :, OFF_CQ:OFF_B])
    cqn = _rms(a[:, OFF_CQ:OFF_CKV], gcq_ref[0]).astype(BF16)
    ckvn = _rms(a[:, OFF_CKV:OFF_KR], gckv_ref[0]).astype(BF16)
    kr = a[:, OFF_KR:OFF_B]
    qraw = _mm(cqn, wuq_ref[0])
    kraw = _mm(ckvn, wk_ref[0])
    va_ref[0] = _nt(wv_ref[0], ckvn).astype(BF16)
    rc, rs1, rs2 = rc_ref[...], rs1_ref[...], rs2_ref[...]
    qna, kna = qna_ref[0], kna_ref[0]
    inv_qk = 1.0 / MLA_QK
    for h in range(MLA_HEADS):
        sl = slice(LANES * h, LANES * (h + 1))
        qh = qraw[:, sl]
        qh = qh * lax.rsqrt(jnp.sum(qh * qh, axis=-1, keepdims=True) * inv_qk + RMS_EPS) * qna
        qa_ref[0, :, sl] = _rope(qh, rc, rs1, rs2).astype(BF16)
        kh = kraw[:, sl] + kr
        kh = kh * lax.rsqrt(jnp.sum(kh * kh, axis=-1, keepdims=True) * inv_qk + RMS_EPS) * kna
        ka_ref[0, :, sl] = _rope(kh, rc, rs1, rs2).astype(BF16)

    bd = bd_ref[...]

    def segnorm(v, g):
        ss = sum(_mm(part, bd) for part in _split3(v * v)[:2])
        return v * lax.rsqrt(ss * (1.0 / HEAD_DIM) + RMS_EPS) * g

    hw = MOBA_HEADS * HEAD_DIM
    pb = _mm(hb, w_ref[0, :, OFF_B:OFF_C])
    qb_ref[0] = segnorm(pb[:, 0:hw], qnb_ref[0]).astype(BF16)
    kb_ref[0] = segnorm(pb[:, hw:2 * hw], knb_ref[0]).astype(BF16)
    pc = _mm(hb, w_ref[0, :, OFF_C:OFF_IQ])
    qc_ref[0] = segnorm(pc[:, 0:hw], qnc_ref[0]).astype(BF16)
    kc_ref[0] = segnorm(pc[:, hw:2 * hw], knc_ref[0]).astype(BF16)

    pt = _nt(wt_ref[0], hb)
    vb_ref[0] = pt[ROW_VB:ROW_VC, :].astype(BF16)
    vc_ref[0] = pt[ROW_VC:ROW_IW, :].astype(BF16)
    iw_ref[0] = pt[ROW_IW:N_T, :]

    pi = _mm(hb, w_ref[0, :, OFF_IQ:OFF_G])
    iq_ref[0] = pi[:, 0:OFF_IK - OFF_IQ].astype(BF16)
    ik_ref[0] = pi[:, OFF_IK - OFF_IQ:OFF_G - OFF_IQ].astype(BF16)
    gt_ref[0] = jax.nn.sigmoid(_mm(hb, w_ref[0, :, OFF_G:N_ALL])).astype(BF16)


def _inproj(x, mod, lw, l, rope_tabs, bd, tm):
    B, S, D = x.shape
    hw = MOBA_HEADS * HEAD_DIM
    row = lambda w: pl.BlockSpec((1, tm, w), lambda b, i: (b, i, 0))
    tab = pl.BlockSpec((tm, LANES), lambda b, i: (i, 0))
    consts = [lw["n1"], lw["w_all"], lw["w_t"], lw["g_cq"], lw["w_uq"], lw["g_ckv"], lw["w_k"], lw["w_v"],
              lw["qn_a"], lw["kn_a"]]
    consts2 = [lw["qn_b"], lw["kn_b"], lw["qn_c"], lw["kn_c"]]
    widths = [MLA_HEADS * LANES, MLA_HEADS * LANES, MLA_HEADS * MLA_V, hw, hw, hw, hw, hw, hw,
              IDX_HEADS * IDX_DIM, LANES, LANES, 3 * D_MODEL]
    dtypes = [BF16] * 11 + [F32, BF16]
    transposed = [False, False, True] * 3 + [False, False, True, False]
    col = lambda w: pl.BlockSpec((1, w, tm), lambda b, i: (b, 0, i))
    return pl.pallas_call(
        _inproj_kernel,
        grid=(B, S // tm),
        in_specs=[row(D), pl.BlockSpec((1, 6, D), lambda b, i: (b, 0, 0))]
        + [_layer_spec(c.shape, l) for c in consts] + [tab, tab, tab, _const_spec(bd.shape)]
        + [_layer_spec(c.shape, l) for c in consts2],
        out_specs=[col(w) if tr else row(w) for w, tr in zip(widths, transposed)],
        out_shape=[jax.ShapeDtypeStruct((B, w, S) if tr else (B, S, w), dt)
                   for w, dt, tr in zip(widths, dtypes, transposed)],
        compiler_params=_params("arbitrary", "arbitrary"),
        name="in_proj",
    )(x, mod, *consts, *rope_tabs, bd, *consts2)


NEG_INF = float("-inf")


def _fold_rows(x, op, n=8):
    out = x[0:n, :]
    for t in range(1, x.shape[0] // n):
        out = op(out, x[n * t:n * (t + 1), :])
    return out


def _interleave(units):
    pending = []
    for make in units:
        first, second = make()
        for t in range(max(len(first), len(pending))):
            if t < len(first):
                first[t]()
            if t < len(pending):
                pending[t]()
        pending = second
    for step in pending:
        step()


class _Unit:
    def __init__(self, logits, n_chunks, ch, tq, score, vt_chunk, done):
        self.m8 = jnp.full((8, tq), NEG_INF, F32)
        self.l8 = None
        self.acc = None

        def first(j):
            s = score(j)
            logits[j * ch:(j + 1) * ch, :] = s
            self.m8 = jnp.maximum(self.m8, _fold_rows(s, jnp.maximum))

        def second(j):
            if j == 0:
                self.m = jnp.max(self.m8, axis=0, keepdims=True)
            p = jnp.exp(logits[j * ch:(j + 1) * ch, :] - self.m)
            f = _fold_rows(p, jnp.add)
            pv = _mm(vt_chunk(j), p.astype(BF16))
            self.l8 = f if self.l8 is None else self.l8 + f
            self.acc = pv if self.acc is None else self.acc + pv
            if j == n_chunks - 1:
                done(self.acc / jnp.sum(self.l8, axis=0, keepdims=True))

        self.steps = ([functools.partial(first, j) for j in range(n_chunks)],
                      [functools.partial(second, j) for j in range(n_chunks)])


def _mla_kernel(q_ref, k_ref, vt_ref, o_ref, s_ref, *, tq, nq):
    krow = lax.broadcasted_iota(jnp.int32, (tq, tq), 0)
    qcol = lax.broadcasted_iota(jnp.int32, (tq, tq), 1)
    causal = krow <= qcol
    outs = {}

    def unit(c, hh, u):
        sl = slice(LANES * hh, LANES * (hh + 1))
        q = q_ref[0, c * tq:(c + 1) * tq, sl]

        def score(j):
            s = _nt(k_ref[0, j * tq:(j + 1) * tq, sl], q)
            return jnp.where(causal, s, NEG_INF) if j == c else s

        def done(out):
            outs[c, hh] = out
            if hh == 1:
                pair = jnp.concatenate([outs[c, 0], outs[c, 1]], axis=0)
                o_ref[0, c * tq:(c + 1) * tq, :] = pair.T.astype(BF16)

        vt = lambda j: vt_ref[0, MLA_V * hh:MLA_V * (hh + 1), j * tq:(j + 1) * tq]
        return _Unit(s_ref.at[u % 3], c + 1, tq, tq, score, vt, done).steps

    _interleave([functools.partial(unit, c, hh, 2 * c + hh) for c in range(nq) for hh in range(2)])


def _mla_attention(qa, ka, vat, tq):
    B, S, _ = qa.shape
    pairs = MLA_HEADS // 2
    nq = S // tq
    wide = pl.BlockSpec((1, S, 2 * LANES), lambda b, h: (b, 0, h))
    return pl.pallas_call(
        functools.partial(_mla_kernel, tq=tq, nq=nq),
        grid=(B, pairs),
        in_specs=[wide, wide, pl.BlockSpec((1, 2 * MLA_V, S), lambda b, h: (b, h, 0))],
        out_specs=pl.BlockSpec((1, S, 2 * MLA_V), lambda b, h: (b, 0, h)),
        out_shape=jax.ShapeDtypeStruct((B, S, MLA_HEADS * MLA_V), BF16),
        scratch_shapes=[pltpu.VMEM((3, S, tq), F32)],
        compiler_params=_params("arbitrary", "arbitrary"),
        name="mla_attn",
    )(qa, ka, vat)


def _moba_kernel(q_ref, k_ref, vt_ref, o_ref, s_ref, *, nb, nbp, n_sel, slopes):
    blk = MOBA_BLOCK
    hw = MOBA_HEADS * HEAD_DIM
    lane = lax.broadcasted_iota(jnp.int32, (1, hw), 1)

    means = [jnp.mean(k_ref[0, n * blk:(n + 1) * blk, :].astype(F32), axis=0, keepdims=True)
             for n in range(nb)]
    km = jnp.concatenate(means + [jnp.zeros((1, hw), F32)] * (nbp - nb), axis=0)
    pieces = []
    for h in range(MOBA_HEADS):
        hm = (lane >= HEAD_DIM * h) & (lane < HEAD_DIM * (h + 1))
        pieces += [p.astype(F32) for p in _split3(jnp.where(hm, km, 0.0))]
    km_stack = jnp.concatenate(pieces, axis=0).astype(BF16)

    krow = lax.broadcasted_iota(jnp.int32, (blk, blk), 0)
    qcol = lax.broadcasted_iota(jnp.int32, (blk, blk), 1)
    causal = krow <= qcol
    krow_f = krow.astype(F32)
    bidx = lax.broadcasted_iota(jnp.int32, (nbp, blk), 0)

    gates = {}
    outs = {}

    def unit(c, h, u):
        q = q_ref[0, c * blk:(c + 1) * blk, :]
        hm = (lane >= HEAD_DIM * h) & (lane < HEAD_DIM * (h + 1))
        qh = jnp.where(hm, q, jnp.zeros_like(q))
        kbias = slopes[h] * krow_f
        if c > 0:
            if h == 0:
                gates[c] = _nt(km_stack, q)
            past = bidx < c
            g = sum(gates[c][(3 * h + p) * nbp:(3 * h + p + 1) * nbp, :] for p in range(3))
            g = jnp.where(past, g, NEG_INF)
            rank = jnp.zeros((nbp, blk), F32)
            for n2 in range(c):
                gm = g[n2:n2 + 1, :]
                beats = (gm > g) | ((gm == g) & (bidx > n2))
                rank = rank + jnp.where(beats, 1.0, 0.0)
            drop = jnp.where(past & (rank < n_sel), 0.0, NEG_INF)

        def score(j):
            s = _nt(k_ref[0, j * blk:(j + 1) * blk, :], qh) + kbias
            if j == c:
                return jnp.where(causal, s, NEG_INF)
            return s + (drop[j:j + 1, :] + slopes[h] * float((j - c) * blk))

        def done(out):
            outs[c, h] = out
            if h == MOBA_HEADS - 1:
                full = jnp.concatenate([outs[c, hh] for hh in range(MOBA_HEADS)], axis=0)
                o_ref[0, c * blk:(c + 1) * blk, :] = full.T.astype(BF16)

        vt = lambda j: vt_ref[0, HEAD_DIM * h:HEAD_DIM * (h + 1), j * blk:(j + 1) * blk]
        return _Unit(s_ref.at[u % 3], c + 1, blk, blk, score, vt, done).steps

    _interleave([functools.partial(unit, c, h, MOBA_HEADS * c + h) for c in range(nb) for h in range(MOBA_HEADS)])


def _moba_attention(qb, kb, vbt, slopes):
    B, S, hw = qb.shape
    blk = MOBA_BLOCK
    nb = S // blk
    nbp = -(-nb // 8) * 8
    n_sel = max(1, min(MOBA_TOPK, nb - 1))
    full = pl.BlockSpec((1, S, hw), lambda b: (b, 0, 0))
    return pl.pallas_call(
        functools.partial(_moba_kernel, nb=nb, nbp=nbp, n_sel=n_sel, slopes=slopes),
        grid=(B,),
        in_specs=[full, full, pl.BlockSpec((1, hw, S), lambda b: (b, 0, 0))],
        out_specs=full,
        out_shape=jax.ShapeDtypeStruct((B, S, hw), BF16),
        scratch_shapes=[pltpu.VMEM((3, S, blk), F32)],
        compiler_params=_params("arbitrary"),
        name="moba_attn",
    )(qb, kb, vbt)


INT_MIN = -2 ** 31
KEY_POS_INF = 0x7F800000
KEY_NEG_INF = -0x7F800001


def _key_to_float(key):
    key = jnp.clip(key, KEY_NEG_INF, KEY_POS_INF)
    return lax.bitcast_convert_type(jnp.where(key >= 0, key, key ^ jnp.int32(0x7FFFFFFF)), F32)


def _kth_largest(sc_ref, rows, n_chunks, tq, k):
    def count_ge(c):
        acc = None
        for j in range(n_chunks):
            f = _fold_rows(jnp.where(sc_ref[rows(j), :] >= c, 1.0, 0.0), jnp.add)
            acc = f if acc is None else acc + f
        return jnp.sum(acc, axis=0, keepdims=True)

    zero = jnp.zeros((1, tq), jnp.int32)
    key = jnp.where(count_ge(_key_to_float(zero)) >= k, zero, jnp.int32(INT_MIN))

    def step(it, key):
        cand = key | lax.shift_left(jnp.int32(1), jnp.int32(30) - it)
        return jnp.where(count_ge(_key_to_float(cand)) >= k, cand, key)

    return _key_to_float(lax.fori_loop(0, 31, step, key))


def _dsa_kernel(q_ref, k_ref, vt_ref, iq_ref, ik_ref, iwt_ref, kpos_ref, o_ref, sc_ref, mb_ref, s_ref, *,
                tq, ch, per, ncls, n_keep, slopes, idx_scale):
    i = pl.program_id(1)
    hw = DSA_HEADS * HEAD_DIM
    q0 = i * tq
    qpos = q0 + lax.broadcasted_iota(jnp.int32, (1, tq), 1)
    krow = lax.broadcasted_iota(jnp.int32, (ch, 1), 0)
    lane = lax.broadcasted_iota(jnp.int32, (1, LANES), 1)
    lane_h = lax.broadcasted_iota(jnp.int32, (1, hw), 1)
    grp = LANES // IDX_DIM

    def body(c):
        n_chunks = (c + 1) * per * tq // ch
        rows = lambda j: slice(j * ch, (j + 1) * ch)
        iwt = iwt_ref[0]
        qms = []
        for hh in range(IDX_HEADS):
            g, r = divmod(hh, grp)
            iqg = iq_ref[0, :, LANES * g:LANES * (g + 1)]
            qms.append(jnp.where((lane >= IDX_DIM * r) & (lane < IDX_DIM * (r + 1)), iqg, jnp.zeros_like(iqg)))
        for j in range(n_chunks):
            ikc = ik_ref[0, rows(j), :]
            score = None
            for hh in range(IDX_HEADS):
                term = jnp.maximum(_nt(ikc, qms[hh]), 0.0) * iwt[hh:hh + 1, :]
                score = term if score is None else score + term
            sc_ref[rows(j), :] = jnp.where(krow + j * ch <= qpos, score * idx_scale, NEG_INF)

        def mask_at_least(thr):
            for j in range(n_chunks):
                keep = (sc_ref[rows(j), :] >= thr) & (krow + j * ch <= qpos)
                mb_ref[rows(j), :] = jnp.where(keep, 0.0, NEG_INF)

        def count(pred):
            acc = None
            for j in range(n_chunks):
                f = _fold_rows(jnp.where(pred(sc_ref[rows(j), :]), 1.0, 0.0), jnp.add)
                acc = f if acc is None else acc + f
            return jnp.sum(acc, axis=0, keepdims=True)

        if n_chunks * ch <= n_keep:
            mask_at_least(jnp.full((1, tq), NEG_INF, F32))
        else:
            thr = _kth_largest(sc_ref, rows, n_chunks, tq, float(n_keep))
            tied = jnp.max(count(lambda v: v >= thr)) > float(n_keep)
            pl.when(jnp.logical_not(tied))(functools.partial(mask_at_least, thr))

            @pl.when(tied)
            def _():
                need = float(n_keep) - count(lambda v: v > thr)
                tri = jnp.where(lax.broadcasted_iota(jnp.int32, (ch, ch), 1)
                                <= lax.broadcasted_iota(jnp.int32, (ch, ch), 0), 1.0, 0.0).astype(BF16)

                def chunk(j, seen):
                    at = pl.ds(pl.multiple_of(j * ch, ch), ch)
                    sc = sc_ref[at, :]
                    eq = jnp.where(sc == thr, 1.0, 0.0)
                    rank = _mm(tri, eq.astype(BF16)) + seen
                    keep = (sc > thr) | ((sc == thr) & (rank <= need))
                    keep = keep & (krow + j * ch <= qpos)
                    mb_ref[at, :] = jnp.where(keep, 0.0, NEG_INF)
                    return seen + jnp.sum(eq, axis=0, keepdims=True)

                lax.fori_loop(0, n_chunks, chunk, jnp.zeros((1, tq), F32))

        q = q_ref[0]
        q0f = q0.astype(F32)
        outs = {}

        def unit(h):
            hm = (lane_h >= HEAD_DIM * h) & (lane_h < HEAD_DIM * (h + 1))
            qh = jnp.where(hm, q, jnp.zeros_like(q))

            def score(j):
                kb = slopes[h] * (kpos_ref[rows(j), :] - q0f)
                kb = jnp.concatenate([kb] * (tq // LANES), axis=1)
                return _nt(k_ref[0, rows(j), :], qh) + kb + mb_ref[rows(j), :]

            def done(out):
                outs[h] = out
                if h == DSA_HEADS - 1:
                    full = jnp.concatenate([outs[hh] for hh in range(DSA_HEADS)], axis=0)
                    o_ref[0] = full.T.astype(BF16)

            vt = lambda j: vt_ref[0, HEAD_DIM * h:HEAD_DIM * (h + 1), rows(j)]
            return _Unit(s_ref.at[h], n_chunks, ch, tq, score, vt, done).steps

        _interleave([functools.partial(unit, h) for h in range(DSA_HEADS)])

    for c in range(ncls):
        pl.when((i >= c * per) & (i < (c + 1) * per))(functools.partial(body, c))


def _dsa_attention(qc, kc, vct, iq, ik, iwt, slopes, tq, ncls):
    B, S, hw = qc.shape
    nq = S // tq
    per = nq // ncls
    ch = tq
    n_keep = min(DSA_TOPK, S // 4)
    kpos = jnp.broadcast_to(jnp.arange(S, dtype=F32)[:, None], (S, LANES))
    full = lambda w: pl.BlockSpec((1, S, w), lambda b, i: (b, 0, 0))
    tile = lambda w: pl.BlockSpec((1, tq, w), lambda b, i: (b, i, 0))
    return pl.pallas_call(
        functools.partial(_dsa_kernel, tq=tq, ch=ch, per=per, ncls=ncls, n_keep=n_keep, slopes=slopes,
                          idx_scale=(IDX_DIM * IDX_HEADS) ** -0.5),
        grid=(B, nq),
        in_specs=[tile(hw), full(hw), pl.BlockSpec((1, hw, S), lambda b, i: (b, 0, 0)),
                  tile(IDX_HEADS * IDX_DIM), full(LANES), pl.BlockSpec((1, LANES, tq), lambda b, i: (b, 0, i)),
                  pl.BlockSpec((S, LANES), lambda b, i: (0, 0))],
        out_specs=tile(hw),
        out_shape=jax.ShapeDtypeStruct((B, S, hw), BF16),
        scratch_shapes=[pltpu.VMEM((S, tq), F32), pltpu.VMEM((S, tq), F32),
                        pltpu.VMEM((DSA_HEADS, S, tq), F32)],
        compiler_params=_params("arbitrary", "arbitrary"),
        name="dsa_attn",
    )(qc, kc, vct, iq, ik, iwt, kpos)


def _merge_kernel(ya_ref, yb_ref, yc_ref, gt_ref, x_ref, mod_ref, wa_ref, wb_ref, wc_ref, wo_ref, o_ref):
    d = D_MODEL
    merged = gt_ref[0, :, 0:d].astype(F32) * _mm(ya_ref[0], wa_ref[0])
    merged = merged + gt_ref[0, :, d:2 * d].astype(F32) * _mm(yb_ref[0], wb_ref[0])
    merged = merged + gt_ref[0, :, 2 * d:3 * d].astype(F32) * _mm(yc_ref[0], wc_ref[0])
    o_ref[0] = x_ref[0] + mod_ref[0, 2:3, :] * _mm(merged.astype(BF16), wo_ref[0])


def _merge(ya, yb, yc, gt, x, mod, lw, l, tm):
    B, S, D = x.shape
    row = lambda w: pl.BlockSpec((1, tm, w), lambda b, i: (b, i, 0))
    consts = [lw["w_br_a"], lw["w_br_b"], lw["w_br_c"], lw["w_out"]]
    return pl.pallas_call(
        _merge_kernel,
        grid=(B, S // tm),
        in_specs=[row(ya.shape[-1]), row(yb.shape[-1]), row(yc.shape[-1]), row(3 * D), row(D),
                  pl.BlockSpec((1, 6, D), lambda b, i: (b, 0, 0))] + [_layer_spec(c.shape, l) for c in consts],
        out_specs=row(D),
        out_shape=jax.ShapeDtypeStruct((B, S, D), F32),
        compiler_params=_params("arbitrary", "arbitrary"),
        name="merge_out",
    )(ya, yb, yc, gt, x, mod, *consts)


def _ffn_kernel(x_ref, mod_ref, n2_ref, wg_ref, wu_ref, wd_ref, o_ref):
    x = x_ref[0]
    h = (_rms(x, n2_ref[0]) * (1.0 + mod_ref[0, 4:5, :]) + mod_ref[0, 3:4, :]).astype(BF16)
    g = _mm(h, wg_ref[0])
    u = _mm(h, wu_ref[0])
    act = (g * jax.nn.sigmoid(g) * u).astype(BF16)
    o_ref[0] = x + mod_ref[0, 5:6, :] * _mm(act, wd_ref[0])


def _ffn(x, mod, lw, l, tm):
    B, S, D = x.shape
    row = pl.BlockSpec((1, tm, D), lambda b, i: (b, i, 0))
    gu = lw["w_gu"]
    half = (gu.shape[0], gu.shape[1], gu.shape[2] // 2)
    consts = [lw["n2"], gu, gu, lw["w_d"]]
    return pl.pallas_call(
        _ffn_kernel,
        grid=(B, S // tm),
        in_specs=[row, pl.BlockSpec((1, 6, D), lambda b, i: (b, 0, 0)), _layer_spec(lw["n2"].shape, l),
                  _layer_spec(half, l, 0), _layer_spec(half, l, 1), _layer_spec(lw["w_d"].shape, l)],
        out_specs=row,
        out_shape=jax.ShapeDtypeStruct((B, S, D), F32),
        compiler_params=_params("arbitrary", "arbitrary"),
        name="swiglu",
    )(x, mod, *consts)


def _pack_weights(norm1, w_in, g_cq, w_uq, g_ckv, w_ukv, qn_mla, kn_mla, qn_moba, kn_moba, qn_dsa, kn_dsa,
                  w_br_a, w_br_b, w_br_c, w_out, norm2, w_gu, w_down):
    L = w_in.shape[0]
    o = 0
    cols = {}
    for name, wd in (("cq", MLA_Q_RANK), ("ckv", MLA_KV_RANK), ("kr", MLA_ROPE),
                     ("b", 3 * MOBA_HEADS * HEAD_DIM), ("c", 3 * DSA_HEADS * HEAD_DIM),
                     ("iq", IDX_HEADS * IDX_DIM), ("ik", IDX_DIM), ("iw", IDX_HEADS), ("g", 3 * D_MODEL)):
        cols[name] = w_in[:, :, o:o + wd]
        o += wd
    pad_last = lambda v, lo, hi: jnp.pad(v, ((0, 0),) * (v.ndim - 1) + ((lo, hi),))
    kr_slot = pad_last(cols["kr"], MLA_NOPE, LANES - MLA_QK)
    ik_rep = jnp.tile(cols["ik"], (1, 1, LANES // IDX_DIM))
    hw = MOBA_HEADS * HEAD_DIM
    w_all = jnp.concatenate([cols["cq"], cols["ckv"], kr_slot, cols["b"][:, :, :2 * hw], cols["c"][:, :, :2 * hw],
                             cols["iq"], ik_rep, cols["g"]], axis=2).astype(BF16)
    w_t = jnp.concatenate([cols["b"][:, :, 2 * hw:], cols["c"][:, :, 2 * hw:],
                           pad_last(cols["iw"], 0, LANES - IDX_HEADS)], axis=2)
    wuq = pad_last(w_uq.reshape(L, MLA_Q_RANK, MLA_HEADS, MLA_QK), 0, LANES - MLA_QK)
    wukv = w_ukv.reshape(L, MLA_KV_RANK, MLA_HEADS, MLA_NOPE + MLA_V)
    wk = pad_last(wukv[..., :MLA_NOPE], 0, LANES - MLA_NOPE)
    wv = wukv[..., MLA_NOPE:].reshape(L, MLA_KV_RANK, MLA_HEADS * MLA_V)
    row = lambda v: v.reshape(L, 1, -1)
    t = lambda v: jnp.swapaxes(v, 1, 2)
    return {
        "n1": row(norm1), "w_all": w_all, "w_t": t(w_t).astype(BF16), "g_cq": row(g_cq),
        "w_uq": wuq.reshape(L, MLA_Q_RANK, MLA_HEADS * LANES).astype(BF16), "g_ckv": row(g_ckv),
        "w_k": wk.reshape(L, MLA_KV_RANK, MLA_HEADS * LANES).astype(BF16), "w_v": t(wv).astype(BF16),
        "qn_a": row(pad_last(qn_mla * MLA_QK ** -0.5, 0, LANES - MLA_QK)),
        "kn_a": row(pad_last(kn_mla, 0, LANES - MLA_QK)),
        "qn_b": row(jnp.tile(qn_moba * HEAD_DIM ** -0.5, (1, MOBA_HEADS))), "kn_b": row(jnp.tile(kn_moba, (1, MOBA_HEADS))),
        "qn_c": row(jnp.tile(qn_dsa * HEAD_DIM ** -0.5, (1, DSA_HEADS))), "kn_c": row(jnp.tile(kn_dsa, (1, DSA_HEADS))),
        "w_br_a": w_br_a.astype(BF16), "w_br_b": w_br_b.astype(BF16), "w_br_c": w_br_c.astype(BF16),
        "w_out": w_out.astype(BF16), "n2": row(norm2), "w_gu": w_gu.astype(BF16), "w_d": w_down.astype(BF16),
    }


def _rope_tables(S):
    half = MLA_ROPE // 2
    freqs = ROPE_THETA ** (-jnp.arange(half, dtype=F32) / half)
    ang = jnp.arange(S, dtype=F32)[:, None] * freqs[None, :]
    cos, sin = jnp.cos(ang), jnp.sin(ang)
    zeros = lambda n: jnp.zeros((S, n), F32)
    tail = LANES - MLA_QK
    rc = jnp.concatenate([jnp.ones((S, MLA_NOPE), F32), cos, cos, jnp.ones((S, tail), F32)], axis=1)
    rs1 = jnp.concatenate([zeros(MLA_NOPE), -sin, zeros(half + tail)], axis=1)
    rs2 = jnp.concatenate([zeros(MLA_NOPE + half), sin, zeros(tail)], axis=1)
    return rc, rs1, rs2


def _tiles(S):
    dsa_q = min(256, S)
    return {"proj": min(256, S), "out": min(512, S), "mla_q": min(256, S), "dsa_q": dsa_q,
            "dsa_cls": min(4, S // dsa_q)}


def kernel(x, c, w_ada, b_ada, norm1, w_in, g_cq, w_uq, g_ckv, w_ukv, qn_mla, kn_mla, qn_moba, kn_moba, qn_dsa, kn_dsa, w_br_a, w_br_b, w_br_c, w_out, norm2, w_gu, w_down):
    B, S, D = x.shape
    L = w_ada.shape[0]
    t = _tiles(S)
    assert D == D_MODEL and S % MOBA_BLOCK == 0 and all(S % t[k] == 0 for k in ("proj", "out", "mla_q", "dsa_q"))
    n_slopes = MOBA_HEADS + DSA_HEADS
    slopes = [2.0 ** (-8.0 * (i + 1) / n_slopes) for i in range(n_slopes)]
    mod_all = _ada(c, w_ada, b_ada).reshape(L, B, 6, D)
    rope_tabs = _rope_tables(S)
    seg = jnp.arange(MOBA_HEADS * HEAD_DIM) // HEAD_DIM
    bd = (seg[:, None] == seg[None, :]).astype(BF16)
    lw = _pack_weights(norm1, w_in, g_cq, w_uq, g_ckv, w_ukv, qn_mla, kn_mla, qn_moba, kn_moba, qn_dsa, kn_dsa,
                       w_br_a, w_br_b, w_br_c, w_out, norm2, w_gu, w_down)
    for l in range(L):
        mod = mod_all[l]
        qa, ka, va, qb, kb, vb, qc, kc, vc, iq, ik, iw, gt = _inproj(x, mod, lw, l, rope_tabs, bd, t["proj"])
        ya = _mla_attention(qa, ka, va, tq=t["mla_q"])
        yb = _moba_attention(qb, kb, vb, tuple(slopes[0::2]))
        yc = _dsa_attention(qc, kc, vc, iq, ik, iw, tuple(slopes[1::2]), tq=t["dsa_q"], ncls=t["dsa_cls"])
        x = _merge(ya, yb, yc, gt, x, mod, lw, l, t["out"])
        x = _ffn(x, mod, lw, l, t["out"])
    return x
```

```python
import functools

import jax
import jax.numpy as jnp
from jax import lax
from jax.experimental import pallas as pl
from jax.experimental.pallas import tpu as pltpu

F32 = jnp.float32
BF16 = jnp.bfloat16

D_MODEL = 1024
HEAD_DIM = 64
RMS_EPS = 1e-6
MLA_HEADS = 8
MLA_NOPE = 64
MLA_ROPE = 32
MLA_QK = MLA_NOPE + MLA_ROPE
MLA_V = 64
MLA_Q_RANK = 768
MLA_KV_RANK = 256
ROPE_THETA = 10000.0
MOBA_HEADS = 4
MOBA_BLOCK = 256
MOBA_TOPK = 3
DSA_HEADS = 4
IDX_HEADS = 16
IDX_DIM = 32
DSA_TOPK = 256

LANES = 128
VMEM_LIMIT = 56 * 1024 * 1024

OFF_CQ = 0
OFF_CKV = OFF_CQ + MLA_Q_RANK
OFF_KR = OFF_CKV + MLA_KV_RANK
OFF_B = OFF_KR + LANES
OFF_C = OFF_B + 2 * MOBA_HEADS * HEAD_DIM
OFF_IQ = OFF_C + 2 * DSA_HEADS * HEAD_DIM
OFF_IK = OFF_IQ + IDX_HEADS * IDX_DIM
OFF_G = OFF_IK + LANES
N_ALL = OFF_G + 3 * D_MODEL
ROW_VB = 0
ROW_VC = ROW_VB + MOBA_HEADS * HEAD_DIM
ROW_IW = ROW_VC + DSA_HEADS * HEAD_DIM
N_T = ROW_IW + LANES


def _nt(a, b):
    return lax.dot_general(a, b, (((1,), (1,)), ((), ())), preferred_element_type=F32)


def _mm(a, b):
    return jnp.dot(a, b, preferred_element_type=F32)


def _split3(x):
    hi = x.astype(BF16)
    r = x - hi.astype(F32)
    mid = r.astype(BF16)
    lo = (r - mid.astype(F32)).astype(BF16)
    return hi, mid, lo


def _rms(x, g):
    return x * lax.rsqrt(jnp.mean(x * x, axis=-1, keepdims=True) + RMS_EPS) * g


def _const_spec(shape):
    nd = len(shape)
    return pl.BlockSpec(shape, lambda *_: (0,) * nd, pipeline_mode=pl.Buffered(1))


def _layer_spec(shape, l, part=0):
    rest = (0,) * (len(shape) - 2) + (part,)
    return pl.BlockSpec((1,) + tuple(shape[1:]), lambda *_: (l,) + rest, pipeline_mode=pl.Buffered(1))


def _params(*sem):
    return pltpu.CompilerParams(dimension_semantics=sem, vmem_limit_bytes=VMEM_LIMIT)


def _ada_kernel(c_ref, w_ref, b_ref, o_ref):
    c = c_ref[...]
    cond = (c * jax.nn.sigmoid(c)).astype(BF16)
    o_ref[0] = _mm(cond, w_ref[0].astype(BF16)) + b_ref[0]


def _ada(c, w_ada, b_ada):
    L, D, N = w_ada.shape
    B = c.shape[0]
    tn = 1024
    return pl.pallas_call(
        _ada_kernel,
        grid=(L, N // tn),
        in_specs=[
            pl.BlockSpec((B, D), lambda l, n: (0, 0)),
            pl.BlockSpec((1, D, tn), lambda l, n: (l, 0, n)),
            pl.BlockSpec((1, 1, tn), lambda l, n: (l, 0, n)),
        ],
        out_specs=pl.BlockSpec((1, B, tn), lambda l, n: (l, 0, n)),
        out_shape=jax.ShapeDtypeStruct((L, B, N), F32),
        compiler_params=_params("arbitrary", "arbitrary"),
        name="ada_mod",
    )(c, w_ada, b_ada.reshape(L, 1, N))


def _rope(x, c, s1, s2):
    return x * c + pltpu.roll(x, LANES - 16, 1) * s1 + pltpu.roll(x, 16, 1) * s2


def _inproj_kernel(x_ref, mod_ref, n1_ref, w_ref, wt_ref, gcq_ref, wuq_ref, gckv_ref, wk_ref, wv_ref,
                   qna_ref, kna_ref, rc_ref, rs1_ref, rs2_ref, bd_ref,
                   qnb_ref, knb_ref, qnc_ref, knc_ref,
                   qa_ref, ka_ref, va_ref, qb_ref, kb_ref, vb_ref, qc_ref, kc_ref, vc_ref,
                   iq_ref, ik_ref, iw_ref, gt_ref):
    x = x_ref[0]
    sh1 = mod_ref[0, 0:1, :]
    sc1 = mod_ref[0, 1:2, :]
    hb = (_rms(x, n1_ref[0]) * (1.0 + sc1) + sh1).astype(BF16)

    a = _mm(hb, w_ref[0, :, OFF_CQ:OFF_B])
    cqn = _rms(a[:, OFF_CQ:OFF_CKV], gcq_ref[0]).astype(BF16)
    ckvn = _rms(a[:, OFF_CKV:OFF_KR], gckv_ref[0]).astype(BF16)
    kr = a[:, OFF_KR:OFF_B]
    qraw = _mm(cqn, wuq_ref[0])
    kraw = _mm(ckvn, wk_ref[0])
    va_ref[0] = _nt(wv_ref[0], ckvn).astype(BF16)
    rc, rs1, rs2 = rc_ref[...], rs1_ref[...], rs2_ref[...]
    qna, kna = qna_ref[0], kna_ref[0]
    inv_qk = 1.0 / MLA_QK
    for h in range(MLA_HEADS):
        sl = slice(LANES * h, LANES * (h + 1))
        qh = qraw[:, sl]
        qh = qh * lax.rsqrt(jnp.sum(qh * qh, axis=-1, keepdims=True) * inv_qk + RMS_EPS) * qna
        qa_ref[0, :, sl] = _rope(qh, rc, rs1, rs2).astype(BF16)
        kh = kraw[:, sl] + kr
        kh = kh * lax.rsqrt(jnp.sum(kh * kh, axis=-1, keepdims=True) * inv_qk + RMS_EPS) * kna
        ka_ref[0, :, sl] = _rope(kh, rc, rs1, rs2).astype(BF16)

    bd = bd_ref[...]

    def segnorm(v, g):
        ss = sum(_mm(part, bd) for part in _split3(v * v)[:2])
        return v * lax.rsqrt(ss * (1.0 / HEAD_DIM) + RMS_EPS) * g

    hw = MOBA_HEADS * HEAD_DIM
    pb = _mm(hb, w_ref[0, :, OFF_B:OFF_C])
    qb_ref[0] = segnorm(pb[:, 0:hw], qnb_ref[0]).astype(BF16)
    kb_ref[0] = segnorm(pb[:, hw:2 * hw], knb_ref[0]).astype(BF16)
    pc = _mm(hb, w_ref[0, :, OFF_C:OFF_IQ])
    qc_ref[0] = segnorm(pc[:, 0:hw], qnc_ref[0]).astype(BF16)
    kc_ref[0] = segnorm(pc[:, hw:2 * hw], knc_ref[0]).astype(BF16)

    pt = _nt(wt_ref[0], hb)
    vb_ref[0] = pt[ROW_VB:ROW_VC, :].astype(BF16)
    vc_ref[0] = pt[ROW_VC:ROW_IW, :].astype(BF16)
    iw_ref[0] = pt[ROW_IW:N_T, :]

    pi = _mm(hb, w_ref[0, :, OFF_IQ:OFF_G])
    iq_ref[0] = pi[:, 0:OFF_IK - OFF_IQ].astype(BF16)
    ik_ref[0] = pi[:, OFF_IK - OFF_IQ:OFF_G - OFF_IQ].astype(BF16)
    gt_ref[0] = jax.nn.sigmoid(_mm(hb, w_ref[0, :, OFF_G:N_ALL])).astype(BF16)


def _inproj(x, mod, lw, l, rope_tabs, bd, tm):
    B, S, D = x.shape
    hw = MOBA_HEADS * HEAD_DIM
    row = lambda w: pl.BlockSpec((1, tm, w), lambda b, i: (b, i, 0))
    tab = pl.BlockSpec((tm, LANES), lambda b, i: (i, 0))
    consts = [lw["n1"], lw["w_all"], lw["w_t"], lw["g_cq"], lw["w_uq"], lw["g_ckv"], lw["w_k"], lw["w_v"],
              lw["qn_a"], lw["kn_a"]]
    consts2 = [lw["qn_b"], lw["kn_b"], lw["qn_c"], lw["kn_c"]]
    widths = [MLA_HEADS * LANES, MLA_HEADS * LANES, MLA_HEADS * MLA_V, hw, hw, hw, hw, hw, hw,
              IDX_HEADS * IDX_DIM, LANES, LANES, 3 * D_MODEL]
    dtypes = [BF16] * 11 + [F32, BF16]
    transposed = [False, False, True] * 3 + [False, False, True, False]
    col = lambda w: pl.BlockSpec((1, w, tm), lambda b, i: (b, 0, i))
    return pl.pallas_call(
        _inproj_kernel,
        grid=(B, S // tm),
        in_specs=[row(D), pl.BlockSpec((1, 6, D), lambda b, i: (b, 0, 0))]
        + [_layer_spec(c.shape, l) for c in consts] + [tab, tab, tab, _const_spec(bd.shape)]
        + [_layer_spec(c.shape, l) for c in consts2],
        out_specs=[col(w) if tr else row(w) for w, tr in zip(widths, transposed)],
        out_shape=[jax.ShapeDtypeStruct((B, w, S) if tr else (B, S, w), dt)
                   for w, dt, tr in zip(widths, dtypes, transposed)],
        compiler_params=_params("arbitrary", "arbitrary"),
        name="in_proj",
    )(x, mod, *consts, *rope_tabs, bd, *consts2)


NEG_INF = float("-inf")


def _fold_rows(x, op, n=8):
    out = x[0:n, :]
    for t in range(1, x.shape[0] // n):
        out = op(out, x[n * t:n * (t + 1), :])
    return out


def _interleave(units):
    pending = []
    for make in units:
        first, second = make()
        for t in range(max(len(first), len(pending))):
            if t < len(first):
                first[t]()
            if t < len(pending):
                pending[t]()
        pending = second
    for step in pending:
        step()


class _Unit:
    def __init__(self, logits, n_chunks, ch, tq, score, vt_chunk, done):
        self.m8 = jnp.full((8, tq), NEG_INF, F32)
        self.l8 = None
        self.acc = None

        def first(j):
            s = score(j)
            logits[j * ch:(j + 1) * ch, :] = s
            self.m8 = jnp.maximum(self.m8, _fold_rows(s, jnp.maximum))

        def second(j):
            if j == 0:
                self.m = jnp.max(self.m8, axis=0, keepdims=True)
            p = jnp.exp(logits[j * ch:(j + 1) * ch, :] - self.m)
            f = _fold_rows(p, jnp.add)
            pv = _mm(vt_chunk(j), p.astype(BF16))
            self.l8 = f if self.l8 is None else self.l8 + f
            self.acc = pv if self.acc is None else self.acc + pv
            if j == n_chunks - 1:
                done(self.acc / jnp.sum(self.l8, axis=0, keepdims=True))

        self.steps = ([functools.partial(first, j) for j in range(n_chunks)],
                      [functools.partial(second, j) for j in range(n_chunks)])


def _mla_kernel(q_ref, k_ref, vt_ref, o_ref, s_ref, *, tq, nq):
    krow = lax.broadcasted_iota(jnp.int32, (tq, tq), 0)
    qcol = lax.broadcasted_iota(jnp.int32, (tq, tq), 1)
    causal = krow <= qcol
    outs = {}

    def unit(c, hh, u):
        sl = slice(LANES * hh, LANES * (hh + 1))
        q = q_ref[0, c * tq:(c + 1) * tq, sl]

        def score(j):
            s = _nt(k_ref[0, j * tq:(j + 1) * tq, sl], q)
            return jnp.where(causal, s, NEG_INF) if j == c else s

        def done(out):
            outs[c, hh] = out
            if hh == 1:
                pair = jnp.concatenate([outs[c, 0], outs[c, 1]], axis=0)
                o_ref[0, c * tq:(c + 1) * tq, :] = pair.T.astype(BF16)

        vt = lambda j: vt_ref[0, MLA_V * hh:MLA_V * (hh + 1), j * tq:(j + 1) * tq]
        return _Unit(s_ref.at[u % 3], c + 1, tq, tq, score, vt, done).steps

    _interleave([functools.partial(unit, c, hh, 2 * c + hh) for c in range(nq) for hh in range(2)])


def _mla_attention(qa, ka, vat, tq):
    B, S, _ = qa.shape
    pairs = MLA_HEADS // 2
    nq = S // tq
    wide = pl.BlockSpec((1, S, 2 * LANES), lambda b, h: (b, 0, h))
    return pl.pallas_call(
        functools.partial(_mla_kernel, tq=tq, nq=nq),
        grid=(B, pairs),
        in_specs=[wide, wide, pl.BlockSpec((1, 2 * MLA_V, S), lambda b, h: (b, h, 0))],
        out_specs=pl.BlockSpec((1, S, 2 * MLA_V), lambda b, h: (b, 0, h)),
        out_shape=jax.ShapeDtypeStruct((B, S, MLA_HEADS * MLA_V), BF16),
        scratch_shapes=[pltpu.VMEM((3, S, tq), F32)],
        compiler_params=_params("arbitrary", "arbitrary"),
        name="mla_attn",
    )(qa, ka, vat)


def _moba_kernel(q_ref, k_ref, vt_ref, o_ref, s_ref, *, nb, nbp, n_sel, slopes):
    blk = MOBA_BLOCK
    hw = MOBA_HEADS * HEAD_DIM
    lane = lax.broadcasted_iota(jnp.int32, (1, hw), 1)

    means = [jnp.mean(k_ref[0, n * blk:(n + 1) * blk, :].astype(F32), axis=0, keepdims=True)
             for n in range(nb)]
    km = jnp.concatenate(means + [jnp.zeros((1, hw), F32)] * (nbp - nb), axis=0)
    pieces = []
    for h in range(MOBA_HEADS):
        hm = (lane >= HEAD_DIM * h) & (lane < HEAD_DIM * (h + 1))
        pieces += [p.astype(F32) for p in _split3(jnp.where(hm, km, 0.0))]
    km_stack = jnp.concatenate(pieces, axis=0).astype(BF16)

    krow = lax.broadcasted_iota(jnp.int32, (blk, blk), 0)
    qcol = lax.broadcasted_iota(jnp.int32, (blk, blk), 1)
    causal = krow <= qcol
    krow_f = krow.astype(F32)
    bidx = lax.broadcasted_iota(jnp.int32, (nbp, blk), 0)

    gates = {}
    outs = {}

    def unit(c, h, u):
        q = q_ref[0, c * blk:(c + 1) * blk, :]
        hm = (lane >= HEAD_DIM * h) & (lane < HEAD_DIM * (h + 1))
        qh = jnp.where(hm, q, jnp.zeros_like(q))
        kbias = slopes[h] * krow_f
        if c > 0:
            if h == 0:
                gates[c] = _nt(km_stack, q)
            past = bidx < c
            g = sum(gates[c][(3 * h + p) * nbp:(3 * h + p + 1) * nbp, :] for p in range(3))
            g = jnp.where(past, g, NEG_INF)
            rank = jnp.zeros((nbp, blk), F32)
            for n2 in range(c):
                gm = g[n2:n2 + 1, :]
                beats = (gm > g) | ((gm == g) & (bidx > n2))
                rank = rank + jnp.where(beats, 1.0, 0.0)
            drop = jnp.where(past & (rank < n_sel), 0.0, NEG_INF)

        def score(j):
            s = _nt(k_ref[0, j * blk:(j + 1) * blk, :], qh) + kbias
            if j == c:
                return jnp.where(causal, s, NEG_INF)
            return s + (drop[j:j + 1, :] + slopes[h] * float((j - c) * blk))

        def done(out):
            outs[c, h] = out
            if h == MOBA_HEADS - 1:
                full = jnp.concatenate([outs[c, hh] for hh in range(MOBA_HEADS)], axis=0)
                o_ref[0, c * blk:(c + 1) * blk, :] = full.T.astype(BF16)

        vt = lambda j: vt_ref[0, HEAD_DIM * h:HEAD_DIM * (h + 1), j * blk:(j + 1) * blk]
        return _Unit(s_ref.at[u % 3], c + 1, blk, blk, score, vt, done).steps

    _interleave([functools.partial(unit, c, h, MOBA_HEADS * c + h) for c in range(nb) for h in range(MOBA_HEADS)])


def _moba_attention(qb, kb, vbt, slopes):
    B, S, hw = qb.shape
    blk = MOBA_BLOCK
    nb = S // blk
    nbp = -(-nb // 8) * 8
    n_sel = max(1, min(MOBA_TOPK, nb - 1))
    full = pl.BlockSpec((1, S, hw), lambda b: (b, 0, 0))
    return pl.pallas_call(
        functools.partial(_moba_kernel, nb=nb, nbp=nbp, n_sel=n_sel, slopes=slopes),
        grid=(B,),
        in_specs=[full, full, pl.BlockSpec((1, hw, S), lambda b: (b, 0, 0))],
        out_specs=full,
        out_shape=jax.ShapeDtypeStruct((B, S, hw), BF16),
        scratch_shapes=[pltpu.VMEM((3, S, blk), F32)],
        compiler_params=_params("arbitrary"),
        name="moba_attn",
    )(qb, kb, vbt)


INT_MIN = -2 ** 31
KEY_POS_INF = 0x7F800000
KEY_NEG_INF = -0x7F800001


def _key_to_float(key):
    key = jnp.clip(key, KEY_NEG_INF, KEY_POS_INF)
    return lax.bitcast_convert_type(jnp.where(key >= 0, key, key ^ jnp.int32(0x7FFFFFFF)), F32)


def _kth_largest(sc_ref, sb_ref, rows, n_chunks, tq, k):
    i32 = jnp.int32

    def count(ref, c, one, zero, n):
        acc = None
        for j in range(n_chunks):
            f = _fold_rows(jnp.where(ref[rows(j), :] >= c, one, zero), jnp.add, n).astype(F32)
            acc = f if acc is None else acc + f
        return jnp.sum(acc, axis=0, keepdims=True)

    count16 = lambda c: count(sb_ref, c.astype(BF16), jnp.bfloat16(1), jnp.bfloat16(0), 16)
    count32 = lambda c: count(sc_ref, c, 1.0, 0.0, 8)
    grid_key = lambda k16: lax.shift_left(k16, 16) | jnp.where(k16 < 0, i32(0xFFFF), i32(0))

    zero = jnp.zeros((1, tq), i32)
    k16 = jnp.where(count16(_key_to_float(grid_key(zero))) >= k, zero, i32(-2 ** 15))

    def coarse(it, k16):
        cand = k16 | lax.shift_left(i32(1), i32(14) - it)
        return jnp.where(count16(_key_to_float(grid_key(cand))) >= k, cand, k16)

    g = grid_key(lax.fori_loop(0, 15, coarse, k16))
    bracket = (g - i32(0x8000), g + i32(0x10001))

    def fine(_, lo_hi):
        lo, hi = lo_hi
        mid = lo + lax.shift_right_arithmetic(hi - lo, 1)
        ok = count32(_key_to_float(mid)) >= k
        return jnp.where(ok, mid, lo), jnp.where(ok, hi, mid)

    lo, _ = lax.fori_loop(0, 17, fine, bracket)
    return _key_to_float(lo)


def _dsa_kernel(q_ref, k_ref, vt_ref, iq_ref, ik_ref, iwt_ref, kpos_ref, o_ref, sc_ref, sb_ref, mb_ref, s_ref, *,
                tq, ch, per, ncls, n_keep, slopes, idx_scale):
    i = pl.program_id(1)
    hw = DSA_HEADS * HEAD_DIM
    q0 = i * tq
    qpos = q0 + lax.broadcasted_iota(jnp.int32, (1, tq), 1)
    krow = lax.broadcasted_iota(jnp.int32, (ch, 1), 0)
    lane = lax.broadcasted_iota(jnp.int32, (1, LANES), 1)
    lane_h = lax.broadcasted_iota(jnp.int32, (1, hw), 1)
    grp = LANES // IDX_DIM

    def body(c):
        n_chunks = (c + 1) * per * tq // ch
        rows = lambda j: slice(j * ch, (j + 1) * ch)
        iwt = iwt_ref[0]
        qms = []
        for hh in range(IDX_HEADS):
            g, r = divmod(hh, grp)
            iqg = iq_ref[0, :, LANES * g:LANES * (g + 1)]
            qms.append(jnp.where((lane >= IDX_DIM * r) & (lane < IDX_DIM * (r + 1)), iqg, jnp.zeros_like(iqg)))
        for j in range(n_chunks):
            ikc = ik_ref[0, rows(j), :]
            score = None
            for hh in range(IDX_HEADS):
                term = jnp.maximum(_nt(ikc, qms[hh]), 0.0) * iwt[hh:hh + 1, :]
                score = term if score is None else score + term
            score = jnp.where(krow + j * ch <= qpos, score * idx_scale, NEG_INF)
            sc_ref[rows(j), :] = score
            sb_ref[rows(j), :] = score.astype(BF16)

        def mask_at_least(thr):
            for j in range(n_chunks):
                keep = (sc_ref[rows(j), :] >= thr) & (krow + j * ch <= qpos)
                mb_ref[rows(j), :] = jnp.where(keep, 0.0, NEG_INF)

        def count(pred):
            acc = None
            for j in range(n_chunks):
                f = _fold_rows(jnp.where(pred(sc_ref[rows(j), :]), 1.0, 0.0), jnp.add)
                acc = f if acc is None else acc + f
            return jnp.sum(acc, axis=0, keepdims=True)

        if n_chunks * ch <= n_keep:
            mask_at_least(jnp.full((1, tq), NEG_INF, F32))
        else:
            thr = _kth_largest(sc_ref, sb_ref, rows, n_chunks, tq, float(n_keep))
            tied = jnp.max(count(lambda v: v >= thr)) > float(n_keep)
            pl.when(jnp.logical_not(tied))(functools.partial(mask_at_least, thr))

            @pl.when(tied)
            def _():
                need = float(n_keep) - count(lambda v: v > thr)
                tri = jnp.where(lax.broadcasted_iota(jnp.int32, (ch, ch), 1)
                                <= lax.broadcasted_iota(jnp.int32, (ch, ch), 0), 1.0, 0.0).astype(BF16)

                def chunk(j, seen):
                    at = pl.ds(pl.multiple_of(j * ch, ch), ch)
                    sc = sc_ref[at, :]
                    eq = jnp.where(sc == thr, 1.0, 0.0)
                    rank = _mm(tri, eq.astype(BF16)) + seen
                    keep = (sc > thr) | ((sc == thr) & (rank <= need))
                    keep = keep & (krow + j * ch <= qpos)
                    mb_ref[at, :] = jnp.where(keep, 0.0, NEG_INF)
                    return seen + jnp.sum(eq, axis=0, keepdims=True)

                lax.fori_loop(0, n_chunks, chunk, jnp.zeros((1, tq), F32))

        q = q_ref[0]
        q0f = q0.astype(F32)
        outs = {}

        def unit(h):
            hm = (lane_h >= HEAD_DIM * h) & (lane_h < HEAD_DIM * (h + 1))
            qh = jnp.where(hm, q, jnp.zeros_like(q))

            def score(j):
                kb = slopes[h] * (kpos_ref[rows(j), :] - q0f)
                kb = jnp.concatenate([kb] * (tq // LANES), axis=1)
                return _nt(k_ref[0, rows(j), :], qh) + kb + mb_ref[rows(j), :]

            def done(out):
                outs[h] = out
                if h == DSA_HEADS - 1:
                    full = jnp.concatenate([outs[hh] for hh in range(DSA_HEADS)], axis=0)
                    o_ref[0] = full.T.astype(BF16)

            vt = lambda j: vt_ref[0, HEAD_DIM * h:HEAD_DIM * (h + 1), rows(j)]
            return _Unit(s_ref.at[h], n_chunks, ch, tq, score, vt, done).steps

        _interleave([functools.partial(unit, h) for h in range(DSA_HEADS)])

    for c in range(ncls):
        pl.when((i >= c * per) & (i < (c + 1) * per))(functools.partial(body, c))


def _dsa_attention(qc, kc, vct, iq, ik, iwt, slopes, tq, ncls):
    B, S, hw = qc.shape
    nq = S // tq
    per = nq // ncls
    ch = tq
    n_keep = min(DSA_TOPK, S // 4)
    kpos = jnp.broadcast_to(jnp.arange(S, dtype=F32)[:, None], (S, LANES))
    full = lambda w: pl.BlockSpec((1, S, w), lambda b, i: (b, 0, 0))
    tile = lambda w: pl.BlockSpec((1, tq, w), lambda b, i: (b, i, 0))
    return pl.pallas_call(
        functools.partial(_dsa_kernel, tq=tq, ch=ch, per=per, ncls=ncls, n_keep=n_keep, slopes=slopes,
                          idx_scale=(IDX_DIM * IDX_HEADS) ** -0.5),
        grid=(B, nq),
        in_specs=[tile(hw), full(hw), pl.BlockSpec((1, hw, S), lambda b, i: (b, 0, 0)),
                  tile(IDX_HEADS * IDX_DIM), full(LANES), pl.BlockSpec((1, LANES, tq), lambda b, i: (b, 0, i)),
                  pl.BlockSpec((S, LANES), lambda b, i: (0, 0))],
        out_specs=tile(hw),
        out_shape=jax.ShapeDtypeStruct((B, S, hw), BF16),
        scratch_shapes=[pltpu.VMEM((S, tq), F32), pltpu.VMEM((S, tq), BF16), pltpu.VMEM((S, tq), F32),
                        pltpu.VMEM((DSA_HEADS, S, tq), F32)],
        compiler_params=_params("arbitrary", "arbitrary"),
        name="dsa_attn",
    )(qc, kc, vct, iq, ik, iwt, kpos)


def _merge_kernel(ya_ref, yb_ref, yc_ref, gt_ref, x_ref, mod_ref, wa_ref, wb_ref, wc_ref, wo_ref, o_ref):
    d = D_MODEL
    merged = gt_ref[0, :, 0:d].astype(F32) * _mm(ya_ref[0], wa_ref[0])
    merged = merged + gt_ref[0, :, d:2 * d].astype(F32) * _mm(yb_ref[0], wb_ref[0])
    merged = merged + gt_ref[0, :, 2 * d:3 * d].astype(F32) * _mm(yc_ref[0], wc_ref[0])
    o_ref[0] = x_ref[0] + mod_ref[0, 2:3, :] * _mm(merged.astype(BF16), wo_ref[0])


def _merge(ya, yb, yc, gt, x, mod, lw, l, tm):
    B, S, D = x.shape
    row = lambda w: pl.BlockSpec((1, tm, w), lambda b, i: (b, i, 0))
    consts = [lw["w_br_a"], lw["w_br_b"], lw["w_br_c"], lw["w_out"]]
    return pl.pallas_call(
        _merge_kernel,
        grid=(B, S // tm),
        in_specs=[row(ya.shape[-1]), row(yb.shape[-1]), row(yc.shape[-1]), row(3 * D), row(D),
                  pl.BlockSpec((1, 6, D), lambda b, i: (b, 0, 0))] + [_layer_spec(c.shape, l) for c in consts],
        out_specs=row(D),
        out_shape=jax.ShapeDtypeStruct((B, S, D), F32),
        compiler_params=_params("arbitrary", "arbitrary"),
        name="merge_out",
    )(ya, yb, yc, gt, x, mod, *consts)


def _ffn_kernel(x_ref, mod_ref, n2_ref, wg_ref, wu_ref, wd_ref, o_ref):
    x = x_ref[0]
    h = (_rms(x, n2_ref[0]) * (1.0 + mod_ref[0, 4:5, :]) + mod_ref[0, 3:4, :]).astype(BF16)
    g = _mm(h, wg_ref[0])
    u = _mm(h, wu_ref[0])
    act = (g * jax.nn.sigmoid(g) * u).astype(BF16)
    o_ref[0] = x + mod_ref[0, 5:6, :] * _mm(act, wd_ref[0])


def _ffn(x, mod, lw, l, tm):
    B, S, D = x.shape
    row = pl.BlockSpec((1, tm, D), lambda b, i: (b, i, 0))
    gu = lw["w_gu"]
    half = (gu.shape[0], gu.shape[1], gu.shape[2] // 2)
    consts = [lw["n2"], gu, gu, lw["w_d"]]
    return pl.pallas_call(
        _ffn_kernel,
        grid=(B, S // tm),
        in_specs=[row, pl.BlockSpec((1, 6, D), lambda b, i: (b, 0, 0)), _layer_spec(lw["n2"].shape, l),
                  _layer_spec(half, l, 0), _layer_spec(half, l, 1), _layer_spec(lw["w_d"].shape, l)],
        out_specs=row,
        out_shape=jax.ShapeDtypeStruct((B, S, D), F32),
        compiler_params=_params("arbitrary", "arbitrary"),
        name="swiglu",
    )(x, mod, *consts)


def _pack_weights(norm1, w_in, g_cq, w_uq, g_ckv, w_ukv, qn_mla, kn_mla, qn_moba, kn_moba, qn_dsa, kn_dsa,
                  w_br_a, w_br_b, w_br_c, w_out, norm2, w_gu, w_down):
    L = w_in.shape[0]
    o = 0
    cols = {}
    for name, wd in (("cq", MLA_Q_RANK), ("ckv", MLA_KV_RANK), ("kr", MLA_ROPE),
                     ("b", 3 * MOBA_HEADS * HEAD_DIM), ("c", 3 * DSA_HEADS * HEAD_DIM),
                     ("iq", IDX_HEADS * IDX_DIM), ("ik", IDX_DIM), ("iw", IDX_HEADS), ("g", 3 * D_MODEL)):
        cols[name] = w_in[:, :, o:o + wd]
        o += wd
    pad_last = lambda v, lo, hi: jnp.pad(v, ((0, 0),) * (v.ndim - 1) + ((lo, hi),))
    kr_slot = pad_last(cols["kr"], MLA_NOPE, LANES - MLA_QK)
    ik_rep = jnp.tile(cols["ik"], (1, 1, LANES // IDX_DIM))
    hw = MOBA_HEADS * HEAD_DIM
    w_all = jnp.concatenate([cols["cq"], cols["ckv"], kr_slot, cols["b"][:, :, :2 * hw], cols["c"][:, :, :2 * hw],
                             cols["iq"], ik_rep, cols["g"]], axis=2).astype(BF16)
    w_t = jnp.concatenate([cols["b"][:, :, 2 * hw:], cols["c"][:, :, 2 * hw:],
                           pad_last(cols["iw"], 0, LANES - IDX_HEADS)], axis=2)
    wuq = pad_last(w_uq.reshape(L, MLA_Q_RANK, MLA_HEADS, MLA_QK), 0, LANES - MLA_QK)
    wukv = w_ukv.reshape(L, MLA_KV_RANK, MLA_HEADS, MLA_NOPE + MLA_V)
    wk = pad_last(wukv[..., :MLA_NOPE], 0, LANES - MLA_NOPE)
    wv = wukv[..., MLA_NOPE:].reshape(L, MLA_KV_RANK, MLA_HEADS * MLA_V)
    row = lambda v: v.reshape(L, 1, -1)
    t = lambda v: jnp.swapaxes(v, 1, 2)
    return {
        "n1": row(norm1), "w_all": w_all, "w_t": t(w_t).astype(BF16), "g_cq": row(g_cq),
        "w_uq": wuq.reshape(L, MLA_Q_RANK, MLA_HEADS * LANES).astype(BF16), "g_ckv": row(g_ckv),
        "w_k": wk.reshape(L, MLA_KV_RANK, MLA_HEADS * LANES).astype(BF16), "w_v": t(wv).astype(BF16),
        "qn_a": row(pad_last(qn_mla * MLA_QK ** -0.5, 0, LANES - MLA_QK)),
        "kn_a": row(pad_last(kn_mla, 0, LANES - MLA_QK)),
        "qn_b": row(jnp.tile(qn_moba * HEAD_DIM ** -0.5, (1, MOBA_HEADS))), "kn_b": row(jnp.tile(kn_moba, (1, MOBA_HEADS))),
        "qn_c": row(jnp.tile(qn_dsa * HEAD_DIM ** -0.5, (1, DSA_HEADS))), "kn_c": row(jnp.tile(kn_dsa, (1, DSA_HEADS))),
        "w_br_a": w_br_a.astype(BF16), "w_br_b": w_br_b.astype(BF16), "w_br_c": w_br_c.astype(BF16),
        "w_out": w_out.astype(BF16), "n2": row(norm2), "w_gu": w_gu.astype(BF16), "w_d": w_down.astype(BF16),
    }


def _rope_tables(S):
    half = MLA_ROPE // 2
    freqs = ROPE_THETA ** (-jnp.arange(half, dtype=F32) / half)
    ang = jnp.arange(S, dtype=F32)[:, None] * freqs[None, :]
    cos, sin = jnp.cos(ang), jnp.sin(ang)
    zeros = lambda n: jnp.zeros((S, n), F32)
    tail = LANES - MLA_QK
    rc = jnp.concatenate([jnp.ones((S, MLA_NOPE), F32), cos, cos, jnp.ones((S, tail), F32)], axis=1)
    rs1 = jnp.concatenate([zeros(MLA_NOPE), -sin, zeros(half + tail)], axis=1)
    rs2 = jnp.concatenate([zeros(MLA_NOPE + half), sin, zeros(tail)], axis=1)
    return rc, rs1, rs2


def _tiles(S):
    dsa_q = min(256, S)
    return {"proj": min(256, S), "out": min(512, S), "mla_q": min(256, S), "dsa_q": dsa_q,
            "dsa_cls": min(4, S // dsa_q)}


def kernel(x, c, w_ada, b_ada, norm1, w_in, g_cq, w_uq, g_ckv, w_ukv, qn_mla, kn_mla, qn_moba, kn_moba, qn_dsa, kn_dsa, w_br_a, w_br_b, w_br_c, w_out, norm2, w_gu, w_down):
    B, S, D = x.shape
    L = w_ada.shape[0]
    t = _tiles(S)
    assert D == D_MODEL and S % MOBA_BLOCK == 0 and all(S % t[k] == 0 for k in ("proj", "out", "mla_q", "dsa_q"))
    n_slopes = MOBA_HEADS + DSA_HEADS
    slopes = [2.0 ** (-8.0 * (i + 1) / n_slopes) for i in range(n_slopes)]
    mod_all = _ada(c, w_ada, b_ada).reshape(L, B, 6, D)
    rope_tabs = _rope_tables(S)
    seg = jnp.arange(MOBA_HEADS * HEAD_DIM) // HEAD_DIM
    bd = (seg[:, None] == seg[None, :]).astype(BF16)
    lw = _pack_weights(norm1, w_in, g_cq, w_uq, g_ckv, w_ukv, qn_mla, kn_mla, qn_moba, kn_moba, qn_dsa, kn_dsa,
                       w_br_a, w_br_b, w_br_c, w_out, norm2, w_gu, w_down)
    for l in range(L):
        mod = mod_all[l]
        qa, ka, va, qb, kb, vb, qc, kc, vc, iq, ik, iw, gt = _inproj(x, mod, lw, l, rope_tabs, bd, t["proj"])
        ya = _mla_attention(qa, ka, va, tq=t["mla_q"])
        yb = _moba_attention(qb, kb, vb, tuple(slopes[0::2]))
        yc = _dsa_attention(qc, kc, vc, iq, ik, iw, tuple(slopes[1::2]), tq=t["dsa_q"], ncls=t["dsa_cls"])
        x = _merge(ya, yb, yc, gt, x, mod, lw, l, t["out"])
        x = _ffn(x, mod, lw, l, t["out"])
    return x
```

```python
import functools

import jax
import jax.numpy as jnp
from jax import lax
from jax.experimental import pallas as pl
from jax.experimental.pallas import tpu as pltpu

F32 = jnp.float32
BF16 = jnp.bfloat16

D_MODEL = 1024
HEAD_DIM = 64
RMS_EPS = 1e-6
MLA_HEADS = 8
MLA_NOPE = 64
MLA_ROPE = 32
MLA_QK = MLA_NOPE + MLA_ROPE
MLA_V = 64
MLA_Q_RANK = 768
MLA_KV_RANK = 256
ROPE_THETA = 10000.0
MOBA_HEADS = 4
MOBA_BLOCK = 256
MOBA_TOPK = 3
DSA_HEADS = 4
IDX_HEADS = 16
IDX_DIM = 32
DSA_TOPK = 256

LANES = 128
VMEM_LIMIT = 56 * 1024 * 1024

OFF_CQ = 0
OFF_CKV = OFF_CQ + MLA_Q_RANK
OFF_KR = OFF_CKV + MLA_KV_RANK
OFF_B = OFF_KR + LANES
OFF_C = OFF_B + 2 * MOBA_HEADS * HEAD_DIM
OFF_IQ = OFF_C + 2 * DSA_HEADS * HEAD_DIM
OFF_IK = OFF_IQ + IDX_HEADS * IDX_DIM
OFF_G = OFF_IK + LANES
N_ALL = OFF_G + 3 * D_MODEL
ROW_VB = 0
ROW_VC = ROW_VB + MOBA_HEADS * HEAD_DIM
ROW_IW = ROW_VC + DSA_HEADS * HEAD_DIM
N_T = ROW_IW + LANES


def _nt(a, b):
    return lax.dot_general(a, b, (((1,), (1,)), ((), ())), preferred_element_type=F32)


def _mm(a, b):
    return jnp.dot(a, b, preferred_element_type=F32)


def _split3(x):
    hi = x.astype(BF16)
    r = x - hi.astype(F32)
    mid = r.astype(BF16)
    lo = (r - mid.astype(F32)).astype(BF16)
    return hi, mid, lo


def _rms(x, g):
    return x * lax.rsqrt(jnp.mean(x * x, axis=-1, keepdims=True) + RMS_EPS) * g


def _const_spec(shape):
    nd = len(shape)
    return pl.BlockSpec(shape, lambda *_: (0,) * nd, pipeline_mode=pl.Buffered(1))


def _layer_spec(shape, l, part=0):
    rest = (0,) * (len(shape) - 2) + (part,)
    return pl.BlockSpec((1,) + tuple(shape[1:]), lambda *_: (l,) + rest, pipeline_mode=pl.Buffered(1))


def _params(*sem):
    return pltpu.CompilerParams(dimension_semantics=sem, vmem_limit_bytes=VMEM_LIMIT)


def _ada_kernel(c_ref, w_ref, b_ref, o_ref):
    c = c_ref[...]
    cond = (c * jax.nn.sigmoid(c)).astype(BF16)
    o_ref[0] = _mm(cond, w_ref[0].astype(BF16)) + b_ref[0]


def _ada(c, w_ada, b_ada):
    L, D, N = w_ada.shape
    B = c.shape[0]
    tn = 1024
    return pl.pallas_call(
        _ada_kernel,
        grid=(L, N // tn),
        in_specs=[
            pl.BlockSpec((B, D), lambda l, n: (0, 0)),
            pl.BlockSpec((1, D, tn), lambda l, n: (l, 0, n)),
            pl.BlockSpec((1, 1, tn), lambda l, n: (l, 0, n)),
        ],
        out_specs=pl.BlockSpec((1, B, tn), lambda l, n: (l, 0, n)),
        out_shape=jax.ShapeDtypeStruct((L, B, N), F32),
        compiler_params=_params("arbitrary", "arbitrary"),
        name="ada_mod",
    )(c, w_ada, b_ada.reshape(L, 1, N))


def _rope(x, c, s1, s2):
    return x * c + pltpu.roll(x, LANES - 16, 1) * s1 + pltpu.roll(x, 16, 1) * s2


def _inproj_kernel(x_ref, mod_ref, n1_ref, w_ref, wt_ref, gcq_ref, wuq_ref, gckv_ref, wk_ref, wv_ref,
                   qna_ref, kna_ref, rc_ref, rs1_ref, rs2_ref, bd_ref,
                   qnb_ref, knb_ref, qnc_ref, knc_ref,
                   qa_ref, ka_ref, va_ref, qb_ref, kb_ref, vb_ref, qc_ref, kc_ref, vc_ref,
                   iq_ref, ik_ref, iw_ref, gt_ref):
    x = x_ref[0]
    sh1 = mod_ref[0, 0:1, :]
    sc1 = mod_ref[0, 1:2, :]
    hb = (_rms(x, n1_ref[0]) * (1.0 + sc1) + sh1).astype(BF16)

    a = _mm(hb, w_ref[0, :, OFF_CQ:OFF_B])
    cqn = _rms(a[:, OFF_CQ:OFF_CKV], gcq_ref[0]).astype(BF16)
    ckvn = _rms(a[:, OFF_CKV:OFF_KR], gckv_ref[0]).astype(BF16)
    kr = a[:, OFF_KR:OFF_B]
    qraw = _mm(cqn, wuq_ref[0])
    kraw = _mm(ckvn, wk_ref[0])
    va_ref[0] = _nt(wv_ref[0], ckvn).astype(BF16)
    rc, rs1, rs2 = rc_ref[...], rs1_ref[...], rs2_ref[...]
    qna, kna = qna_ref[0], kna_ref[0]
    inv_qk = 1.0 / MLA_QK
    for h in range(MLA_HEADS):
        sl = slice(LANES * h, LANES * (h + 1))
        qh = qraw[:, sl]
        qh = qh * lax.rsqrt(jnp.sum(qh * qh, axis=-1, keepdims=True) * inv_qk + RMS_EPS) * qna
        qa_ref[0, :, sl] = _rope(qh, rc, rs1, rs2).astype(BF16)
        kh = kraw[:, sl] + kr
        kh = kh * lax.rsqrt(jnp.sum(kh * kh, axis=-1, keepdims=True) * inv_qk + RMS_EPS) * kna
        ka_ref[0, :, sl] = _rope(kh, rc, rs1, rs2).astype(BF16)

    bd = bd_ref[...]

    def segnorm(v, g):
        ss = sum(_mm(part, bd) for part in _split3(v * v)[:2])
        return v * lax.rsqrt(ss * (1.0 / HEAD_DIM) + RMS_EPS) * g

    hw = MOBA_HEADS * HEAD_DIM
    pb = _mm(hb, w_ref[0, :, OFF_B:OFF_C])
    qb_ref[0] = segnorm(pb[:, 0:hw], qnb_ref[0]).astype(BF16)
    kb_ref[0] = segnorm(pb[:, hw:2 * hw], knb_ref[0]).astype(BF16)
    pc = _mm(hb, w_ref[0, :, OFF_C:OFF_IQ])
    qc_ref[0] = segnorm(pc[:, 0:hw], qnc_ref[0]).astype(BF16)
    kc_ref[0] = segnorm(pc[:, hw:2 * hw], knc_ref[0]).astype(BF16)

    pt = _nt(wt_ref[0], hb)
    vb_ref[0] = pt[ROW_VB:ROW_VC, :].astype(BF16)
    vc_ref[0] = pt[ROW_VC:ROW_IW, :].astype(BF16)
    iw_ref[0] = pt[ROW_IW:N_T, :]

    pi = _mm(hb, w_ref[0, :, OFF_IQ:OFF_G])
    iq_ref[0] = pi[:, 0:OFF_IK - OFF_IQ].astype(BF16)
    ik_ref[0] = pi[:, OFF_IK - OFF_IQ:OFF_G - OFF_IQ].astype(BF16)
    gt_ref[0] = jax.nn.sigmoid(_mm(hb, w_ref[0, :, OFF_G:N_ALL])).astype(BF16)


def _inproj(x, mod, lw, l, rope_tabs, bd, tm):
    B, S, D = x.shape
    hw = MOBA_HEADS * HEAD_DIM
    row = lambda w: pl.BlockSpec((1, tm, w), lambda b, i: (b, i, 0))
    tab = pl.BlockSpec((tm, LANES), lambda b, i: (i, 0))
    consts = [lw["n1"], lw["w_all"], lw["w_t"], lw["g_cq"], lw["w_uq"], lw["g_ckv"], lw["w_k"], lw["w_v"],
              lw["qn_a"], lw["kn_a"]]
    consts2 = [lw["qn_b"], lw["kn_b"], lw["qn_c"], lw["kn_c"]]
    widths = [MLA_HEADS * LANES, MLA_HEADS * LANES, MLA_HEADS * MLA_V, hw, hw, hw, hw, hw, hw,
              IDX_HEADS * IDX_DIM, LANES, LANES, 3 * D_MODEL]
    dtypes = [BF16] * 11 + [F32, BF16]
    transposed = [False, False, True] * 3 + [False, False, True, False]
    col = lambda w: pl.BlockSpec((1, w, tm), lambda b, i: (b, 0, i))
    return pl.pallas_call(
        _inproj_kernel,
        grid=(B, S // tm),
        in_specs=[row(D), pl.BlockSpec((1, 6, D), lambda b, i: (b, 0, 0))]
        + [_layer_spec(c.shape, l) for c in consts] + [tab, tab, tab, _const_spec(bd.shape)]
        + [_layer_spec(c.shape, l) for c in consts2],
        out_specs=[col(w) if tr else row(w) for w, tr in zip(widths, transposed)],
        out_shape=[jax.ShapeDtypeStruct((B, w, S) if tr else (B, S, w), dt)
                   for w, dt, tr in zip(widths, dtypes, transposed)],
        compiler_params=_params("arbitrary", "arbitrary"),
        name="in_proj",
    )(x, mod, *consts, *rope_tabs, bd, *consts2)


NEG_INF = float("-inf")


def _fold_rows(x, op, n=8):
    out = x[0:n, :]
    for t in range(1, x.shape[0] // n):
        out = op(out, x[n * t:n * (t + 1), :])
    return out


def _interleave(units):
    pending = []
    for make in units:
        first, second = make()
        for t in range(max(len(first), len(pending))):
            if t < len(first):
                first[t]()
            if t < len(pending):
                pending[t]()
        pending = second
    for step in pending:
        step()


class _Unit:
    def __init__(self, logits, n_chunks, ch, tq, score, vt_chunk, done):
        self.m8 = jnp.full((8, tq), NEG_INF, F32)
        self.l8 = None
        self.acc = None

        def first(j):
            s = score(j)
            logits[j * ch:(j + 1) * ch, :] = s
            self.m8 = jnp.maximum(self.m8, _fold_rows(s, jnp.maximum))

        def second(j):
            if j == 0:
                self.m = jnp.max(self.m8, axis=0, keepdims=True)
            p = jnp.exp(logits[j * ch:(j + 1) * ch, :] - self.m)
            f = _fold_rows(p, jnp.add)
            pv = _mm(vt_chunk(j), p.astype(BF16))
            self.l8 = f if self.l8 is None else self.l8 + f
            self.acc = pv if self.acc is None else self.acc + pv
            if j == n_chunks - 1:
                done(self.acc / jnp.sum(self.l8, axis=0, keepdims=True))

        self.steps = ([functools.partial(first, j) for j in range(n_chunks)],
                      [functools.partial(second, j) for j in range(n_chunks)])


def _mla_kernel(q_ref, k_ref, vt_ref, o_ref, s_ref, *, tq, nq):
    krow = lax.broadcasted_iota(jnp.int32, (tq, tq), 0)
    qcol = lax.broadcasted_iota(jnp.int32, (tq, tq), 1)
    causal = krow <= qcol
    outs = {}

    def unit(c, hh, u):
        sl = slice(LANES * hh, LANES * (hh + 1))
        q = q_ref[0, c * tq:(c + 1) * tq, sl]

        def score(j):
            s = _nt(k_ref[0, j * tq:(j + 1) * tq, sl], q)
            return jnp.where(causal, s, NEG_INF) if j == c else s

        def done(out):
            outs[c, hh] = out
            if hh == 1:
                pair = jnp.concatenate([outs[c, 0], outs[c, 1]], axis=0)
                o_ref[0, c * tq:(c + 1) * tq, :] = pair.T.astype(BF16)

        vt = lambda j: vt_ref[0, MLA_V * hh:MLA_V * (hh + 1), j * tq:(j + 1) * tq]
        return _Unit(s_ref.at[u % 3], c + 1, tq, tq, score, vt, done).steps

    _interleave([functools.partial(unit, c, hh, 2 * c + hh) for c in range(nq) for hh in range(2)])


def _mla_attention(qa, ka, vat, tq):
    B, S, _ = qa.shape
    pairs = MLA_HEADS // 2
    nq = S // tq
    wide = pl.BlockSpec((1, S, 2 * LANES), lambda b, h: (b, 0, h))
    return pl.pallas_call(
        functools.partial(_mla_kernel, tq=tq, nq=nq),
        grid=(B, pairs),
        in_specs=[wide, wide, pl.BlockSpec((1, 2 * MLA_V, S), lambda b, h: (b, h, 0))],
        out_specs=pl.BlockSpec((1, S, 2 * MLA_V), lambda b, h: (b, 0, h)),
        out_shape=jax.ShapeDtypeStruct((B, S, MLA_HEADS * MLA_V), BF16),
        scratch_shapes=[pltpu.VMEM((3, S, tq), F32)],
        compiler_params=_params("arbitrary", "arbitrary"),
        name="mla_attn",
    )(qa, ka, vat)


def _moba_kernel(q_ref, k_ref, vt_ref, o_ref, s_ref, *, nb, nbp, n_sel, slopes):
    blk = MOBA_BLOCK
    hw = MOBA_HEADS * HEAD_DIM
    lane = lax.broadcasted_iota(jnp.int32, (1, hw), 1)

    means = [jnp.mean(k_ref[0, n * blk:(n + 1) * blk, :].astype(F32), axis=0, keepdims=True)
             for n in range(nb)]
    km = jnp.concatenate(means + [jnp.zeros((1, hw), F32)] * (nbp - nb), axis=0)
    pieces = []
    for h in range(MOBA_HEADS):
        hm = (lane >= HEAD_DIM * h) & (lane < HEAD_DIM * (h + 1))
        pieces += [p.astype(F32) for p in _split3(jnp.where(hm, km, 0.0))]
    km_stack = jnp.concatenate(pieces, axis=0).astype(BF16)

    krow = lax.broadcasted_iota(jnp.int32, (blk, blk), 0)
    qcol = lax.broadcasted_iota(jnp.int32, (blk, blk), 1)
    causal = krow <= qcol
    krow_f = krow.astype(F32)
    bidx = lax.broadcasted_iota(jnp.int32, (nbp, blk), 0)

    gates = {}
    outs = {}

    def unit(c, h, u):
        q = q_ref[0, c * blk:(c + 1) * blk, :]
        hm = (lane >= HEAD_DIM * h) & (lane < HEAD_DIM * (h + 1))
        qh = jnp.where(hm, q, jnp.zeros_like(q))
        kbias = slopes[h] * krow_f
        if c > 0:
            if h == 0:
                gates[c] = _nt(km_stack, q)
            past = bidx < c
            g = sum(gates[c][(3 * h + p) * nbp:(3 * h + p + 1) * nbp, :] for p in range(3))
            g = jnp.where(past, g, NEG_INF)
            rank = jnp.zeros((nbp, blk), F32)
            for n2 in range(c):
                gm = g[n2:n2 + 1, :]
                beats = (gm > g) | ((gm == g) & (bidx > n2))
                rank = rank + jnp.where(beats, 1.0, 0.0)
            drop = jnp.where(past & (rank < n_sel), 0.0, NEG_INF)

        def score(j):
            s = _nt(k_ref[0, j * blk:(j + 1) * blk, :], qh) + kbias
            if j == c:
                return jnp.where(causal, s, NEG_INF)
            return s + (drop[j:j + 1, :] + slopes[h] * float((j - c) * blk))

        def done(out):
            outs[c, h] = out
            if h == MOBA_HEADS - 1:
                full = jnp.concatenate([outs[c, hh] for hh in range(MOBA_HEADS)], axis=0)
                o_ref[0, c * blk:(c + 1) * blk, :] = full.T.astype(BF16)

        vt = lambda j: vt_ref[0, HEAD_DIM * h:HEAD_DIM * (h + 1), j * blk:(j + 1) * blk]
        return _Unit(s_ref.at[u % 3], c + 1, blk, blk, score, vt, done).steps

    _interleave([functools.partial(unit, c, h, MOBA_HEADS * c + h) for c in range(nb) for h in range(MOBA_HEADS)])


def _moba_attention(qb, kb, vbt, slopes):
    B, S, hw = qb.shape
    blk = MOBA_BLOCK
    nb = S // blk
    nbp = -(-nb // 8) * 8
    n_sel = max(1, min(MOBA_TOPK, nb - 1))
    full = pl.BlockSpec((1, S, hw), lambda b: (b, 0, 0))
    return pl.pallas_call(
        functools.partial(_moba_kernel, nb=nb, nbp=nbp, n_sel=n_sel, slopes=slopes),
        grid=(B,),
        in_specs=[full, full, pl.BlockSpec((1, hw, S), lambda b: (b, 0, 0))],
        out_specs=full,
        out_shape=jax.ShapeDtypeStruct((B, S, hw), BF16),
        scratch_shapes=[pltpu.VMEM((3, S, blk), F32)],
        compiler_params=_params("arbitrary"),
        name="moba_attn",
    )(qb, kb, vbt)


KEY_POS_INF = 0x7F800000
KEY_NEG_INF = -0x7F800001


def _key_to_float(key):
    key = jnp.clip(key, KEY_NEG_INF, KEY_POS_INF)
    return lax.bitcast_convert_type(jnp.where(key >= 0, key, key ^ jnp.int32(0x7FFFFFFF)), F32)


def _kth_largest(sc_ref, sb_ref, rows, n_chunks, tq, k):
    i32 = jnp.int32

    def count(ref, c, one, zero, n):
        acc = None
        for j in range(n_chunks):
            f = _fold_rows(jnp.where(ref[rows(j), :] >= c, one, zero), jnp.add, n).astype(F32)
            acc = f if acc is None else acc + f
        return jnp.sum(acc, axis=0, keepdims=True)

    count16 = lambda c: count(sb_ref, c.astype(BF16), jnp.bfloat16(1), jnp.bfloat16(0), 16)
    count32 = lambda c: count(sc_ref, c, 1.0, 0.0, 8)
    grid_key = lambda k16: lax.shift_left(k16, 16) | jnp.where(k16 < 0, i32(0xFFFF), i32(0))

    zero = jnp.zeros((1, tq), i32)
    k16 = jnp.where(count16(_key_to_float(grid_key(zero))) >= k, zero, i32(-2 ** 15))

    def coarse(it, k16):
        cand = k16 | lax.shift_left(i32(1), i32(14) - it)
        return jnp.where(count16(_key_to_float(grid_key(cand))) >= k, cand, k16)

    g = grid_key(lax.fori_loop(0, 15, coarse, k16))
    bracket = (g - i32(0x8000), g + i32(0x10001))

    def fine(_, lo_hi):
        lo, hi = lo_hi
        mid = lo + lax.shift_right_arithmetic(hi - lo, 1)
        ok = count32(_key_to_float(mid)) >= k
        return jnp.where(ok, mid, lo), jnp.where(ok, hi, mid)

    lo, _ = lax.fori_loop(0, 17, fine, bracket)
    return _key_to_float(lo)


def _dsa_kernel(q_ref, k_ref, vt_ref, iq_ref, ik_ref, iwt_ref, kpos_ref, o_ref, sc_ref, sb_ref, mb_ref, s_ref, *,
                tq, ch, per, ncls, n_keep, slopes, idx_scale):
    i = pl.program_id(1)
    hw = DSA_HEADS * HEAD_DIM
    q0 = i * tq
    qpos = q0 + lax.broadcasted_iota(jnp.int32, (1, tq), 1)
    krow = lax.broadcasted_iota(jnp.int32, (ch, 1), 0)
    lane = lax.broadcasted_iota(jnp.int32, (1, LANES), 1)
    lane_h = lax.broadcasted_iota(jnp.int32, (1, hw), 1)
    grp = LANES // IDX_DIM

    def body(c):
        n_chunks = (c + 1) * per * tq // ch
        rows = lambda j: slice(j * ch, (j + 1) * ch)
        iwt = iwt_ref[0]
        qms = []
        for hh in range(IDX_HEADS):
            g, r = divmod(hh, grp)
            iqg = iq_ref[0, :, LANES * g:LANES * (g + 1)]
            qms.append(jnp.where((lane >= IDX_DIM * r) & (lane < IDX_DIM * (r + 1)), iqg, jnp.zeros_like(iqg)))
        for j in range(n_chunks):
            ikc = ik_ref[0, rows(j), :]
            score = None
            for hh in range(IDX_HEADS):
                term = jnp.maximum(_nt(ikc, qms[hh]), 0.0) * iwt[hh:hh + 1, :]
                score = term if score is None else score + term
            score = jnp.where(krow + j * ch <= qpos, score * idx_scale, NEG_INF)
            sc_ref[rows(j), :] = score
            sb_ref[rows(j), :] = score.astype(BF16)

        def mask_at_least(thr):
            for j in range(n_chunks):
                keep = (sc_ref[rows(j), :] >= thr) & (krow + j * ch <= qpos)
                mb_ref[rows(j), :] = jnp.where(keep, 0.0, NEG_INF)

        def count(pred):
            acc = None
            for j in range(n_chunks):
                f = _fold_rows(jnp.where(pred(sc_ref[rows(j), :]), 1.0, 0.0), jnp.add)
                acc = f if acc is None else acc + f
            return jnp.sum(acc, axis=0, keepdims=True)

        keep_all = functools.partial(mask_at_least, jnp.full((1, tq), NEG_INF, F32))

        def select_and_mask():
            thr = _kth_largest(sc_ref, sb_ref, rows, n_chunks, tq, float(n_keep))
            tied = jnp.max(count(lambda v: v >= thr)) > float(n_keep)
            pl.when(jnp.logical_not(tied))(functools.partial(mask_at_least, thr))

            @pl.when(tied)
            def _():
                need = float(n_keep) - count(lambda v: v > thr)
                tri = jnp.where(lax.broadcasted_iota(jnp.int32, (ch, ch), 1)
                                <= lax.broadcasted_iota(jnp.int32, (ch, ch), 0), 1.0, 0.0).astype(BF16)

                def chunk(j, seen):
                    at = pl.ds(pl.multiple_of(j * ch, ch), ch)
                    sc = sc_ref[at, :]
                    eq = jnp.where(sc == thr, 1.0, 0.0)
                    rank = _mm(tri, eq.astype(BF16)) + seen
                    keep = (sc > thr) | ((sc == thr) & (rank <= need))
                    keep = keep & (krow + j * ch <= qpos)
                    mb_ref[at, :] = jnp.where(keep, 0.0, NEG_INF)
                    return seen + jnp.sum(eq, axis=0, keepdims=True)

                lax.fori_loop(0, n_chunks, chunk, jnp.zeros((1, tq), F32))

        if n_chunks * ch <= n_keep:
            keep_all()
        elif c * per * tq < n_keep:
            few = q0 + tq <= n_keep
            pl.when(few)(keep_all)
            pl.when(jnp.logical_not(few))(select_and_mask)
        else:
            select_and_mask()

        q = q_ref[0]
        q0f = q0.astype(F32)
        outs = {}

        def unit(h):
            hm = (lane_h >= HEAD_DIM * h) & (lane_h < HEAD_DIM * (h + 1))
            qh = jnp.where(hm, q, jnp.zeros_like(q))

            def score(j):
                kb = slopes[h] * (kpos_ref[rows(j), :] - q0f)
                kb = jnp.concatenate([kb] * (tq // LANES), axis=1)
                return _nt(k_ref[0, rows(j), :], qh) + kb + mb_ref[rows(j), :]

            def done(out):
                outs[h] = out
                if h == DSA_HEADS - 1:
                    full = jnp.concatenate([outs[hh] for hh in range(DSA_HEADS)], axis=0)
                    o_ref[0] = full.T.astype(BF16)

            vt = lambda j: vt_ref[0, HEAD_DIM * h:HEAD_DIM * (h + 1), rows(j)]
            return _Unit(s_ref.at[h], n_chunks, ch, tq, score, vt, done).steps

        _interleave([functools.partial(unit, h) for h in range(DSA_HEADS)])

    for c in range(ncls):
        pl.when((i >= c * per) & (i < (c + 1) * per))(functools.partial(body, c))


def _dsa_attention(qc, kc, vct, iq, ik, iwt, slopes, tq, ncls):
    B, S, hw = qc.shape
    nq = S // tq
    per = nq // ncls
    ch = tq
    n_keep = min(DSA_TOPK, S // 4)
    kpos = jnp.broadcast_to(jnp.arange(S, dtype=F32)[:, None], (S, LANES))
    full = lambda w: pl.BlockSpec((1, S, w), lambda b, i: (b, 0, 0))
    tile = lambda w: pl.BlockSpec((1, tq, w), lambda b, i: (b, i, 0))
    return pl.pallas_call(
        functools.partial(_dsa_kernel, tq=tq, ch=ch, per=per, ncls=ncls, n_keep=n_keep, slopes=slopes,
                          idx_scale=(IDX_DIM * IDX_HEADS) ** -0.5),
        grid=(B, nq),
        in_specs=[tile(hw), full(hw), pl.BlockSpec((1, hw, S), lambda b, i: (b, 0, 0)),
                  tile(IDX_HEADS * IDX_DIM), full(LANES), pl.BlockSpec((1, LANES, tq), lambda b, i: (b, 0, i)),
                  pl.BlockSpec((S, LANES), lambda b, i: (0, 0))],
        out_specs=tile(hw),
        out_shape=jax.ShapeDtypeStruct((B, S, hw), BF16),
        scratch_shapes=[pltpu.VMEM((S, tq), F32), pltpu.VMEM((S, tq), BF16), pltpu.VMEM((S, tq), F32),
                        pltpu.VMEM((DSA_HEADS, S, tq), F32)],
        compiler_params=_params("arbitrary", "arbitrary"),
        name="dsa_attn",
    )(qc, kc, vct, iq, ik, iwt, kpos)


def _merge_kernel(ya_ref, yb_ref, yc_ref, gt_ref, x_ref, mod_ref, wa_ref, wb_ref, wc_ref, wo_ref, o_ref):
    d = D_MODEL
    merged = gt_ref[0, :, 0:d].astype(F32) * _mm(ya_ref[0], wa_ref[0])
    merged = merged + gt_ref[0, :, d:2 * d].astype(F32) * _mm(yb_ref[0], wb_ref[0])
    merged = merged + gt_ref[0, :, 2 * d:3 * d].astype(F32) * _mm(yc_ref[0], wc_ref[0])
    o_ref[0] = x_ref[0] + mod_ref[0, 2:3, :] * _mm(merged.astype(BF16), wo_ref[0])


def _merge(ya, yb, yc, gt, x, mod, lw, l, tm):
    B, S, D = x.shape
    row = lambda w: pl.BlockSpec((1, tm, w), lambda b, i: (b, i, 0))
    consts = [lw["w_br_a"], lw["w_br_b"], lw["w_br_c"], lw["w_out"]]
    return pl.pallas_call(
        _merge_kernel,
        grid=(B, S // tm),
        in_specs=[row(ya.shape[-1]), row(yb.shape[-1]), row(yc.shape[-1]), row(3 * D), row(D),
                  pl.BlockSpec((1, 6, D), lambda b, i: (b, 0, 0))] + [_layer_spec(c.shape, l) for c in consts],
        out_specs=row(D),
        out_shape=jax.ShapeDtypeStruct((B, S, D), F32),
        compiler_params=_params("arbitrary", "arbitrary"),
        name="merge_out",
    )(ya, yb, yc, gt, x, mod, *consts)


def _ffn_kernel(x_ref, mod_ref, n2_ref, wg_ref, wu_ref, wd_ref, o_ref):
    x = x_ref[0]
    h = (_rms(x, n2_ref[0]) * (1.0 + mod_ref[0, 4:5, :]) + mod_ref[0, 3:4, :]).astype(BF16)
    g = _mm(h, wg_ref[0])
    u = _mm(h, wu_ref[0])
    act = (g * jax.nn.sigmoid(g) * u).astype(BF16)
    o_ref[0] = x + mod_ref[0, 5:6, :] * _mm(act, wd_ref[0])


def _ffn(x, mod, lw, l, tm):
    B, S, D = x.shape
    row = pl.BlockSpec((1, tm, D), lambda b, i: (b, i, 0))
    gu = lw["w_gu"]
    half = (gu.shape[0], gu.shape[1], gu.shape[2] // 2)
    consts = [lw["n2"], gu, gu, lw["w_d"]]
    return pl.pallas_call(
        _ffn_kernel,
        grid=(B, S // tm),
        in_specs=[row, pl.BlockSpec((1, 6, D), lambda b, i: (b, 0, 0)), _layer_spec(lw["n2"].shape, l),
                  _layer_spec(half, l, 0), _layer_spec(half, l, 1), _layer_spec(lw["w_d"].shape, l)],
        out_specs=row,
        out_shape=jax.ShapeDtypeStruct((B, S, D), F32),
        compiler_params=_params("arbitrary", "arbitrary"),
        name="swiglu",
    )(x, mod, *consts)


def _pack_weights(norm1, w_in, g_cq, w_uq, g_ckv, w_ukv, qn_mla, kn_mla, qn_moba, kn_moba, qn_dsa, kn_dsa,
                  w_br_a, w_br_b, w_br_c, w_out, norm2, w_gu, w_down):
    L = w_in.shape[0]
    o = 0
    cols = {}
    for name, wd in (("cq", MLA_Q_RANK), ("ckv", MLA_KV_RANK), ("kr", MLA_ROPE),
                     ("b", 3 * MOBA_HEADS * HEAD_DIM), ("c", 3 * DSA_HEADS * HEAD_DIM),
                     ("iq", IDX_HEADS * IDX_DIM), ("ik", IDX_DIM), ("iw", IDX_HEADS), ("g", 3 * D_MODEL)):
        cols[name] = w_in[:, :, o:o + wd]
        o += wd
    pad_last = lambda v, lo, hi: jnp.pad(v, ((0, 0),) * (v.ndim - 1) + ((lo, hi),))
    kr_slot = pad_last(cols["kr"], MLA_NOPE, LANES - MLA_QK)
    ik_rep = jnp.tile(cols["ik"], (1, 1, LANES // IDX_DIM))
    hw = MOBA_HEADS * HEAD_DIM
    w_all = jnp.concatenate([cols["cq"], cols["ckv"], kr_slot, cols["b"][:, :, :2 * hw], cols["c"][:, :, :2 * hw],
                             cols["iq"], ik_rep, cols["g"]], axis=2).astype(BF16)
    w_t = jnp.concatenate([cols["b"][:, :, 2 * hw:], cols["c"][:, :, 2 * hw:],
                           pad_last(cols["iw"], 0, LANES - IDX_HEADS)], axis=2)
    wuq = pad_last(w_uq.reshape(L, MLA_Q_RANK, MLA_HEADS, MLA_QK), 0, LANES - MLA_QK)
    wukv = w_ukv.reshape(L, MLA_KV_RANK, MLA_HEADS, MLA_NOPE + MLA_V)
    wk = pad_last(wukv[..., :MLA_NOPE], 0, LANES - MLA_NOPE)
    wv = wukv[..., MLA_NOPE:].reshape(L, MLA_KV_RANK, MLA_HEADS * MLA_V)
    row = lambda v: v.reshape(L, 1, -1)
    t = lambda v: jnp.swapaxes(v, 1, 2)
    return {
        "n1": row(norm1), "w_all": w_all, "w_t": t(w_t).astype(BF16), "g_cq": row(g_cq),
        "w_uq": wuq.reshape(L, MLA_Q_RANK, MLA_HEADS * LANES).astype(BF16), "g_ckv": row(g_ckv),
        "w_k": wk.reshape(L, MLA_KV_RANK, MLA_HEADS * LANES).astype(BF16), "w_v": t(wv).astype(BF16),
        "qn_a": row(pad_last(qn_mla * MLA_QK ** -0.5, 0, LANES - MLA_QK)),
        "kn_a": row(pad_last(kn_mla, 0, LANES - MLA_QK)),
        "qn_b": row(jnp.tile(qn_moba * HEAD_DIM ** -0.5, (1, MOBA_HEADS))), "kn_b": row(jnp.tile(kn_moba, (1, MOBA_HEADS))),
        "qn_c": row(jnp.tile(qn_dsa * HEAD_DIM ** -0.5, (1, DSA_HEADS))), "kn_c": row(jnp.tile(kn_dsa, (1, DSA_HEADS))),
        "w_br_a": w_br_a.astype(BF16), "w_br_b": w_br_b.astype(BF16), "w_br_c": w_br_c.astype(BF16),
        "w_out": w_out.astype(BF16), "n2": row(norm2), "w_gu": w_gu.astype(BF16), "w_d": w_down.astype(BF16),
    }


def _rope_tables(S):
    half = MLA_ROPE // 2
    freqs = ROPE_THETA ** (-jnp.arange(half, dtype=F32) / half)
    ang = jnp.arange(S, dtype=F32)[:, None] * freqs[None, :]
    cos, sin = jnp.cos(ang), jnp.sin(ang)
    zeros = lambda n: jnp.zeros((S, n), F32)
    tail = LANES - MLA_QK
    rc = jnp.concatenate([jnp.ones((S, MLA_NOPE), F32), cos, cos, jnp.ones((S, tail), F32)], axis=1)
    rs1 = jnp.concatenate([zeros(MLA_NOPE), -sin, zeros(half + tail)], axis=1)
    rs2 = jnp.concatenate([zeros(MLA_NOPE + half), sin, zeros(tail)], axis=1)
    return rc, rs1, rs2


def _tiles(S):
    dsa_q = min(256, S)
    return {"proj": min(256, S), "out": min(512, S), "mla_q": min(256, S), "dsa_q": dsa_q,
            "dsa_cls": min(4, S // dsa_q)}


def kernel(x, c, w_ada, b_ada, norm1, w_in, g_cq, w_uq, g_ckv, w_ukv, qn_mla, kn_mla, qn_moba, kn_moba, qn_dsa, kn_dsa, w_br_a, w_br_b, w_br_c, w_out, norm2, w_gu, w_down):
    B, S, D = x.shape
    L = w_ada.shape[0]
    t = _tiles(S)
    assert D == D_MODEL and S % MOBA_BLOCK == 0 and all(S % t[k] == 0 for k in ("proj", "out", "mla_q", "dsa_q"))
    n_slopes = MOBA_HEADS + DSA_HEADS
    slopes = [2.0 ** (-8.0 * (i + 1) / n_slopes) for i in range(n_slopes)]
    mod_all = _ada(c, w_ada, b_ada).reshape(L, B, 6, D)
    rope_tabs = _rope_tables(S)
    seg = jnp.arange(MOBA_HEADS * HEAD_DIM) // HEAD_DIM
    bd = (seg[:, None] == seg[None, :]).astype(BF16)
    lw = _pack_weights(norm1, w_in, g_cq, w_uq, g_ckv, w_ukv, qn_mla, kn_mla, qn_moba, kn_moba, qn_dsa, kn_dsa,
                       w_br_a, w_br_b, w_br_c, w_out, norm2, w_gu, w_down)
    for l in range(L):
        mod = mod_all[l]
        qa, ka, va, qb, kb, vb, qc, kc, vc, iq, ik, iw, gt = _inproj(x, mod, lw, l, rope_tabs, bd, t["proj"])
        ya = _mla_attention(qa, ka, va, tq=t["mla_q"])
        yb = _moba_attention(qb, kb, vb, tuple(slopes[0::2]))
        yc = _dsa_attention(qc, kc, vc, iq, ik, iw, tuple(slopes[1::2]), tq=t["dsa_q"], ncls=t["dsa_cls"])
        x = _merge(ya, yb, yc, gt, x, mod, lw, l, t["out"])
        x = _ffn(x, mod, lw, l, t["out"])
    return x
```

```python
import functools

import jax
import jax.numpy as jnp
from jax import lax
from jax.experimental import pallas as pl
from jax.experimental.pallas import tpu as pltpu

F32 = jnp.float32
BF16 = jnp.bfloat16

D_MODEL = 1024
HEAD_DIM = 64
RMS_EPS = 1e-6
MLA_HEADS = 8
MLA_NOPE = 64
MLA_ROPE = 32
MLA_QK = MLA_NOPE + MLA_ROPE
MLA_V = 64
MLA_Q_RANK = 768
MLA_KV_RANK = 256
ROPE_THETA = 10000.0
MOBA_HEADS = 4
MOBA_BLOCK = 256
MOBA_TOPK = 3
DSA_HEADS = 4
IDX_HEADS = 16
IDX_DIM = 32
DSA_TOPK = 256

LANES = 128
VMEM_LIMIT = 56 * 1024 * 1024

OFF_CQ = 0
OFF_CKV = OFF_CQ + MLA_Q_RANK
OFF_KR = OFF_CKV + MLA_KV_RANK
OFF_B = OFF_KR + LANES
OFF_C = OFF_B + 2 * MOBA_HEADS * HEAD_DIM
OFF_IQ = OFF_C + 2 * DSA_HEADS * HEAD_DIM
OFF_IK = OFF_IQ + IDX_HEADS * IDX_DIM
OFF_G = OFF_IK + LANES
N_ALL = OFF_G + 3 * D_MODEL
ROW_VB = 0
ROW_VC = ROW_VB + MOBA_HEADS * HEAD_DIM
ROW_IW = ROW_VC + DSA_HEADS * HEAD_DIM
N_T = ROW_IW + LANES


def _nt(a, b):
    return lax.dot_general(a, b, (((1,), (1,)), ((), ())), preferred_element_type=F32)


def _mm(a, b):
    return jnp.dot(a, b, preferred_element_type=F32)


def _split3(x):
    hi = x.astype(BF16)
    r = x - hi.astype(F32)
    mid = r.astype(BF16)
    lo = (r - mid.astype(F32)).astype(BF16)
    return hi, mid, lo


def _rms(x, g):
    return x * lax.rsqrt(jnp.mean(x * x, axis=-1, keepdims=True) + RMS_EPS) * g


def _const_spec(shape):
    nd = len(shape)
    return pl.BlockSpec(shape, lambda *_: (0,) * nd, pipeline_mode=pl.Buffered(1))


def _layer_spec(shape, l, part=0):
    rest = (0,) * (len(shape) - 2) + (part,)
    return pl.BlockSpec((1,) + tuple(shape[1:]), lambda *_: (l,) + rest, pipeline_mode=pl.Buffered(1))


def _params(*sem):
    return pltpu.CompilerParams(dimension_semantics=sem, vmem_limit_bytes=VMEM_LIMIT)


def _ada_kernel(c_ref, w_ref, b_ref, o_ref):
    c = c_ref[...]
    cond = (c * jax.nn.sigmoid(c)).astype(BF16)
    o_ref[0] = _mm(cond, w_ref[0].astype(BF16)) + b_ref[0]


def _ada(c, w_ada, b_ada):
    L, D, N = w_ada.shape
    B = c.shape[0]
    tn = 1024
    return pl.pallas_call(
        _ada_kernel,
        grid=(L, N // tn),
        in_specs=[
            pl.BlockSpec((B, D), lambda l, n: (0, 0)),
            pl.BlockSpec((1, D, tn), lambda l, n: (l, 0, n)),
            pl.BlockSpec((1, 1, tn), lambda l, n: (l, 0, n)),
        ],
        out_specs=pl.BlockSpec((1, B, tn), lambda l, n: (l, 0, n)),
        out_shape=jax.ShapeDtypeStruct((L, B, N), F32),
        compiler_params=_params("arbitrary", "arbitrary"),
        name="ada_mod",
    )(c, w_ada, b_ada.reshape(L, 1, N))


def _rope(x, c, s1, s2):
    return x * c + pltpu.roll(x, LANES - 16, 1) * s1 + pltpu.roll(x, 16, 1) * s2


def _inproj_kernel(x_ref, mod_ref, n1_ref, w_ref, wt_ref, gcq_ref, wuq_ref, gckv_ref, wk_ref, wv_ref,
                   qna_ref, kna_ref, rc_ref, rs1_ref, rs2_ref, bd_ref,
                   qnb_ref, knb_ref, qnc_ref, knc_ref,
                   qa_ref, ka_ref, va_ref, qb_ref, kb_ref, vb_ref, qc_ref, kc_ref, vc_ref,
                   iq_ref, ik_ref, iw_ref, gt_ref):
    x = x_ref[0]
    sh1 = mod_ref[0, 0:1, :]
    sc1 = mod_ref[0, 1:2, :]
    hb = (_rms(x, n1_ref[0]) * (1.0 + sc1) + sh1).astype(BF16)

    a = _mm(hb, w_ref[0, :, OFF_CQ:OFF_B])
    cqn = _rms(a[:, OFF_CQ:OFF_CKV], gcq_ref[0]).astype(BF16)
    ckvn = _rms(a[:, OFF_CKV:OFF_KR], gckv_ref[0]).astype(BF16)
    kr = a[:, OFF_KR:OFF_B]
    qraw = _mm(cqn, wuq_ref[0])
    kraw = _mm(ckvn, wk_ref[0])
    va_ref[0] = _nt(wv_ref[0], ckvn).astype(BF16)
    rc, rs1, rs2 = rc_ref[...], rs1_ref[...], rs2_ref[...]
    qna, kna = qna_ref[0], kna_ref[0]
    inv_qk = 1.0 / MLA_QK
    for h in range(MLA_HEADS):
        sl = slice(LANES * h, LANES * (h + 1))
        qh = qraw[:, sl]
        qh = qh * lax.rsqrt(jnp.sum(qh * qh, axis=-1, keepdims=True) * inv_qk + RMS_EPS) * qna
        qa_ref[0, :, sl] = _rope(qh, rc, rs1, rs2).astype(BF16)
        kh = kraw[:, sl] + kr
        kh = kh * lax.rsqrt(jnp.sum(kh * kh, axis=-1, keepdims=True) * inv_qk + RMS_EPS) * kna
        ka_ref[0, :, sl] = _rope(kh, rc, rs1, rs2).astype(BF16)
        if h in (1, 4, 7):
            g0 = D_MODEL * (h // 3)
            gt_ref[0, :, g0:g0 + D_MODEL] = jax.nn.sigmoid(
                _mm(hb, w_ref[0, :, OFF_G + g0:OFF_G + g0 + D_MODEL])).astype(BF16)

    bd = bd_ref[...]

    def segnorm(v, g):
        ss = sum(_mm(part, bd) for part in _split3(v * v)[:2])
        return v * lax.rsqrt(ss * (1.0 / HEAD_DIM) + RMS_EPS) * g

    hw = MOBA_HEADS * HEAD_DIM
    pb = _mm(hb, w_ref[0, :, OFF_B:OFF_C])
    qb_ref[0] = segnorm(pb[:, 0:hw], qnb_ref[0]).astype(BF16)
    kb_ref[0] = segnorm(pb[:, hw:2 * hw], knb_ref[0]).astype(BF16)
    pc = _mm(hb, w_ref[0, :, OFF_C:OFF_IQ])
    qc_ref[0] = segnorm(pc[:, 0:hw], qnc_ref[0]).astype(BF16)
    kc_ref[0] = segnorm(pc[:, hw:2 * hw], knc_ref[0]).astype(BF16)

    pt = _nt(wt_ref[0], hb)
    vb_ref[0] = pt[ROW_VB:ROW_VC, :].astype(BF16)
    vc_ref[0] = pt[ROW_VC:ROW_IW, :].astype(BF16)
    iw_ref[0] = pt[ROW_IW:N_T, :]

    pi = _mm(hb, w_ref[0, :, OFF_IQ:OFF_G])
    iq_ref[0] = pi[:, 0:OFF_IK - OFF_IQ].astype(BF16)
    ik_ref[0] = pi[:, OFF_IK - OFF_IQ:OFF_G - OFF_IQ].astype(BF16)


def _inproj(x, mod, lw, l, rope_tabs, bd, tm):
    B, S, D = x.shape
    hw = MOBA_HEADS * HEAD_DIM
    row = lambda w: pl.BlockSpec((1, tm, w), lambda b, i: (b, i, 0))
    tab = pl.BlockSpec((tm, LANES), lambda b, i: (i, 0))
    consts = [lw["n1"], lw["w_all"], lw["w_t"], lw["g_cq"], lw["w_uq"], lw["g_ckv"], lw["w_k"], lw["w_v"],
              lw["qn_a"], lw["kn_a"]]
    consts2 = [lw["qn_b"], lw["kn_b"], lw["qn_c"], lw["kn_c"]]
    widths = [MLA_HEADS * LANES, MLA_HEADS * LANES, MLA_HEADS * MLA_V, hw, hw, hw, hw, hw, hw,
              IDX_HEADS * IDX_DIM, LANES, LANES, 3 * D_MODEL]
    dtypes = [BF16] * 11 + [F32, BF16]
    transposed = [False, False, True] * 3 + [False, False, True, False]
    col = lambda w: pl.BlockSpec((1, w, tm), lambda b, i: (b, 0, i))
    return pl.pallas_call(
        _inproj_kernel,
        grid=(B, S // tm),
        in_specs=[row(D), pl.BlockSpec((1, 6, D), lambda b, i: (b, 0, 0))]
        + [_layer_spec(c.shape, l) for c in consts] + [tab, tab, tab, _const_spec(bd.shape)]
        + [_layer_spec(c.shape, l) for c in consts2],
        out_specs=[col(w) if tr else row(w) for w, tr in zip(widths, transposed)],
        out_shape=[jax.ShapeDtypeStruct((B, w, S) if tr else (B, S, w), dt)
                   for w, dt, tr in zip(widths, dtypes, transposed)],
        compiler_params=_params("arbitrary", "arbitrary"),
        name="in_proj",
    )(x, mod, *consts, *rope_tabs, bd, *consts2)


NEG_INF = float("-inf")


def _fold_rows(x, op, n=8):
    out = x[0:n, :]
    for t in range(1, x.shape[0] // n):
        out = op(out, x[n * t:n * (t + 1), :])
    return out


def _interleave(units):
    pending = []
    for make in units:
        first, second = make()
        for t in range(max(len(first), len(pending))):
            if t < len(first):
                first[t]()
            if t < len(pending):
                pending[t]()
        pending = second
    for step in pending:
        step()


class _Unit:
    def __init__(self, logits, n_chunks, ch, tq, score, vt_chunk, done):
        self.m8 = jnp.full((8, tq), NEG_INF, F32)
        self.l8 = None
        self.acc = None

        def first(j):
            s = score(j)
            logits[j * ch:(j + 1) * ch, :] = s
            self.m8 = jnp.maximum(self.m8, _fold_rows(s, jnp.maximum))

        def second(j):
            if j == 0:
                self.m = jnp.max(self.m8, axis=0, keepdims=True)
            p = jnp.exp(logits[j * ch:(j + 1) * ch, :] - self.m)
            f = _fold_rows(p, jnp.add)
            pv = _mm(vt_chunk(j), p.astype(BF16))
            self.l8 = f if self.l8 is None else self.l8 + f
            self.acc = pv if self.acc is None else self.acc + pv
            if j == n_chunks - 1:
                done(self.acc / jnp.sum(self.l8, axis=0, keepdims=True))

        self.steps = ([functools.partial(first, j) for j in range(n_chunks)],
                      [functools.partial(second, j) for j in range(n_chunks)])


def _mla_kernel(q_ref, k_ref, vt_ref, o_ref, s_ref, *, tq, nq):
    krow = lax.broadcasted_iota(jnp.int32, (tq, tq), 0)
    qcol = lax.broadcasted_iota(jnp.int32, (tq, tq), 1)
    causal = krow <= qcol
    outs = {}

    def unit(c, hh, u):
        sl = slice(LANES * hh, LANES * (hh + 1))
        q = q_ref[0, c * tq:(c + 1) * tq, sl]

        def score(j):
            s = _nt(k_ref[0, j * tq:(j + 1) * tq, sl], q)
            return jnp.where(causal, s, NEG_INF) if j == c else s

        def done(out):
            outs[c, hh] = out
            if hh == 1:
                pair = jnp.concatenate([outs[c, 0], outs[c, 1]], axis=0)
                o_ref[0, c * tq:(c + 1) * tq, :] = pair.T.astype(BF16)

        vt = lambda j: vt_ref[0, MLA_V * hh:MLA_V * (hh + 1), j * tq:(j + 1) * tq]
        return _Unit(s_ref.at[u % 3], c + 1, tq, tq, score, vt, done).steps

    _interleave([functools.partial(unit, c, hh, 2 * c + hh) for c in range(nq) for hh in range(2)])


def _mla_attention(qa, ka, vat, tq):
    B, S, _ = qa.shape
    pairs = MLA_HEADS // 2
    nq = S // tq
    wide = pl.BlockSpec((1, S, 2 * LANES), lambda b, h: (b, 0, h))
    return pl.pallas_call(
        functools.partial(_mla_kernel, tq=tq, nq=nq),
        grid=(B, pairs),
        in_specs=[wide, wide, pl.BlockSpec((1, 2 * MLA_V, S), lambda b, h: (b, h, 0))],
        out_specs=pl.BlockSpec((1, S, 2 * MLA_V), lambda b, h: (b, 0, h)),
        out_shape=jax.ShapeDtypeStruct((B, S, MLA_HEADS * MLA_V), BF16),
        scratch_shapes=[pltpu.VMEM((3, S, tq), F32)],
        compiler_params=_params("arbitrary", "arbitrary"),
        name="mla_attn",
    )(qa, ka, vat)


def _moba_kernel(q_ref, k_ref, vt_ref, o_ref, s_ref, *, nb, nbp, n_sel, slopes):
    blk = MOBA_BLOCK
    hw = MOBA_HEADS * HEAD_DIM
    lane = lax.broadcasted_iota(jnp.int32, (1, hw), 1)

    means = [jnp.mean(k_ref[0, n * blk:(n + 1) * blk, :].astype(F32), axis=0, keepdims=True)
             for n in range(nb)]
    km = jnp.concatenate(means + [jnp.zeros((1, hw), F32)] * (nbp - nb), axis=0)
    pieces = []
    for h in range(MOBA_HEADS):
        hm = (lane >= HEAD_DIM * h) & (lane < HEAD_DIM * (h + 1))
        pieces += [p.astype(F32) for p in _split3(jnp.where(hm, km, 0.0))]
    km_stack = jnp.concatenate(pieces, axis=0).astype(BF16)

    krow = lax.broadcasted_iota(jnp.int32, (blk, blk), 0)
    qcol = lax.broadcasted_iota(jnp.int32, (blk, blk), 1)
    causal = krow <= qcol
    krow_f = krow.astype(F32)
    bidx = lax.broadcasted_iota(jnp.int32, (nbp, blk), 0)

    gates = {}
    outs = {}

    def unit(c, h, u):
        q = q_ref[0, c * blk:(c + 1) * blk, :]
        hm = (lane >= HEAD_DIM * h) & (lane < HEAD_DIM * (h + 1))
        qh = jnp.where(hm, q, jnp.zeros_like(q))
        kbias = slopes[h] * krow_f
        if c > 0:
            if h == 0:
                gates[c] = _nt(km_stack, q)
            past = bidx < c
            g = sum(gates[c][(3 * h + p) * nbp:(3 * h + p + 1) * nbp, :] for p in range(3))
            g = jnp.where(past, g, NEG_INF)
            rank = jnp.zeros((nbp, blk), F32)
            for n2 in range(c):
                gm = g[n2:n2 + 1, :]
                beats = (gm > g) | ((gm == g) & (bidx > n2))
                rank = rank + jnp.where(beats, 1.0, 0.0)
            drop = jnp.where(past & (rank < n_sel), 0.0, NEG_INF)

        def score(j):
            s = _nt(k_ref[0, j * blk:(j + 1) * blk, :], qh) + kbias
            if j == c:
                return jnp.where(causal, s, NEG_INF)
            return s + (drop[j:j + 1, :] + slopes[h] * float((j - c) * blk))

        def done(out):
            outs[c, h] = out
            if h == MOBA_HEADS - 1:
                full = jnp.concatenate([outs[c, hh] for hh in range(MOBA_HEADS)], axis=0)
                o_ref[0, c * blk:(c + 1) * blk, :] = full.T.astype(BF16)

        vt = lambda j: vt_ref[0, HEAD_DIM * h:HEAD_DIM * (h + 1), j * blk:(j + 1) * blk]
        return _Unit(s_ref.at[u % 3], c + 1, blk, blk, score, vt, done).steps

    _interleave([functools.partial(unit, c, h, MOBA_HEADS * c + h) for c in range(nb) for h in range(MOBA_HEADS)])


def _moba_attention(qb, kb, vbt, slopes):
    B, S, hw = qb.shape
    blk = MOBA_BLOCK
    nb = S // blk
    nbp = -(-nb // 8) * 8
    n_sel = max(1, min(MOBA_TOPK, nb - 1))
    full = pl.BlockSpec((1, S, hw), lambda b: (b, 0, 0))
    return pl.pallas_call(
        functools.partial(_moba_kernel, nb=nb, nbp=nbp, n_sel=n_sel, slopes=slopes),
        grid=(B,),
        in_specs=[full, full, pl.BlockSpec((1, hw, S), lambda b: (b, 0, 0))],
        out_specs=full,
        out_shape=jax.ShapeDtypeStruct((B, S, hw), BF16),
        scratch_shapes=[pltpu.VMEM((3, S, blk), F32)],
        compiler_params=_params("arbitrary"),
        name="moba_attn",
    )(qb, kb, vbt)


KEY_POS_INF = 0x7F800000
KEY_NEG_INF = -0x7F800001


def _key_to_float(key):
    key = jnp.clip(key, KEY_NEG_INF, KEY_POS_INF)
    return lax.bitcast_convert_type(jnp.where(key >= 0, key, key ^ jnp.int32(0x7FFFFFFF)), F32)


def _kth_largest(sc_ref, sb_ref, rows, n_chunks, tq, k):
    i32 = jnp.int32

    def count(ref, c, one, zero, n):
        acc = None
        for j in range(n_chunks):
            f = _fold_rows(jnp.where(ref[rows(j), :] >= c, one, zero), jnp.add, n).astype(F32)
            acc = f if acc is None else acc + f
        return jnp.sum(acc, axis=0, keepdims=True)

    count16 = lambda c: count(sb_ref, c.astype(BF16), jnp.bfloat16(1), jnp.bfloat16(0), 16)
    count32 = lambda c: count(sc_ref, c, 1.0, 0.0, 8)
    grid_key = lambda k16: lax.shift_left(k16, 16) | jnp.where(k16 < 0, i32(0xFFFF), i32(0))

    zero = jnp.zeros((1, tq), i32)
    k16 = jnp.where(count16(_key_to_float(grid_key(zero))) >= k, zero, i32(-2 ** 15))

    def coarse(it, k16):
        cand = k16 | lax.shift_left(i32(1), i32(14) - it)
        return jnp.where(count16(_key_to_float(grid_key(cand))) >= k, cand, k16)

    g = grid_key(lax.fori_loop(0, 15, coarse, k16))
    bracket = (g - i32(0x8000), g + i32(0x10001))

    def fine(_, lo_hi):
        lo, hi = lo_hi
        mid = lo + lax.shift_right_arithmetic(hi - lo, 1)
        ok = count32(_key_to_float(mid)) >= k
        return jnp.where(ok, mid, lo), jnp.where(ok, hi, mid)

    lo, _ = lax.fori_loop(0, 17, fine, bracket)
    return _key_to_float(lo)


def _dsa_kernel(q_ref, k_ref, vt_ref, iq_ref, ik_ref, iwt_ref, kpos_ref, o_ref, sc_ref, sb_ref, mb_ref, s_ref, *,
                tq, ch, per, ncls, n_keep, slopes, idx_scale):
    i = pl.program_id(1)
    hw = DSA_HEADS * HEAD_DIM
    q0 = i * tq
    qpos = q0 + lax.broadcasted_iota(jnp.int32, (1, tq), 1)
    krow = lax.broadcasted_iota(jnp.int32, (ch, 1), 0)
    lane = lax.broadcasted_iota(jnp.int32, (1, LANES), 1)
    lane_h = lax.broadcasted_iota(jnp.int32, (1, hw), 1)
    grp = LANES // IDX_DIM

    def body(c):
        n_chunks = (c + 1) * per * tq // ch
        rows = lambda j: slice(j * ch, (j + 1) * ch)
        iwt = iwt_ref[0]
        qms = []
        for hh in range(IDX_HEADS):
            g, r = divmod(hh, grp)
            iqg = iq_ref[0, :, LANES * g:LANES * (g + 1)]
            qms.append(jnp.where((lane >= IDX_DIM * r) & (lane < IDX_DIM * (r + 1)), iqg, jnp.zeros_like(iqg)))
        for j in range(n_chunks):
            ikc = ik_ref[0, rows(j), :]
            score = None
            for hh in range(IDX_HEADS):
                term = jnp.maximum(_nt(ikc, qms[hh]), 0.0) * iwt[hh:hh + 1, :]
                score = term if score is None else score + term
            score = jnp.where(krow + j * ch <= qpos, score * idx_scale, NEG_INF)
            sc_ref[rows(j), :] = score
            sb_ref[rows(j), :] = score.astype(BF16)

        def mask_at_least(thr):
            for j in range(n_chunks):
                keep = (sc_ref[rows(j), :] >= thr) & (krow + j * ch <= qpos)
                mb_ref[rows(j), :] = jnp.where(keep, 0.0, NEG_INF)

        def count(pred):
            acc = None
            for j in range(n_chunks):
                f = _fold_rows(jnp.where(pred(sc_ref[rows(j), :]), 1.0, 0.0), jnp.add)
                acc = f if acc is None else acc + f
            return jnp.sum(acc, axis=0, keepdims=True)

        keep_all = functools.partial(mask_at_least, jnp.full((1, tq), NEG_INF, F32))

        def select_and_mask():
            thr = _kth_largest(sc_ref, sb_ref, rows, n_chunks, tq, float(n_keep))
            tied = jnp.max(count(lambda v: v >= thr)) > float(n_keep)
            pl.when(jnp.logical_not(tied))(functools.partial(mask_at_least, thr))

            @pl.when(tied)
            def _():
                need = float(n_keep) - count(lambda v: v > thr)
                tri = jnp.where(lax.broadcasted_iota(jnp.int32, (ch, ch), 1)
                                <= lax.broadcasted_iota(jnp.int32, (ch, ch), 0), 1.0, 0.0).astype(BF16)

                def chunk(j, seen):
                    at = pl.ds(pl.multiple_of(j * ch, ch), ch)
                    sc = sc_ref[at, :]
                    eq = jnp.where(sc == thr, 1.0, 0.0)
                    rank = _mm(tri, eq.astype(BF16)) + seen
                    keep = (sc > thr) | ((sc == thr) & (rank <= need))
                    keep = keep & (krow + j * ch <= qpos)
                    mb_ref[at, :] = jnp.where(keep, 0.0, NEG_INF)
                    return seen + jnp.sum(eq, axis=0, keepdims=True)

                lax.fori_loop(0, n_chunks, chunk, jnp.zeros((1, tq), F32))

        if n_chunks * ch <= n_keep:
            keep_all()
        elif c * per * tq < n_keep:
            few = q0 + tq <= n_keep
            pl.when(few)(keep_all)
            pl.when(jnp.logical_not(few))(select_and_mask)
        else:
            select_and_mask()

        q = q_ref[0]
        q0f = q0.astype(F32)
        outs = {}

        def unit(h):
            hm = (lane_h >= HEAD_DIM * h) & (lane_h < HEAD_DIM * (h + 1))
            qh = jnp.where(hm, q, jnp.zeros_like(q))

            def score(j):
                kb = slopes[h] * (kpos_ref[rows(j), :] - q0f)
                kb = jnp.concatenate([kb] * (tq // LANES), axis=1)
                return _nt(k_ref[0, rows(j), :], qh) + kb + mb_ref[rows(j), :]

            def done(out):
                outs[h] = out
                if h == DSA_HEADS - 1:
                    full = jnp.concatenate([outs[hh] for hh in range(DSA_HEADS)], axis=0)
                    o_ref[0] = full.T.astype(BF16)

            vt = lambda j: vt_ref[0, HEAD_DIM * h:HEAD_DIM * (h + 1), rows(j)]
            return _Unit(s_ref.at[h], n_chunks, ch, tq, score, vt, done).steps

        _interleave([functools.partial(unit, h) for h in range(DSA_HEADS)])

    for c in range(ncls):
        pl.when((i >= c * per) & (i < (c + 1) * per))(functools.partial(body, c))


def _dsa_attention(qc, kc, vct, iq, ik, iwt, slopes, tq, ncls):
    B, S, hw = qc.shape
    nq = S // tq
    per = nq // ncls
    ch = tq
    n_keep = min(DSA_TOPK, S // 4)
    kpos = jnp.broadcast_to(jnp.arange(S, dtype=F32)[:, None], (S, LANES))
    full = lambda w: pl.BlockSpec((1, S, w), lambda b, i: (b, 0, 0))
    tile = lambda w: pl.BlockSpec((1, tq, w), lambda b, i: (b, i, 0))
    return pl.pallas_call(
        functools.partial(_dsa_kernel, tq=tq, ch=ch, per=per, ncls=ncls, n_keep=n_keep, slopes=slopes,
                          idx_scale=(IDX_DIM * IDX_HEADS) ** -0.5),
        grid=(B, nq),
        in_specs=[tile(hw), full(hw), pl.BlockSpec((1, hw, S), lambda b, i: (b, 0, 0)),
                  tile(IDX_HEADS * IDX_DIM), full(LANES), pl.BlockSpec((1, LANES, tq), lambda b, i: (b, 0, i)),
                  pl.BlockSpec((S, LANES), lambda b, i: (0, 0))],
        out_specs=tile(hw),
        out_shape=jax.ShapeDtypeStruct((B, S, hw), BF16),
        scratch_shapes=[pltpu.VMEM((S, tq), F32), pltpu.VMEM((S, tq), BF16), pltpu.VMEM((S, tq), F32),
                        pltpu.VMEM((DSA_HEADS, S, tq), F32)],
        compiler_params=_params("arbitrary", "arbitrary"),
        name="dsa_attn",
    )(qc, kc, vct, iq, ik, iwt, kpos)


def _merge_kernel(ya_ref, yb_ref, yc_ref, gt_ref, x_ref, mod_ref, wa_ref, wb_ref, wc_ref, wo_ref, o_ref):
    d = D_MODEL
    merged = gt_ref[0, :, 0:d].astype(F32) * _mm(ya_ref[0], wa_ref[0])
    merged = merged + gt_ref[0, :, d:2 * d].astype(F32) * _mm(yb_ref[0], wb_ref[0])
    merged = merged + gt_ref[0, :, 2 * d:3 * d].astype(F32) * _mm(yc_ref[0], wc_ref[0])
    o_ref[0] = x_ref[0] + mod_ref[0, 2:3, :] * _mm(merged.astype(BF16), wo_ref[0])


def _merge(ya, yb, yc, gt, x, mod, lw, l, tm):
    B, S, D = x.shape
    row = lambda w: pl.BlockSpec((1, tm, w), lambda b, i: (b, i, 0))
    consts = [lw["w_br_a"], lw["w_br_b"], lw["w_br_c"], lw["w_out"]]
    return pl.pallas_call(
        _merge_kernel,
        grid=(B, S // tm),
        in_specs=[row(ya.shape[-1]), row(yb.shape[-1]), row(yc.shape[-1]), row(3 * D), row(D),
                  pl.BlockSpec((1, 6, D), lambda b, i: (b, 0, 0))] + [_layer_spec(c.shape, l) for c in consts],
        out_specs=row(D),
        out_shape=jax.ShapeDtypeStruct((B, S, D), F32),
        compiler_params=_params("arbitrary", "arbitrary"),
        name="merge_out",
    )(ya, yb, yc, gt, x, mod, *consts)


def _ffn_kernel(x_ref, mod_ref, n2_ref, wg_ref, wu_ref, wd_ref, o_ref):
    x = x_ref[0]
    h = (_rms(x, n2_ref[0]) * (1.0 + mod_ref[0, 4:5, :]) + mod_ref[0, 3:4, :]).astype(BF16)
    g = _mm(h, wg_ref[0])
    u = _mm(h, wu_ref[0])
    act = (g * jax.nn.sigmoid(g) * u).astype(BF16)
    o_ref[0] = x + mod_ref[0, 5:6, :] * _mm(act, wd_ref[0])


def _ffn(x, mod, lw, l, tm):
    B, S, D = x.shape
    row = pl.BlockSpec((1, tm, D), lambda b, i: (b, i, 0))
    gu = lw["w_gu"]
    half = (gu.shape[0], gu.shape[1], gu.shape[2] // 2)
    consts = [lw["n2"], gu, gu, lw["w_d"]]
    return pl.pallas_call(
        _ffn_kernel,
        grid=(B, S // tm),
        in_specs=[row, pl.BlockSpec((1, 6, D), lambda b, i: (b, 0, 0)), _layer_spec(lw["n2"].shape, l),
                  _layer_spec(half, l, 0), _layer_spec(half, l, 1), _layer_spec(lw["w_d"].shape, l)],
        out_specs=row,
        out_shape=jax.ShapeDtypeStruct((B, S, D), F32),
        compiler_params=_params("arbitrary", "arbitrary"),
        name="swiglu",
    )(x, mod, *consts)


def _pack_weights(norm1, w_in, g_cq, w_uq, g_ckv, w_ukv, qn_mla, kn_mla, qn_moba, kn_moba, qn_dsa, kn_dsa,
                  w_br_a, w_br_b, w_br_c, w_out, norm2, w_gu, w_down):
    L = w_in.shape[0]
    o = 0
    cols = {}
    for name, wd in (("cq", MLA_Q_RANK), ("ckv", MLA_KV_RANK), ("kr", MLA_ROPE),
                     ("b", 3 * MOBA_HEADS * HEAD_DIM), ("c", 3 * DSA_HEADS * HEAD_DIM),
                     ("iq", IDX_HEADS * IDX_DIM), ("ik", IDX_DIM), ("iw", IDX_HEADS), ("g", 3 * D_MODEL)):
        cols[name] = w_in[:, :, o:o + wd]
        o += wd
    pad_last = lambda v, lo, hi: jnp.pad(v, ((0, 0),) * (v.ndim - 1) + ((lo, hi),))
    kr_slot = pad_last(cols["kr"], MLA_NOPE, LANES - MLA_QK)
    ik_rep = jnp.tile(cols["ik"], (1, 1, LANES // IDX_DIM))
    hw = MOBA_HEADS * HEAD_DIM
    w_all = jnp.concatenate([cols["cq"], cols["ckv"], kr_slot, cols["b"][:, :, :2 * hw], cols["c"][:, :, :2 * hw],
                             cols["iq"], ik_rep, cols["g"]], axis=2).astype(BF16)
    assert w_all.shape[-1] == N_ALL
    w_t = jnp.concatenate([cols["b"][:, :, 2 * hw:], cols["c"][:, :, 2 * hw:],
                           pad_last(cols["iw"], 0, LANES - IDX_HEADS)], axis=2)
    wuq = pad_last(w_uq.reshape(L, MLA_Q_RANK, MLA_HEADS, MLA_QK), 0, LANES - MLA_QK)
    wukv = w_ukv.reshape(L, MLA_KV_RANK, MLA_HEADS, MLA_NOPE + MLA_V)
    wk = pad_last(wukv[..., :MLA_NOPE], 0, LANES - MLA_NOPE)
    wv = wukv[..., MLA_NOPE:].reshape(L, MLA_KV_RANK, MLA_HEADS * MLA_V)
    row = lambda v: v.reshape(L, 1, -1)
    t = lambda v: jnp.swapaxes(v, 1, 2)
    return {
        "n1": row(norm1), "w_all": w_all, "w_t": t(w_t).astype(BF16), "g_cq": row(g_cq),
        "w_uq": wuq.reshape(L, MLA_Q_RANK, MLA_HEADS * LANES).astype(BF16), "g_ckv": row(g_ckv),
        "w_k": wk.reshape(L, MLA_KV_RANK, MLA_HEADS * LANES).astype(BF16), "w_v": t(wv).astype(BF16),
        "qn_a": row(pad_last(qn_mla * MLA_QK ** -0.5, 0, LANES - MLA_QK)),
        "kn_a": row(pad_last(kn_mla, 0, LANES - MLA_QK)),
        "qn_b": row(jnp.tile(qn_moba * HEAD_DIM ** -0.5, (1, MOBA_HEADS))), "kn_b": row(jnp.tile(kn_moba, (1, MOBA_HEADS))),
        "qn_c": row(jnp.tile(qn_dsa * HEAD_DIM ** -0.5, (1, DSA_HEADS))), "kn_c": row(jnp.tile(kn_dsa, (1, DSA_HEADS))),
        "w_br_a": w_br_a.astype(BF16), "w_br_b": w_br_b.astype(BF16), "w_br_c": w_br_c.astype(BF16),
        "w_out": w_out.astype(BF16), "n2": row(norm2), "w_gu": w_gu.astype(BF16), "w_d": w_down.astype(BF16),
    }


def _rope_tables(S):
    half = MLA_ROPE // 2
    freqs = ROPE_THETA ** (-jnp.arange(half, dtype=F32) / half)
    ang = jnp.arange(S, dtype=F32)[:, None] * freqs[None, :]
    cos, sin = jnp.cos(ang), jnp.sin(ang)
    zeros = lambda n: jnp.zeros((S, n), F32)
    tail = LANES - MLA_QK
    rc = jnp.concatenate([jnp.ones((S, MLA_NOPE), F32), cos, cos, jnp.ones((S, tail), F32)], axis=1)
    rs1 = jnp.concatenate([zeros(MLA_NOPE), -sin, zeros(half + tail)], axis=1)
    rs2 = jnp.concatenate([zeros(MLA_NOPE + half), sin, zeros(tail)], axis=1)
    return rc, rs1, rs2


def _tiles(S):
    dsa_q = min(256, S)
    return {"proj": min(256, S), "out": min(512, S), "mla_q": min(512, S), "dsa_q": dsa_q,
            "dsa_cls": min(4, S // dsa_q)}


def kernel(x, c, w_ada, b_ada, norm1, w_in, g_cq, w_uq, g_ckv, w_ukv, qn_mla, kn_mla, qn_moba, kn_moba, qn_dsa, kn_dsa, w_br_a, w_br_b, w_br_c, w_out, norm2, w_gu, w_down):
    B, S, D = x.shape
    L = w_ada.shape[0]
    t = _tiles(S)
    assert D == D_MODEL and S % MOBA_BLOCK == 0 and all(S % t[k] == 0 for k in ("proj", "out", "mla_q", "dsa_q"))
    n_slopes = MOBA_HEADS + DSA_HEADS
    slopes = [2.0 ** (-8.0 * (i + 1) / n_slopes) for i in range(n_slopes)]
    mod_all = _ada(c, w_ada, b_ada).reshape(L, B, 6, D)
    rope_tabs = _rope_tables(S)
    seg = jnp.arange(MOBA_HEADS * HEAD_DIM) // HEAD_DIM
    bd = (seg[:, None] == seg[None, :]).astype(BF16)
    lw = _pack_weights(norm1, w_in, g_cq, w_uq, g_ckv, w_ukv, qn_mla, kn_mla, qn_moba, kn_moba, qn_dsa, kn_dsa,
                       w_br_a, w_br_b, w_br_c, w_out, norm2, w_gu, w_down)
    for l in range(L):
        mod = mod_all[l]
        qa, ka, va, qb, kb, vb, qc, kc, vc, iq, ik, iw, gt = _inproj(x, mod, lw, l, rope_tabs, bd, t["proj"])
        ya = _mla_attention(qa, ka, va, tq=t["mla_q"])
        yb = _moba_attention(qb, kb, vb, tuple(slopes[0::2]))
        yc = _dsa_attention(qc, kc, vc, iq, ik, iw, tuple(slopes[1::2]), tq=t["dsa_q"], ncls=t["dsa_cls"])
        x = _merge(ya, yb, yc, gt, x, mod, lw, l, t["out"])
        x = _ffn(x, mod, lw, l, t["out"])
    return x
```

```python
import functools

import jax
import jax.numpy as jnp
from jax import lax
from jax.experimental import pallas as pl
from jax.experimental.pallas import tpu as pltpu

F32 = jnp.float32
BF16 = jnp.bfloat16

D_MODEL = 1024
HEAD_DIM = 64
RMS_EPS = 1e-6
MLA_HEADS = 8
MLA_NOPE = 64
MLA_ROPE = 32
MLA_QK = MLA_NOPE + MLA_ROPE
MLA_V = 64
MLA_Q_RANK = 768
MLA_KV_RANK = 256
ROPE_THETA = 10000.0
MOBA_HEADS = 4
MOBA_BLOCK = 256
MOBA_TOPK = 3
DSA_HEADS = 4
IDX_HEADS = 16
IDX_DIM = 32
DSA_TOPK = 256

LANES = 128
VMEM_LIMIT = 56 * 1024 * 1024

OFF_CQ = 0
OFF_CKV = OFF_CQ + MLA_Q_RANK
OFF_KR = OFF_CKV + MLA_KV_RANK
OFF_B = OFF_KR + LANES
OFF_C = OFF_B + 2 * MOBA_HEADS * HEAD_DIM
OFF_IQ = OFF_C + 2 * DSA_HEADS * HEAD_DIM
OFF_IK = OFF_IQ + IDX_HEADS * IDX_DIM
OFF_G = OFF_IK + LANES
N_ALL = OFF_G + 3 * D_MODEL
ROW_VB = 0
ROW_VC = ROW_VB + MOBA_HEADS * HEAD_DIM
ROW_IW = ROW_VC + DSA_HEADS * HEAD_DIM
N_T = ROW_IW + LANES


def _nt(a, b):
    return lax.dot_general(a, b, (((1,), (1,)), ((), ())), preferred_element_type=F32)


def _mm(a, b):
    return jnp.dot(a, b, preferred_element_type=F32)


def _split3(x):
    hi = x.astype(BF16)
    r = x - hi.astype(F32)
    mid = r.astype(BF16)
    lo = (r - mid.astype(F32)).astype(BF16)
    return hi, mid, lo


def _rms(x, g):
    return x * lax.rsqrt(jnp.mean(x * x, axis=-1, keepdims=True) + RMS_EPS) * g


def _const_spec(shape):
    nd = len(shape)
    return pl.BlockSpec(shape, lambda *_: (0,) * nd, pipeline_mode=pl.Buffered(1))


def _layer_spec(shape, l, part=0):
    rest = (0,) * (len(shape) - 2) + (part,)
    return pl.BlockSpec((1,) + tuple(shape[1:]), lambda *_: (l,) + rest, pipeline_mode=pl.Buffered(1))


def _params(*sem):
    return pltpu.CompilerParams(dimension_semantics=sem, vmem_limit_bytes=VMEM_LIMIT)


def _ada_kernel(c_ref, w_ref, b_ref, o_ref):
    c = c_ref[...]
    cond = (c * jax.nn.sigmoid(c)).astype(BF16)
    o_ref[0] = _mm(cond, w_ref[0].astype(BF16)) + b_ref[0]


def _ada(c, w_ada, b_ada):
    L, D, N = w_ada.shape
    B = c.shape[0]
    tn = 1024
    return pl.pallas_call(
        _ada_kernel,
        grid=(L, N // tn),
        in_specs=[
            pl.BlockSpec((B, D), lambda l, n: (0, 0)),
            pl.BlockSpec((1, D, tn), lambda l, n: (l, 0, n)),
            pl.BlockSpec((1, 1, tn), lambda l, n: (l, 0, n)),
        ],
        out_specs=pl.BlockSpec((1, B, tn), lambda l, n: (l, 0, n)),
        out_shape=jax.ShapeDtypeStruct((L, B, N), F32),
        compiler_params=_params("arbitrary", "arbitrary"),
        name="ada_mod",
    )(c, w_ada, b_ada.reshape(L, 1, N))


def _rope(x, c, s1, s2):
    return x * c + pltpu.roll(x, LANES - 16, 1) * s1 + pltpu.roll(x, 16, 1) * s2


def _inproj_kernel(x_ref, mod_ref, n1_ref, w_ref, wt_ref, gcq_ref, wuq_ref, gckv_ref, wk_ref, wv_ref,
                   qna_ref, kna_ref, rc_ref, rs1_ref, rs2_ref, bd_ref,
                   qnb_ref, knb_ref, qnc_ref, knc_ref,
                   qa_ref, ka_ref, va_ref, qb_ref, kb_ref, vb_ref, qc_ref, kc_ref, vc_ref,
                   iq_ref, ik_ref, iw_ref, gt_ref):
    x = x_ref[0]
    sh1 = mod_ref[0, 0:1, :]
    sc1 = mod_ref[0, 1:2, :]
    hb = (_rms(x, n1_ref[0]) * (1.0 + sc1) + sh1).astype(BF16)

    a = _mm(hb, w_ref[0, :, OFF_CQ:OFF_B])
    cqn = _rms(a[:, OFF_CQ:OFF_CKV], gcq_ref[0]).astype(BF16)
    ckvn = _rms(a[:, OFF_CKV:OFF_KR], gckv_ref[0]).astype(BF16)
    kr = a[:, OFF_KR:OFF_B]
    qraw = _mm(cqn, wuq_ref[0])
    kraw = _mm(ckvn, wk_ref[0])
    va_ref[0] = _nt(wv_ref[0], ckvn).astype(BF16)
    rc, rs1, rs2 = rc_ref[...], rs1_ref[...], rs2_ref[...]
    qna, kna = qna_ref[0], kna_ref[0]
    inv_qk = 1.0 / MLA_QK
    for h in range(MLA_HEADS):
        sl = slice(LANES * h, LANES * (h + 1))
        qh = qraw[:, sl]
        qh = qh * lax.rsqrt(jnp.sum(qh * qh, axis=-1, keepdims=True) * inv_qk + RMS_EPS) * qna
        qa_ref[0, :, sl] = _rope(qh, rc, rs1, rs2).astype(BF16)
        kh = kraw[:, sl] + kr
        kh = kh * lax.rsqrt(jnp.sum(kh * kh, axis=-1, keepdims=True) * inv_qk + RMS_EPS) * kna
        ka_ref[0, :, sl] = _rope(kh, rc, rs1, rs2).astype(BF16)
        if h in (1, 4, 7):
            g0 = D_MODEL * (h // 3)
            gt_ref[0, :, g0:g0 + D_MODEL] = jax.nn.sigmoid(
                _mm(hb, w_ref[0, :, OFF_G + g0:OFF_G + g0 + D_MODEL])).astype(BF16)

    bd = bd_ref[...]

    def segnorm(v, g):
        ss = sum(_mm(part, bd) for part in _split3(v * v)[:2])
        return v * lax.rsqrt(ss * (1.0 / HEAD_DIM) + RMS_EPS) * g

    hw = MOBA_HEADS * HEAD_DIM
    pb = _mm(hb, w_ref[0, :, OFF_B:OFF_C])
    qb_ref[0] = segnorm(pb[:, 0:hw], qnb_ref[0]).astype(BF16)
    kb_ref[0] = segnorm(pb[:, hw:2 * hw], knb_ref[0]).astype(BF16)
    pc = _mm(hb, w_ref[0, :, OFF_C:OFF_IQ])
    qc_ref[0] = segnorm(pc[:, 0:hw], qnc_ref[0]).astype(BF16)
    kc_ref[0] = segnorm(pc[:, hw:2 * hw], knc_ref[0]).astype(BF16)

    pt = _nt(wt_ref[0], hb)
    vb_ref[0] = pt[ROW_VB:ROW_VC, :].astype(BF16)
    vc_ref[0] = pt[ROW_VC:ROW_IW, :].astype(BF16)
    iw_ref[0] = pt[ROW_IW:N_T, :]

    pi = _mm(hb, w_ref[0, :, OFF_IQ:OFF_G])
    iq_ref[0] = pi[:, 0:OFF_IK - OFF_IQ].astype(BF16)
    ik_ref[0] = pi[:, OFF_IK - OFF_IQ:OFF_G - OFF_IQ].astype(BF16)


def _inproj(x, mod, lw, l, rope_tabs, bd, tm):
    B, S, D = x.shape
    hw = MOBA_HEADS * HEAD_DIM
    row = lambda w: pl.BlockSpec((1, tm, w), lambda b, i: (b, i, 0))
    tab = pl.BlockSpec((tm, LANES), lambda b, i: (i, 0))
    consts = [lw["n1"], lw["w_all"], lw["w_t"], lw["g_cq"], lw["w_uq"], lw["g_ckv"], lw["w_k"], lw["w_v"],
              lw["qn_a"], lw["kn_a"]]
    consts2 = [lw["qn_b"], lw["kn_b"], lw["qn_c"], lw["kn_c"]]
    widths = [MLA_HEADS * LANES, MLA_HEADS * LANES, MLA_HEADS * MLA_V, hw, hw, hw, hw, hw, hw,
              IDX_HEADS * IDX_DIM, LANES, LANES, 3 * D_MODEL]
    dtypes = [BF16] * 11 + [F32, BF16]
    transposed = [False, False, True] * 3 + [False, False, True, False]
    col = lambda w: pl.BlockSpec((1, w, tm), lambda b, i: (b, 0, i))
    return pl.pallas_call(
        _inproj_kernel,
        grid=(B, S // tm),
        in_specs=[row(D), pl.BlockSpec((1, 6, D), lambda b, i: (b, 0, 0))]
        + [_layer_spec(c.shape, l) for c in consts] + [tab, tab, tab, _const_spec(bd.shape)]
        + [_layer_spec(c.shape, l) for c in consts2],
        out_specs=[col(w) if tr else row(w) for w, tr in zip(widths, transposed)],
        out_shape=[jax.ShapeDtypeStruct((B, w, S) if tr else (B, S, w), dt)
                   for w, dt, tr in zip(widths, dtypes, transposed)],
        compiler_params=_params("arbitrary", "arbitrary"),
        name="in_proj",
    )(x, mod, *consts, *rope_tabs, bd, *consts2)


NEG_INF = float("-inf")


def _fold_rows(x, op, n=8):
    out = x[0:n, :]
    for t in range(1, x.shape[0] // n):
        out = op(out, x[n * t:n * (t + 1), :])
    return out


def _interleave(units):
    pending = []
    for make in units:
        first, second = make()
        for t in range(max(len(first), len(pending))):
            if t < len(first):
                first[t]()
            if t < len(pending):
                pending[t]()
        pending = second
    for step in pending:
        step()


class _Unit:
    def __init__(self, logits, n_chunks, ch, tq, score, vt_chunk, done):
        self.m8 = jnp.full((8, tq), NEG_INF, F32)
        self.l8 = None
        self.acc = None

        def first(j):
            s = score(j)
            logits[j * ch:(j + 1) * ch, :] = s
            self.m8 = jnp.maximum(self.m8, _fold_rows(s, jnp.maximum))

        def second(j):
            if j == 0:
                self.m = jnp.max(self.m8, axis=0, keepdims=True)
            p = jnp.exp(logits[j * ch:(j + 1) * ch, :] - self.m)
            f = _fold_rows(p, jnp.add)
            pv = _mm(vt_chunk(j), p.astype(BF16))
            self.l8 = f if self.l8 is None else self.l8 + f
            self.acc = pv if self.acc is None else self.acc + pv
            if j == n_chunks - 1:
                done(self.acc / jnp.sum(self.l8, axis=0, keepdims=True))

        self.steps = ([functools.partial(first, j) for j in range(n_chunks)],
                      [functools.partial(second, j) for j in range(n_chunks)])


def _mla_kernel(q_ref, k_ref, vt_ref, o_ref, s_ref, *, tq, nq):
    krow = lax.broadcasted_iota(jnp.int32, (tq, tq), 0)
    qcol = lax.broadcasted_iota(jnp.int32, (tq, tq), 1)
    causal = krow <= qcol
    outs = {}

    def unit(c, hh, u):
        sl = slice(LANES * hh, LANES * (hh + 1))
        q = q_ref[0, c * tq:(c + 1) * tq, sl]

        def score(j):
            s = _nt(k_ref[0, j * tq:(j + 1) * tq, sl], q)
            return jnp.where(causal, s, NEG_INF) if j == c else s

        def done(out):
            outs[c, hh] = out
            if hh == 1:
                pair = jnp.concatenate([outs[c, 0], outs[c, 1]], axis=0)
                o_ref[0, c * tq:(c + 1) * tq, :] = pair.T.astype(BF16)

        vt = lambda j: vt_ref[0, MLA_V * hh:MLA_V * (hh + 1), j * tq:(j + 1) * tq]
        return _Unit(s_ref.at[u % 3], c + 1, tq, tq, score, vt, done).steps

    _interleave([functools.partial(unit, c, hh, 2 * c + hh) for c in range(nq) for hh in range(2)])


def _mla_attention(qa, ka, vat, tq):
    B, S, _ = qa.shape
    pairs = MLA_HEADS // 2
    nq = S // tq
    wide = pl.BlockSpec((1, S, 2 * LANES), lambda b, h: (b, 0, h))
    return pl.pallas_call(
        functools.partial(_mla_kernel, tq=tq, nq=nq),
        grid=(B, pairs),
        in_specs=[wide, wide, pl.BlockSpec((1, 2 * MLA_V, S), lambda b, h: (b, h, 0))],
        out_specs=pl.BlockSpec((1, S, 2 * MLA_V), lambda b, h: (b, 0, h)),
        out_shape=jax.ShapeDtypeStruct((B, S, MLA_HEADS * MLA_V), BF16),
        scratch_shapes=[pltpu.VMEM((3, S, tq), F32)],
        compiler_params=_params("arbitrary", "arbitrary"),
        name="mla_attn",
    )(qa, ka, vat)


def _moba_kernel(q_ref, k_ref, vt_ref, o_ref, s_ref, *, nb, nbp, n_sel, slopes):
    blk = MOBA_BLOCK
    hw = MOBA_HEADS * HEAD_DIM
    lane = lax.broadcasted_iota(jnp.int32, (1, hw), 1)

    means = [jnp.mean(k_ref[0, n * blk:(n + 1) * blk, :].astype(F32), axis=0, keepdims=True)
             for n in range(nb)]
    km = jnp.concatenate(means + [jnp.zeros((1, hw), F32)] * (nbp - nb), axis=0)
    pieces = []
    for h in range(MOBA_HEADS):
        hm = (lane >= HEAD_DIM * h) & (lane < HEAD_DIM * (h + 1))
        pieces += [p.astype(F32) for p in _split3(jnp.where(hm, km, 0.0))]
    km_stack = jnp.concatenate(pieces, axis=0).astype(BF16)

    krow = lax.broadcasted_iota(jnp.int32, (blk, blk), 0)
    qcol = lax.broadcasted_iota(jnp.int32, (blk, blk), 1)
    causal = krow <= qcol
    krow_f = krow.astype(F32)
    bidx = lax.broadcasted_iota(jnp.int32, (nbp, blk), 0)

    gates = {}
    outs = {}

    def unit(c, h, u):
        q = q_ref[0, c * blk:(c + 1) * blk, :]
        hm = (lane >= HEAD_DIM * h) & (lane < HEAD_DIM * (h + 1))
        qh = jnp.where(hm, q, jnp.zeros_like(q))
        kbias = slopes[h] * krow_f
        if c > 0:
            if h == 0:
                gates[c] = _nt(km_stack, q)
            past = bidx < c
            g = sum(gates[c][(3 * h + p) * nbp:(3 * h + p + 1) * nbp, :] for p in range(3))
            g = jnp.where(past, g, NEG_INF)
            rank = jnp.zeros((nbp, blk), F32)
            for n2 in range(c):
                gm = g[n2:n2 + 1, :]
                beats = (gm > g) | ((gm == g) & (bidx > n2))
                rank = rank + jnp.where(beats, 1.0, 0.0)
            drop = jnp.where(past & (rank < n_sel), 0.0, NEG_INF)

        def score(j):
            s = _nt(k_ref[0, j * blk:(j + 1) * blk, :], qh) + kbias
            if j == c:
                return jnp.where(causal, s, NEG_INF)
            return s + (drop[j:j + 1, :] + slopes[h] * float((j - c) * blk))

        def done(out):
            outs[c, h] = out
            if h == MOBA_HEADS - 1:
                full = jnp.concatenate([outs[c, hh] for hh in range(MOBA_HEADS)], axis=0)
                o_ref[0, c * blk:(c + 1) * blk, :] = full.T.astype(BF16)

        vt = lambda j: vt_ref[0, HEAD_DIM * h:HEAD_DIM * (h + 1), j * blk:(j + 1) * blk]
        return _Unit(s_ref.at[u % 3], c + 1, blk, blk, score, vt, done).steps

    _interleave([functools.partial(unit, c, h, MOBA_HEADS * c + h) for c in range(nb) for h in range(MOBA_HEADS)])


def _moba_attention(qb, kb, vbt, slopes):
    B, S, hw = qb.shape
    blk = MOBA_BLOCK
    nb = S // blk
    nbp = -(-nb // 8) * 8
    n_sel = max(1, min(MOBA_TOPK, nb - 1))
    full = pl.BlockSpec((1, S, hw), lambda b: (b, 0, 0))
    return pl.pallas_call(
        functools.partial(_moba_kernel, nb=nb, nbp=nbp, n_sel=n_sel, slopes=slopes),
        grid=(B,),
        in_specs=[full, full, pl.BlockSpec((1, hw, S), lambda b: (b, 0, 0))],
        out_specs=full,
        out_shape=jax.ShapeDtypeStruct((B, S, hw), BF16),
        scratch_shapes=[pltpu.VMEM((3, S, blk), F32)],
        compiler_params=_params("arbitrary"),
        name="moba_attn",
    )(qb, kb, vbt)


KEY_POS_INF = 0x7F800000
KEY_NEG_INF = -0x7F800001


def _key_to_float(key):
    key = jnp.clip(key, KEY_NEG_INF, KEY_POS_INF)
    return lax.bitcast_convert_type(jnp.where(key >= 0, key, key ^ jnp.int32(0x7FFFFFFF)), F32)


def _kth_largest(sc_ref, sb_ref, rows, n_chunks, tq, k):
    i32 = jnp.int32

    def count(ref, c, one, zero, n):
        acc = None
        for j in range(n_chunks):
            f = _fold_rows(jnp.where(ref[rows(j), :] >= c, one, zero), jnp.add, n).astype(F32)
            acc = f if acc is None else acc + f
        return jnp.sum(acc, axis=0, keepdims=True)

    count16 = lambda c: count(sb_ref, c.astype(BF16), jnp.bfloat16(1), jnp.bfloat16(0), 16)
    count32 = lambda c: count(sc_ref, c, 1.0, 0.0, 8)
    grid_key = lambda k16: lax.shift_left(k16, 16) | jnp.where(k16 < 0, i32(0xFFFF), i32(0))

    zero = jnp.zeros((1, tq), i32)
    k16 = jnp.where(count16(_key_to_float(grid_key(zero))) >= k, zero, i32(-2 ** 15))

    def coarse(it, k16):
        cand = k16 | lax.shift_left(i32(1), i32(14) - it)
        return jnp.where(count16(_key_to_float(grid_key(cand))) >= k, cand, k16)

    g = grid_key(lax.fori_loop(0, 15, coarse, k16))
    bracket = (g - i32(0x8000), g + i32(0x10001))

    def fine(_, lo_hi):
        lo, hi = lo_hi
        mid = lo + lax.shift_right_arithmetic(hi - lo, 1)
        ok = count32(_key_to_float(mid)) >= k
        return jnp.where(ok, mid, lo), jnp.where(ok, hi, mid)

    lo, _ = lax.fori_loop(0, 17, fine, bracket)
    return _key_to_float(lo)


def _dsa_kernel(q_ref, k_ref, vt_ref, iq_ref, ik_ref, iwt_ref, kpos_ref, o_ref, sc_ref, sb_ref, mb_ref, s_ref, *,
                tq, ch, groups, n_keep, slopes, idx_scale):
    i = pl.program_id(1)
    hw = DSA_HEADS * HEAD_DIM
    q0 = i * tq
    qpos = q0 + lax.broadcasted_iota(jnp.int32, (1, tq), 1)
    krow = lax.broadcasted_iota(jnp.int32, (ch, 1), 0)
    lane = lax.broadcasted_iota(jnp.int32, (1, LANES), 1)
    lane_h = lax.broadcasted_iota(jnp.int32, (1, hw), 1)
    grp = LANES // IDX_DIM

    def body(lo, hi):
        n_chunks = hi * tq // ch
        rows = lambda j: slice(j * ch, (j + 1) * ch)
        iwt = iwt_ref[0]
        qms = []
        for hh in range(IDX_HEADS):
            g, r = divmod(hh, grp)
            iqg = iq_ref[0, :, LANES * g:LANES * (g + 1)]
            qms.append(jnp.where((lane >= IDX_DIM * r) & (lane < IDX_DIM * (r + 1)), iqg, jnp.zeros_like(iqg)))
        for j in range(n_chunks):
            ikc = ik_ref[0, rows(j), :]
            score = None
            for hh in range(IDX_HEADS):
                term = jnp.maximum(_nt(ikc, qms[hh]), 0.0) * iwt[hh:hh + 1, :]
                score = term if score is None else score + term
            score = jnp.where(krow + j * ch <= qpos, score * idx_scale, NEG_INF)
            sc_ref[rows(j), :] = score
            sb_ref[rows(j), :] = score.astype(BF16)

        def mask_at_least(thr):
            for j in range(n_chunks):
                keep = (sc_ref[rows(j), :] >= thr) & (krow + j * ch <= qpos)
                mb_ref[rows(j), :] = jnp.where(keep, 0.0, NEG_INF)

        def count(pred):
            acc = None
            for j in range(n_chunks):
                f = _fold_rows(jnp.where(pred(sc_ref[rows(j), :]), 1.0, 0.0), jnp.add)
                acc = f if acc is None else acc + f
            return jnp.sum(acc, axis=0, keepdims=True)

        keep_all = functools.partial(mask_at_least, jnp.full((1, tq), NEG_INF, F32))

        def select_and_mask():
            thr = _kth_largest(sc_ref, sb_ref, rows, n_chunks, tq, float(n_keep))
            tied = jnp.max(count(lambda v: v >= thr)) > float(n_keep)
            pl.when(jnp.logical_not(tied))(functools.partial(mask_at_least, thr))

            @pl.when(tied)
            def _():
                need = float(n_keep) - count(lambda v: v > thr)
                tri = jnp.where(lax.broadcasted_iota(jnp.int32, (ch, ch), 1)
                                <= lax.broadcasted_iota(jnp.int32, (ch, ch), 0), 1.0, 0.0).astype(BF16)

                def chunk(j, seen):
                    at = pl.ds(pl.multiple_of(j * ch, ch), ch)
                    sc = sc_ref[at, :]
                    eq = jnp.where(sc == thr, 1.0, 0.0)
                    rank = _mm(tri, eq.astype(BF16)) + seen
                    keep = (sc > thr) | ((sc == thr) & (rank <= need))
                    keep = keep & (krow + j * ch <= qpos)
                    mb_ref[at, :] = jnp.where(keep, 0.0, NEG_INF)
                    return seen + jnp.sum(eq, axis=0, keepdims=True)

                lax.fori_loop(0, n_chunks, chunk, jnp.zeros((1, tq), F32))

        if n_chunks * ch <= n_keep:
            keep_all()
        elif lo * tq < n_keep:
            few = q0 + tq <= n_keep
            pl.when(few)(keep_all)
            pl.when(jnp.logical_not(few))(select_and_mask)
        else:
            select_and_mask()

        q = q_ref[0]
        q0f = q0.astype(F32)
        outs = {}

        def unit(h):
            hm = (lane_h >= HEAD_DIM * h) & (lane_h < HEAD_DIM * (h + 1))
            qh = jnp.where(hm, q, jnp.zeros_like(q))

            def score(j):
                kb = slopes[h] * (kpos_ref[rows(j), :] - q0f)
                kb = jnp.concatenate([kb] * (tq // LANES), axis=1)
                return _nt(k_ref[0, rows(j), :], qh) + kb + mb_ref[rows(j), :]

            def done(out):
                outs[h] = out
                if h == DSA_HEADS - 1:
                    full = jnp.concatenate([outs[hh] for hh in range(DSA_HEADS)], axis=0)
                    o_ref[0] = full.T.astype(BF16)

            vt = lambda j: vt_ref[0, HEAD_DIM * h:HEAD_DIM * (h + 1), rows(j)]
            return _Unit(s_ref.at[h], n_chunks, ch, tq, score, vt, done).steps

        _interleave([functools.partial(unit, h) for h in range(DSA_HEADS)])

    for lo, hi in groups:
        pl.when((i >= lo) & (i < hi))(functools.partial(body, lo, hi))


def _dsa_attention(qc, kc, vct, iq, ik, iwt, slopes, tq, sizes):
    B, S, hw = qc.shape
    nq = S // tq
    assert sum(sizes) == nq
    groups = tuple((sum(sizes[:n]), sum(sizes[:n + 1])) for n in range(len(sizes)))
    ch = tq
    n_keep = min(DSA_TOPK, S // 4)
    kpos = jnp.broadcast_to(jnp.arange(S, dtype=F32)[:, None], (S, LANES))
    full = lambda w: pl.BlockSpec((1, S, w), lambda b, i: (b, 0, 0))
    tile = lambda w: pl.BlockSpec((1, tq, w), lambda b, i: (b, i, 0))
    return pl.pallas_call(
        functools.partial(_dsa_kernel, tq=tq, ch=ch, groups=groups, n_keep=n_keep, slopes=slopes,
                          idx_scale=(IDX_DIM * IDX_HEADS) ** -0.5),
        grid=(B, nq),
        in_specs=[tile(hw), full(hw), pl.BlockSpec((1, hw, S), lambda b, i: (b, 0, 0)),
                  tile(IDX_HEADS * IDX_DIM), full(LANES), pl.BlockSpec((1, LANES, tq), lambda b, i: (b, 0, i)),
                  pl.BlockSpec((S, LANES), lambda b, i: (0, 0))],
        out_specs=tile(hw),
        out_shape=jax.ShapeDtypeStruct((B, S, hw), BF16),
        scratch_shapes=[pltpu.VMEM((S, tq), F32), pltpu.VMEM((S, tq), BF16), pltpu.VMEM((S, tq), F32),
                        pltpu.VMEM((DSA_HEADS, S, tq), F32)],
        compiler_params=_params("arbitrary", "arbitrary"),
        name="dsa_attn",
    )(qc, kc, vct, iq, ik, iwt, kpos)


def _merge_kernel(ya_ref, yb_ref, yc_ref, gt_ref, x_ref, mod_ref, wa_ref, wb_ref, wc_ref, wo_ref, o_ref):
    d = D_MODEL
    merged = gt_ref[0, :, 0:d].astype(F32) * _mm(ya_ref[0], wa_ref[0])
    merged = merged + gt_ref[0, :, d:2 * d].astype(F32) * _mm(yb_ref[0], wb_ref[0])
    merged = merged + gt_ref[0, :, 2 * d:3 * d].astype(F32) * _mm(yc_ref[0], wc_ref[0])
    o_ref[0] = x_ref[0] + mod_ref[0, 2:3, :] * _mm(merged.astype(BF16), wo_ref[0])


def _merge(ya, yb, yc, gt, x, mod, lw, l, tm):
    B, S, D = x.shape
    row = lambda w: pl.BlockSpec((1, tm, w), lambda b, i: (b, i, 0))
    consts = [lw["w_br_a"], lw["w_br_b"], lw["w_br_c"], lw["w_out"]]
    return pl.pallas_call(
        _merge_kernel,
        grid=(B, S // tm),
        in_specs=[row(ya.shape[-1]), row(yb.shape[-1]), row(yc.shape[-1]), row(3 * D), row(D),
                  pl.BlockSpec((1, 6, D), lambda b, i: (b, 0, 0))] + [_layer_spec(c.shape, l) for c in consts],
        out_specs=row(D),
        out_shape=jax.ShapeDtypeStruct((B, S, D), F32),
        compiler_params=_params("arbitrary", "arbitrary"),
        name="merge_out",
    )(ya, yb, yc, gt, x, mod, *consts)


def _ffn_kernel(x_ref, mod_ref, n2_ref, wg_ref, wu_ref, wd_ref, o_ref):
    x = x_ref[0]
    h = (_rms(x, n2_ref[0]) * (1.0 + mod_ref[0, 4:5, :]) + mod_ref[0, 3:4, :]).astype(BF16)
    g = _mm(h, wg_ref[0])
    u = _mm(h, wu_ref[0])
    act = (g * jax.nn.sigmoid(g) * u).astype(BF16)
    o_ref[0] = x + mod_ref[0, 5:6, :] * _mm(act, wd_ref[0])


def _ffn(x, mod, lw, l, tm):
    B, S, D = x.shape
    row = pl.BlockSpec((1, tm, D), lambda b, i: (b, i, 0))
    gu = lw["w_gu"]
    half = (gu.shape[0], gu.shape[1], gu.shape[2] // 2)
    consts = [lw["n2"], gu, gu, lw["w_d"]]
    return pl.pallas_call(
        _ffn_kernel,
        grid=(B, S // tm),
        in_specs=[row, pl.BlockSpec((1, 6, D), lambda b, i: (b, 0, 0)), _layer_spec(lw["n2"].shape, l),
                  _layer_spec(half, l, 0), _layer_spec(half, l, 1), _layer_spec(lw["w_d"].shape, l)],
        out_specs=row,
        out_shape=jax.ShapeDtypeStruct((B, S, D), F32),
        compiler_params=_params("arbitrary", "arbitrary"),
        name="swiglu",
    )(x, mod, *consts)


def _pack_weights(norm1, w_in, g_cq, w_uq, g_ckv, w_ukv, qn_mla, kn_mla, qn_moba, kn_moba, qn_dsa, kn_dsa,
                  w_br_a, w_br_b, w_br_c, w_out, norm2, w_gu, w_down):
    L = w_in.shape[0]
    o = 0
    cols = {}
    for name, wd in (("cq", MLA_Q_RANK), ("ckv", MLA_KV_RANK), ("kr", MLA_ROPE),
                     ("b", 3 * MOBA_HEADS * HEAD_DIM), ("c", 3 * DSA_HEADS * HEAD_DIM),
                     ("iq", IDX_HEADS * IDX_DIM), ("ik", IDX_DIM), ("iw", IDX_HEADS), ("g", 3 * D_MODEL)):
        cols[name] = w_in[:, :, o:o + wd]
        o += wd
    pad_last = lambda v, lo, hi: jnp.pad(v, ((0, 0),) * (v.ndim - 1) + ((lo, hi),))
    kr_slot = pad_last(cols["kr"], MLA_NOPE, LANES - MLA_QK)
    ik_rep = jnp.tile(cols["ik"], (1, 1, LANES // IDX_DIM))
    hw = MOBA_HEADS * HEAD_DIM
    w_all = jnp.concatenate([cols["cq"], cols["ckv"], kr_slot, cols["b"][:, :, :2 * hw], cols["c"][:, :, :2 * hw],
                             cols["iq"], ik_rep, cols["g"]], axis=2).astype(BF16)
    assert w_all.shape[-1] == N_ALL
    w_t = jnp.concatenate([cols["b"][:, :, 2 * hw:], cols["c"][:, :, 2 * hw:],
                           pad_last(cols["iw"], 0, LANES - IDX_HEADS)], axis=2)
    wuq = pad_last(w_uq.reshape(L, MLA_Q_RANK, MLA_HEADS, MLA_QK), 0, LANES - MLA_QK)
    wukv = w_ukv.reshape(L, MLA_KV_RANK, MLA_HEADS, MLA_NOPE + MLA_V)
    wk = pad_last(wukv[..., :MLA_NOPE], 0, LANES - MLA_NOPE)
    wv = wukv[..., MLA_NOPE:].reshape(L, MLA_KV_RANK, MLA_HEADS * MLA_V)
    row = lambda v: v.reshape(L, 1, -1)
    t = lambda v: jnp.swapaxes(v, 1, 2)
    return {
        "n1": row(norm1), "w_all": w_all, "w_t": t(w_t).astype(BF16), "g_cq": row(g_cq),
        "w_uq": wuq.reshape(L, MLA_Q_RANK, MLA_HEADS * LANES).astype(BF16), "g_ckv": row(g_ckv),
        "w_k": wk.reshape(L, MLA_KV_RANK, MLA_HEADS * LANES).astype(BF16), "w_v": t(wv).astype(BF16),
        "qn_a": row(pad_last(qn_mla * MLA_QK ** -0.5, 0, LANES - MLA_QK)),
        "kn_a": row(pad_last(kn_mla, 0, LANES - MLA_QK)),
        "qn_b": row(jnp.tile(qn_moba * HEAD_DIM ** -0.5, (1, MOBA_HEADS))), "kn_b": row(jnp.tile(kn_moba, (1, MOBA_HEADS))),
        "qn_c": row(jnp.tile(qn_dsa * HEAD_DIM ** -0.5, (1, DSA_HEADS))), "kn_c": row(jnp.tile(kn_dsa, (1, DSA_HEADS))),
        "w_br_a": w_br_a.astype(BF16), "w_br_b": w_br_b.astype(BF16), "w_br_c": w_br_c.astype(BF16),
        "w_out": w_out.astype(BF16), "n2": row(norm2), "w_gu": w_gu.astype(BF16), "w_d": w_down.astype(BF16),
    }


def _rope_tables(S):
    half = MLA_ROPE // 2
    freqs = ROPE_THETA ** (-jnp.arange(half, dtype=F32) / half)
    ang = jnp.arange(S, dtype=F32)[:, None] * freqs[None, :]
    cos, sin = jnp.cos(ang), jnp.sin(ang)
    zeros = lambda n: jnp.zeros((S, n), F32)
    tail = LANES - MLA_QK
    rc = jnp.concatenate([jnp.ones((S, MLA_NOPE), F32), cos, cos, jnp.ones((S, tail), F32)], axis=1)
    rs1 = jnp.concatenate([zeros(MLA_NOPE), -sin, zeros(half + tail)], axis=1)
    rs2 = jnp.concatenate([zeros(MLA_NOPE + half), sin, zeros(tail)], axis=1)
    return rc, rs1, rs2


def _dsa_groups(nq):
    single = min(4, nq)
    rest = nq - single
    return (1,) * single + (2,) * (rest // 2) + (1,) * (rest % 2)


def _tiles(S):
    dsa_q = min(256, S)
    return {"proj": min(256, S), "out": min(512, S), "mla_q": min(512, S), "dsa_q": dsa_q,
            "dsa_groups": _dsa_groups(S // dsa_q)}


def kernel(x, c, w_ada, b_ada, norm1, w_in, g_cq, w_uq, g_ckv, w_ukv, qn_mla, kn_mla, qn_moba, kn_moba, qn_dsa, kn_dsa, w_br_a, w_br_b, w_br_c, w_out, norm2, w_gu, w_down):
    B, S, D = x.shape
    L = w_ada.shape[0]
    t = _tiles(S)
    assert D == D_MODEL and S % MOBA_BLOCK == 0 and all(S % t[k] == 0 for k in ("proj", "out", "mla_q", "dsa_q"))
    n_slopes = MOBA_HEADS + DSA_HEADS
    slopes = [2.0 ** (-8.0 * (i + 1) / n_slopes) for i in range(n_slopes)]
    mod_all = _ada(c, w_ada, b_ada).reshape(L, B, 6, D)
    rope_tabs = _rope_tables(S)
    seg = jnp.arange(MOBA_HEADS * HEAD_DIM) // HEAD_DIM
    bd = (seg[:, None] == seg[None, :]).astype(BF16)
    lw = _pack_weights(norm1, w_in, g_cq, w_uq, g_ckv, w_ukv, qn_mla, kn_mla, qn_moba, kn_moba, qn_dsa, kn_dsa,
                       w_br_a, w_br_b, w_br_c, w_out, norm2, w_gu, w_down)
    for l in range(L):
        mod = mod_all[l]
        qa, ka, va, qb, kb, vb, qc, kc, vc, iq, ik, iw, gt = _inproj(x, mod, lw, l, rope_tabs, bd, t["proj"])
        ya = _mla_attention(qa, ka, va, tq=t["mla_q"])
        yb = _moba_attention(qb, kb, vb, tuple(slopes[0::2]))
        yc = _dsa_attention(qc, kc, vc, iq, ik, iw, tuple(slopes[1::2]), tq=t["dsa_q"], sizes=t["dsa_groups"])
        x = _merge(ya, yb, yc, gt, x, mod, lw, l, t["out"])
        x = _ffn(x, mod, lw, l, t["out"])
    return x
```

```python
import functools

import jax
import jax.numpy as jnp
from jax import lax
from jax.experimental import pallas as pl
from jax.experimental.pallas import tpu as pltpu

F32 = jnp.float32
BF16 = jnp.bfloat16

D_MODEL = 1024
HEAD_DIM = 64
RMS_EPS = 1e-6
MLA_HEADS = 8
MLA_NOPE = 64
MLA_ROPE = 32
MLA_QK = MLA_NOPE + MLA_ROPE
MLA_V = 64
MLA_Q_RANK = 768
MLA_KV_RANK = 256
ROPE_THETA = 10000.0
MOBA_HEADS = 4
MOBA_BLOCK = 256
MOBA_TOPK = 3
DSA_HEADS = 4
IDX_HEADS = 16
IDX_DIM = 32
DSA_TOPK = 256

LANES = 128
VMEM_LIMIT = 56 * 1024 * 1024

OFF_CQ = 0
OFF_CKV = OFF_CQ + MLA_Q_RANK
OFF_KR = OFF_CKV + MLA_KV_RANK
OFF_B = OFF_KR + LANES
OFF_C = OFF_B + 2 * MOBA_HEADS * HEAD_DIM
OFF_IQ = OFF_C + 2 * DSA_HEADS * HEAD_DIM
OFF_IK = OFF_IQ + IDX_HEADS * IDX_DIM
OFF_G = OFF_IK + LANES
N_ALL = OFF_G + 3 * D_MODEL
ROW_VB = 0
ROW_VC = ROW_VB + MOBA_HEADS * HEAD_DIM
ROW_IW = ROW_VC + DSA_HEADS * HEAD_DIM
N_T = ROW_IW + LANES


def _nt(a, b):
    return lax.dot_general(a, b, (((1,), (1,)), ((), ())), preferred_element_type=F32)


def _mm(a, b):
    return jnp.dot(a, b, preferred_element_type=F32)


def _split3(x):
    hi = x.astype(BF16)
    r = x - hi.astype(F32)
    mid = r.astype(BF16)
    lo = (r - mid.astype(F32)).astype(BF16)
    return hi, mid, lo


def _rms(x, g):
    return x * lax.rsqrt(jnp.mean(x * x, axis=-1, keepdims=True) + RMS_EPS) * g


def _const_spec(shape):
    nd = len(shape)
    return pl.BlockSpec(shape, lambda *_: (0,) * nd, pipeline_mode=pl.Buffered(1))


def _layer_spec(shape, l, part=0):
    rest = (0,) * (len(shape) - 2) + (part,)
    return pl.BlockSpec((1,) + tuple(shape[1:]), lambda *_: (l,) + rest, pipeline_mode=pl.Buffered(1))


def _params(*sem):
    return pltpu.CompilerParams(dimension_semantics=sem, vmem_limit_bytes=VMEM_LIMIT)


def _ada_kernel(c_ref, w_ref, b_ref, o_ref):
    c = c_ref[...]
    cond = (c * jax.nn.sigmoid(c)).astype(BF16)
    o_ref[0] = _mm(cond, w_ref[0].astype(BF16)) + b_ref[0]


def _ada(c, w_ada, b_ada):
    L, D, N = w_ada.shape
    B = c.shape[0]
    tn = 1024
    return pl.pallas_call(
        _ada_kernel,
        grid=(L, N // tn),
        in_specs=[
            pl.BlockSpec((B, D), lambda l, n: (0, 0)),
            pl.BlockSpec((1, D, tn), lambda l, n: (l, 0, n)),
            pl.BlockSpec((1, 1, tn), lambda l, n: (l, 0, n)),
        ],
        out_specs=pl.BlockSpec((1, B, tn), lambda l, n: (l, 0, n)),
        out_shape=jax.ShapeDtypeStruct((L, B, N), F32),
        compiler_params=_params("arbitrary", "arbitrary"),
        name="ada_mod",
    )(c, w_ada, b_ada.reshape(L, 1, N))


def _rope(x, c, s1, s2):
    return x * c + pltpu.roll(x, LANES - 16, 1) * s1 + pltpu.roll(x, 16, 1) * s2


def _inproj_kernel(x_ref, mod_ref, n1_ref, w_ref, wt_ref, gcq_ref, wuq_ref, gckv_ref, wk_ref, wv_ref,
                   qna_ref, kna_ref, rc_ref, rs1_ref, rs2_ref, bd_ref,
                   qnb_ref, knb_ref, qnc_ref, knc_ref,
                   qa_ref, ka_ref, va_ref, qb_ref, kb_ref, vb_ref, qc_ref, kc_ref, vc_ref,
                   iq_ref, ik_ref, iw_ref, gt_ref):
    x = x_ref[0]
    sh1 = mod_ref[0, 0:1, :]
    sc1 = mod_ref[0, 1:2, :]
    hb = (_rms(x, n1_ref[0]) * (1.0 + sc1) + sh1).astype(BF16)

    a = _mm(hb, w_ref[0, :, OFF_CQ:OFF_B])
    cqn = _rms(a[:, OFF_CQ:OFF_CKV], gcq_ref[0]).astype(BF16)
    ckvn = _rms(a[:, OFF_CKV:OFF_KR], gckv_ref[0]).astype(BF16)
    kr = a[:, OFF_KR:OFF_B]
    qraw = _mm(cqn, wuq_ref[0])
    kraw = _mm(ckvn, wk_ref[0])
    va_ref[0] = _nt(wv_ref[0], ckvn).astype(BF16)
    rc, rs1, rs2 = rc_ref[...], rs1_ref[...], rs2_ref[...]
    qna, kna = qna_ref[0], kna_ref[0]
    inv_qk = 1.0 / MLA_QK
    for h in range(MLA_HEADS):
        sl = slice(LANES * h, LANES * (h + 1))
        qh = qraw[:, sl]
        qh = qh * lax.rsqrt(jnp.sum(qh * qh, axis=-1, keepdims=True) * inv_qk + RMS_EPS) * qna
        qa_ref[0, :, sl] = _rope(qh, rc, rs1, rs2).astype(BF16)
        kh = kraw[:, sl] + kr
        kh = kh * lax.rsqrt(jnp.sum(kh * kh, axis=-1, keepdims=True) * inv_qk + RMS_EPS) * kna
        ka_ref[0, :, sl] = _rope(kh, rc, rs1, rs2).astype(BF16)
        if h in (1, 4, 7):
            g0 = D_MODEL * (h // 3)
            gt_ref[0, :, g0:g0 + D_MODEL] = jax.nn.sigmoid(
                _mm(hb, w_ref[0, :, OFF_G + g0:OFF_G + g0 + D_MODEL])).astype(BF16)

    bd = bd_ref[...]

    def segnorm(v, g):
        ss = sum(_mm(part, bd) for part in _split3(v * v)[:2])
        return v * lax.rsqrt(ss * (1.0 / HEAD_DIM) + RMS_EPS) * g

    hw = MOBA_HEADS * HEAD_DIM
    pb = _mm(hb, w_ref[0, :, OFF_B:OFF_C])
    qb_ref[0] = segnorm(pb[:, 0:hw], qnb_ref[0]).astype(BF16)
    kb_ref[0] = segnorm(pb[:, hw:2 * hw], knb_ref[0]).astype(BF16)
    pc = _mm(hb, w_ref[0, :, OFF_C:OFF_IQ])
    qc_ref[0] = segnorm(pc[:, 0:hw], qnc_ref[0]).astype(BF16)
    kc_ref[0] = segnorm(pc[:, hw:2 * hw], knc_ref[0]).astype(BF16)

    pt = _nt(wt_ref[0], hb)
    vb_ref[0] = pt[ROW_VB:ROW_VC, :].astype(BF16)
    vc_ref[0] = pt[ROW_VC:ROW_IW, :].astype(BF16)
    iw_ref[0] = pt[ROW_IW:N_T, :]

    pi = _mm(hb, w_ref[0, :, OFF_IQ:OFF_G])
    iq_ref[0] = pi[:, 0:OFF_IK - OFF_IQ].astype(BF16)
    ik_ref[0] = pi[:, OFF_IK - OFF_IQ:OFF_G - OFF_IQ].astype(BF16)


def _inproj(x, mod, lw, l, rope_tabs, bd, tm):
    B, S, D = x.shape
    hw = MOBA_HEADS * HEAD_DIM
    row = lambda w: pl.BlockSpec((1, tm, w), lambda b, i: (b, i, 0))
    tab = pl.BlockSpec((tm, LANES), lambda b, i: (i, 0))
    consts = [lw["n1"], lw["w_all"], lw["w_t"], lw["g_cq"], lw["w_uq"], lw["g_ckv"], lw["w_k"], lw["w_v"],
              lw["qn_a"], lw["kn_a"]]
    consts2 = [lw["qn_b"], lw["kn_b"], lw["qn_c"], lw["kn_c"]]
    widths = [MLA_HEADS * LANES, MLA_HEADS * LANES, MLA_HEADS * MLA_V, hw, hw, hw, hw, hw, hw,
              IDX_HEADS * IDX_DIM, LANES, LANES, 3 * D_MODEL]
    dtypes = [BF16] * 11 + [F32, BF16]
    transposed = [False, False, True] * 3 + [False, False, True, False]
    col = lambda w: pl.BlockSpec((1, w, tm), lambda b, i: (b, 0, i))
    return pl.pallas_call(
        _inproj_kernel,
        grid=(B, S // tm),
        in_specs=[row(D), pl.BlockSpec((1, 6, D), lambda b, i: (b, 0, 0))]
        + [_layer_spec(c.shape, l) for c in consts] + [tab, tab, tab, _const_spec(bd.shape)]
        + [_layer_spec(c.shape, l) for c in consts2],
        out_specs=[col(w) if tr else row(w) for w, tr in zip(widths, transposed)],
        out_shape=[jax.ShapeDtypeStruct((B, w, S) if tr else (B, S, w), dt)
                   for w, dt, tr in zip(widths, dtypes, transposed)],
        compiler_params=_params("arbitrary", "arbitrary"),
        name="in_proj",
    )(x, mod, *consts, *rope_tabs, bd, *consts2)


NEG_INF = float("-inf")


def _fold_rows(x, op, n=8):
    out = x[0:n, :]
    for t in range(1, x.shape[0] // n):
        out = op(out, x[n * t:n * (t + 1), :])
    return out


def _interleave(units):
    pending = []
    for make in units:
        first, second = make()
        for t in range(max(len(first), len(pending))):
            if t < len(first):
                first[t]()
            if t < len(pending):
                pending[t]()
        pending = second
    for step in pending:
        step()


class _Unit:
    def __init__(self, logits, n_chunks, ch, tq, score, vt_chunk, done):
        self.m8 = jnp.full((8, tq), NEG_INF, F32)
        self.l8 = None
        self.acc = None

        def first(j):
            s = score(j)
            logits[j * ch:(j + 1) * ch, :] = s
            self.m8 = jnp.maximum(self.m8, _fold_rows(s, jnp.maximum))

        def second(j):
            if j == 0:
                self.m = jnp.max(self.m8, axis=0, keepdims=True)
            p = jnp.exp(logits[j * ch:(j + 1) * ch, :] - self.m)
            f = _fold_rows(p, jnp.add)
            pv = _mm(vt_chunk(j), p.astype(BF16))
            self.l8 = f if self.l8 is None else self.l8 + f
            self.acc = pv if self.acc is None else self.acc + pv
            if j == n_chunks - 1:
                done(self.acc / jnp.sum(self.l8, axis=0, keepdims=True))

        self.steps = ([functools.partial(first, j) for j in range(n_chunks)],
                      [functools.partial(second, j) for j in range(n_chunks)])


def _mla_kernel(q_ref, k_ref, vt_ref, o_ref, s_ref, *, tq, nq):
    krow = lax.broadcasted_iota(jnp.int32, (tq, tq), 0)
    qcol = lax.broadcasted_iota(jnp.int32, (tq, tq), 1)
    causal = krow <= qcol
    outs = {}

    def unit(c, hh, u):
        sl = slice(LANES * hh, LANES * (hh + 1))
        q = q_ref[0, c * tq:(c + 1) * tq, sl]

        def score(j):
            s = _nt(k_ref[0, j * tq:(j + 1) * tq, sl], q)
            return jnp.where(causal, s, NEG_INF) if j == c else s

        def done(out):
            outs[c, hh] = out
            if hh == 1:
                pair = jnp.concatenate([outs[c, 0], outs[c, 1]], axis=0)
                o_ref[0, c * tq:(c + 1) * tq, :] = pair.T.astype(BF16)

        vt = lambda j: vt_ref[0, MLA_V * hh:MLA_V * (hh + 1), j * tq:(j + 1) * tq]
        return _Unit(s_ref.at[u % 3], c + 1, tq, tq, score, vt, done).steps

    _interleave([functools.partial(unit, c, hh, 2 * c + hh) for c in range(nq) for hh in range(2)])


def _mla_attention(qa, ka, vat, tq):
    B, S, _ = qa.shape
    pairs = MLA_HEADS // 2
    nq = S // tq
    wide = pl.BlockSpec((1, S, 2 * LANES), lambda b, h: (b, 0, h))
    return pl.pallas_call(
        functools.partial(_mla_kernel, tq=tq, nq=nq),
        grid=(B, pairs),
        in_specs=[wide, wide, pl.BlockSpec((1, 2 * MLA_V, S), lambda b, h: (b, h, 0))],
        out_specs=pl.BlockSpec((1, S, 2 * MLA_V), lambda b, h: (b, 0, h)),
        out_shape=jax.ShapeDtypeStruct((B, S, MLA_HEADS * MLA_V), BF16),
        scratch_shapes=[pltpu.VMEM((3, S, tq), F32)],
        compiler_params=_params("arbitrary", "arbitrary"),
        name="mla_attn",
    )(qa, ka, vat)


def _moba_kernel(q_ref, k_ref, vt_ref, o_ref, s_ref, *, nb, nbp, n_sel, slopes):
    blk = MOBA_BLOCK
    hw = MOBA_HEADS * HEAD_DIM
    lane = lax.broadcasted_iota(jnp.int32, (1, hw), 1)

    means = [jnp.mean(k_ref[0, n * blk:(n + 1) * blk, :].astype(F32), axis=0, keepdims=True)
             for n in range(nb)]
    km = jnp.concatenate(means + [jnp.zeros((1, hw), F32)] * (nbp - nb), axis=0)
    pieces = []
    for h in range(MOBA_HEADS):
        hm = (lane >= HEAD_DIM * h) & (lane < HEAD_DIM * (h + 1))
        pieces += [p.astype(F32) for p in _split3(jnp.where(hm, km, 0.0))]
    km_stack = jnp.concatenate(pieces, axis=0).astype(BF16)

    krow = lax.broadcasted_iota(jnp.int32, (blk, blk), 0)
    qcol = lax.broadcasted_iota(jnp.int32, (blk, blk), 1)
    causal = krow <= qcol
    krow_f = krow.astype(F32)
    bidx = lax.broadcasted_iota(jnp.int32, (nbp, blk), 0)

    gates = {}
    outs = {}

    def unit(c, h, u):
        q = q_ref[0, c * blk:(c + 1) * blk, :]
        hm = (lane >= HEAD_DIM * h) & (lane < HEAD_DIM * (h + 1))
        qh = jnp.where(hm, q, jnp.zeros_like(q))
        kbias = slopes[h] * krow_f
        if c > 0:
            if h == 0:
                gates[c] = _nt(km_stack, q)
            past = bidx < c
            g = sum(gates[c][(3 * h + p) * nbp:(3 * h + p + 1) * nbp, :] for p in range(3))
            g = jnp.where(past, g, NEG_INF)
            rank = jnp.zeros((nbp, blk), F32)
            for n2 in range(c):
                gm = g[n2:n2 + 1, :]
                beats = (gm > g) | ((gm == g) & (bidx > n2))
                rank = rank + jnp.where(beats, 1.0, 0.0)
            drop = jnp.where(past & (rank < n_sel), 0.0, NEG_INF)

        def score(j):
            s = _nt(k_ref[0, j * blk:(j + 1) * blk, :], qh) + kbias
            if j == c:
                return jnp.where(causal, s, NEG_INF)
            return s + (drop[j:j + 1, :] + slopes[h] * float((j - c) * blk))

        def done(out):
            outs[c, h] = out
            if h == MOBA_HEADS - 1:
                full = jnp.concatenate([outs[c, hh] for hh in range(MOBA_HEADS)], axis=0)
                o_ref[0, c * blk:(c + 1) * blk, :] = full.T.astype(BF16)

        vt = lambda j: vt_ref[0, HEAD_DIM * h:HEAD_DIM * (h + 1), j * blk:(j + 1) * blk]
        return _Unit(s_ref.at[u % 3], c + 1, blk, blk, score, vt, done).steps

    _interleave([functools.partial(unit, c, h, MOBA_HEADS * c + h) for c in range(nb) for h in range(MOBA_HEADS)])


def _moba_attention(qb, kb, vbt, slopes):
    B, S, hw = qb.shape
    blk = MOBA_BLOCK
    nb = S // blk
    nbp = -(-nb // 8) * 8
    n_sel = max(1, min(MOBA_TOPK, nb - 1))
    full = pl.BlockSpec((1, S, hw), lambda b: (b, 0, 0))
    return pl.pallas_call(
        functools.partial(_moba_kernel, nb=nb, nbp=nbp, n_sel=n_sel, slopes=slopes),
        grid=(B,),
        in_specs=[full, full, pl.BlockSpec((1, hw, S), lambda b: (b, 0, 0))],
        out_specs=full,
        out_shape=jax.ShapeDtypeStruct((B, S, hw), BF16),
        scratch_shapes=[pltpu.VMEM((3, S, blk), F32)],
        compiler_params=_params("arbitrary"),
        name="moba_attn",
    )(qb, kb, vbt)


KEY_POS_INF = 0x7F800000
KEY_NEG_INF = -0x7F800001


def _key_to_float(key):
    key = jnp.clip(key, KEY_NEG_INF, KEY_POS_INF)
    return lax.bitcast_convert_type(jnp.where(key >= 0, key, key ^ jnp.int32(0x7FFFFFFF)), F32)


def _kth_largest(sc_ref, sb_ref, rows, n_chunks, tq, k):
    i32 = jnp.int32

    def count(ref, c, one, zero, n):
        acc = None
        for j in range(n_chunks):
            f = _fold_rows(jnp.where(ref[rows(j), :] >= c, one, zero), jnp.add, n).astype(F32)
            acc = f if acc is None else acc + f
        return jnp.sum(acc, axis=0, keepdims=True)

    count16 = lambda c: count(sb_ref, c.astype(BF16), jnp.bfloat16(1), jnp.bfloat16(0), 16)
    count32 = lambda c: count(sc_ref, c, 1.0, 0.0, 8)
    grid_key = lambda k16: lax.shift_left(k16, 16) | jnp.where(k16 < 0, i32(0xFFFF), i32(0))

    zero = jnp.zeros((1, tq), i32)
    k16 = jnp.where(count16(_key_to_float(grid_key(zero))) >= k, zero, i32(-2 ** 15))

    def coarse(it, k16):
        cand = k16 | lax.shift_left(i32(1), i32(14) - it)
        return jnp.where(count16(_key_to_float(grid_key(cand))) >= k, cand, k16)

    g = grid_key(lax.fori_loop(0, 15, coarse, k16))
    bracket = (g - i32(0x8000), g + i32(0x10001))

    def fine(_, lo_hi):
        lo, hi = lo_hi
        mid = lo + lax.shift_right_arithmetic(hi - lo, 1)
        ok = count32(_key_to_float(mid)) >= k
        return jnp.where(ok, mid, lo), jnp.where(ok, hi, mid)

    lo, _ = lax.fori_loop(0, 17, fine, bracket)
    return _key_to_float(lo)


def _dsa_kernel(q_ref, k_ref, vt_ref, iq_ref, ik_ref, iwt_ref, kpos_ref, o_ref, sc_ref, sb_ref, mb_ref, s_ref, *,
                tq, ch, groups, n_keep, slopes, idx_scale):
    i = pl.program_id(1)
    hw = DSA_HEADS * HEAD_DIM
    q0 = i * tq
    qpos = q0 + lax.broadcasted_iota(jnp.int32, (1, tq), 1)
    krow = lax.broadcasted_iota(jnp.int32, (ch, 1), 0)
    lane = lax.broadcasted_iota(jnp.int32, (1, LANES), 1)
    lane_h = lax.broadcasted_iota(jnp.int32, (1, hw), 1)
    grp = LANES // IDX_DIM

    def body(lo, hi):
        n_chunks = hi * tq // ch
        rows = lambda j: slice(j * ch, (j + 1) * ch)
        iwt = iwt_ref[0] * idx_scale
        past_chunks = lo * tq // ch
        qms = []
        for hh in range(IDX_HEADS):
            g, r = divmod(hh, grp)
            iqg = iq_ref[0, :, LANES * g:LANES * (g + 1)]
            qms.append(jnp.where((lane >= IDX_DIM * r) & (lane < IDX_DIM * (r + 1)), iqg, jnp.zeros_like(iqg)))
        for j in range(n_chunks):
            ikc = ik_ref[0, rows(j), :]
            score = None
            for hh in range(IDX_HEADS):
                term = jnp.maximum(_nt(ikc, qms[hh]), 0.0) * iwt[hh:hh + 1, :]
                score = term if score is None else score + term
            if j >= past_chunks:
                score = jnp.where(krow + j * ch <= qpos, score, NEG_INF)
            sc_ref[rows(j), :] = score
            sb_ref[rows(j), :] = score.astype(BF16)

        def mask_at_least(thr):
            for j in range(n_chunks):
                keep = sc_ref[rows(j), :] >= thr
                if j >= past_chunks:
                    keep = keep & (krow + j * ch <= qpos)
                mb_ref[rows(j), :] = jnp.where(keep, 0.0, NEG_INF)

        def count(pred):
            acc = None
            for j in range(n_chunks):
                f = _fold_rows(jnp.where(pred(sc_ref[rows(j), :]), 1.0, 0.0), jnp.add)
                acc = f if acc is None else acc + f
            return jnp.sum(acc, axis=0, keepdims=True)

        keep_all = functools.partial(mask_at_least, jnp.full((1, tq), NEG_INF, F32))

        def select_and_mask():
            thr = _kth_largest(sc_ref, sb_ref, rows, n_chunks, tq, float(n_keep))
            tied = jnp.max(count(lambda v: v >= thr)) > float(n_keep)
            pl.when(jnp.logical_not(tied))(functools.partial(mask_at_least, thr))

            @pl.when(tied)
            def _():
                need = float(n_keep) - count(lambda v: v > thr)
                tri = jnp.where(lax.broadcasted_iota(jnp.int32, (ch, ch), 1)
                                <= lax.broadcasted_iota(jnp.int32, (ch, ch), 0), 1.0, 0.0).astype(BF16)

                def chunk(j, seen):
                    at = pl.ds(pl.multiple_of(j * ch, ch), ch)
                    sc = sc_ref[at, :]
                    eq = jnp.where(sc == thr, 1.0, 0.0)
                    rank = _mm(tri, eq.astype(BF16)) + seen
                    keep = (sc > thr) | ((sc == thr) & (rank <= need))
                    keep = keep & (krow + j * ch <= qpos)
                    mb_ref[at, :] = jnp.where(keep, 0.0, NEG_INF)
                    return seen + jnp.sum(eq, axis=0, keepdims=True)

                lax.fori_loop(0, n_chunks, chunk, jnp.zeros((1, tq), F32))

        if n_chunks * ch <= n_keep:
            keep_all()
        elif lo * tq < n_keep:
            few = q0 + tq <= n_keep
            pl.when(few)(keep_all)
            pl.when(jnp.logical_not(few))(select_and_mask)
        else:
            select_and_mask()

        q = q_ref[0]
        q0f = q0.astype(F32)
        outs = {}

        def unit(h):
            hm = (lane_h >= HEAD_DIM * h) & (lane_h < HEAD_DIM * (h + 1))
            qh = jnp.where(hm, q, jnp.zeros_like(q))

            def score(j):
                kb = slopes[h] * (kpos_ref[rows(j), :] - q0f)
                kb = jnp.concatenate([kb] * (tq // LANES), axis=1)
                return _nt(k_ref[0, rows(j), :], qh) + kb + mb_ref[rows(j), :]

            def done(out):
                outs[h] = out
                if h == DSA_HEADS - 1:
                    full = jnp.concatenate([outs[hh] for hh in range(DSA_HEADS)], axis=0)
                    o_ref[0] = full.T.astype(BF16)

            vt = lambda j: vt_ref[0, HEAD_DIM * h:HEAD_DIM * (h + 1), rows(j)]
            return _Unit(s_ref.at[h], n_chunks, ch, tq, score, vt, done).steps

        _interleave([functools.partial(unit, h) for h in range(DSA_HEADS)])

    for lo, hi in groups:
        pl.when((i >= lo) & (i < hi))(functools.partial(body, lo, hi))


def _dsa_attention(qc, kc, vct, iq, ik, iwt, slopes, tq, sizes):
    B, S, hw = qc.shape
    nq = S // tq
    assert sum(sizes) == nq
    groups = tuple((sum(sizes[:n]), sum(sizes[:n + 1])) for n in range(len(sizes)))
    ch = tq
    n_keep = min(DSA_TOPK, S // 4)
    kpos = jnp.broadcast_to(jnp.arange(S, dtype=F32)[:, None], (S, LANES))
    full = lambda w: pl.BlockSpec((1, S, w), lambda b, i: (b, 0, 0))
    tile = lambda w: pl.BlockSpec((1, tq, w), lambda b, i: (b, i, 0))
    return pl.pallas_call(
        functools.partial(_dsa_kernel, tq=tq, ch=ch, groups=groups, n_keep=n_keep, slopes=slopes,
                          idx_scale=(IDX_DIM * IDX_HEADS) ** -0.5),
        grid=(B, nq),
        in_specs=[tile(hw), full(hw), pl.BlockSpec((1, hw, S), lambda b, i: (b, 0, 0)),
                  tile(IDX_HEADS * IDX_DIM), full(LANES), pl.BlockSpec((1, LANES, tq), lambda b, i: (b, 0, i)),
                  pl.BlockSpec((S, LANES), lambda b, i: (0, 0))],
        out_specs=tile(hw),
        out_shape=jax.ShapeDtypeStruct((B, S, hw), BF16),
        scratch_shapes=[pltpu.VMEM((S, tq), F32), pltpu.VMEM((S, tq), BF16), pltpu.VMEM((S, tq), F32),
                        pltpu.VMEM((DSA_HEADS, S, tq), F32)],
        compiler_params=_params("arbitrary", "arbitrary"),
        name="dsa_attn",
    )(qc, kc, vct, iq, ik, iwt, kpos)


def _merge_kernel(ya_ref, yb_ref, yc_ref, gt_ref, x_ref, mod_ref, wa_ref, wb_ref, wc_ref, wo_ref, o_ref):
    d = D_MODEL
    merged = gt_ref[0, :, 0:d].astype(F32) * _mm(ya_ref[0], wa_ref[0])
    merged = merged + gt_ref[0, :, d:2 * d].astype(F32) * _mm(yb_ref[0], wb_ref[0])
    merged = merged + gt_ref[0, :, 2 * d:3 * d].astype(F32) * _mm(yc_ref[0], wc_ref[0])
    o_ref[0] = x_ref[0] + mod_ref[0, 2:3, :] * _mm(merged.astype(BF16), wo_ref[0])


def _merge(ya, yb, yc, gt, x, mod, lw, l, tm):
    B, S, D = x.shape
    row = lambda w: pl.BlockSpec((1, tm, w), lambda b, i: (b, i, 0))
    consts = [lw["w_br_a"], lw["w_br_b"], lw["w_br_c"], lw["w_out"]]
    return pl.pallas_call(
        _merge_kernel,
        grid=(B, S // tm),
        in_specs=[row(ya.shape[-1]), row(yb.shape[-1]), row(yc.shape[-1]), row(3 * D), row(D),
                  pl.BlockSpec((1, 6, D), lambda b, i: (b, 0, 0))] + [_layer_spec(c.shape, l) for c in consts],
        out_specs=row(D),
        out_shape=jax.ShapeDtypeStruct((B, S, D), F32),
        compiler_params=_params("arbitrary", "arbitrary"),
        name="merge_out",
    )(ya, yb, yc, gt, x, mod, *consts)


def _ffn_kernel(x_ref, mod_ref, n2_ref, wg_ref, wu_ref, wd_ref, o_ref):
    x = x_ref[0]
    h = (_rms(x, n2_ref[0]) * (1.0 + mod_ref[0, 4:5, :]) + mod_ref[0, 3:4, :]).astype(BF16)
    g = _mm(h, wg_ref[0])
    u = _mm(h, wu_ref[0])
    act = (g * jax.nn.sigmoid(g) * u).astype(BF16)
    o_ref[0] = x + mod_ref[0, 5:6, :] * _mm(act, wd_ref[0])


def _ffn(x, mod, lw, l, tm):
    B, S, D = x.shape
    row = pl.BlockSpec((1, tm, D), lambda b, i: (b, i, 0))
    gu = lw["w_gu"]
    half = (gu.shape[0], gu.shape[1], gu.shape[2] // 2)
    consts = [lw["n2"], gu, gu, lw["w_d"]]
    return pl.pallas_call(
        _ffn_kernel,
        grid=(B, S // tm),
        in_specs=[row, pl.BlockSpec((1, 6, D), lambda b, i: (b, 0, 0)), _layer_spec(lw["n2"].shape, l),
                  _layer_spec(half, l, 0), _layer_spec(half, l, 1), _layer_spec(lw["w_d"].shape, l)],
        out_specs=row,
        out_shape=jax.ShapeDtypeStruct((B, S, D), F32),
        compiler_params=_params("arbitrary", "arbitrary"),
        name="swiglu",
    )(x, mod, *consts)


def _pack_weights(norm1, w_in, g_cq, w_uq, g_ckv, w_ukv, qn_mla, kn_mla, qn_moba, kn_moba, qn_dsa, kn_dsa,
                  w_br_a, w_br_b, w_br_c, w_out, norm2, w_gu, w_down):
    L = w_in.shape[0]
    o = 0
    cols = {}
    for name, wd in (("cq", MLA_Q_RANK), ("ckv", MLA_KV_RANK), ("kr", MLA_ROPE),
                     ("b", 3 * MOBA_HEADS * HEAD_DIM), ("c", 3 * DSA_HEADS * HEAD_DIM),
                     ("iq", IDX_HEADS * IDX_DIM), ("ik", IDX_DIM), ("iw", IDX_HEADS), ("g", 3 * D_MODEL)):
        cols[name] = w_in[:, :, o:o + wd]
        o += wd
    pad_last = lambda v, lo, hi: jnp.pad(v, ((0, 0),) * (v.ndim - 1) + ((lo, hi),))
    kr_slot = pad_last(cols["kr"], MLA_NOPE, LANES - MLA_QK)
    ik_rep = jnp.tile(cols["ik"], (1, 1, LANES // IDX_DIM))
    hw = MOBA_HEADS * HEAD_DIM
    w_all = jnp.concatenate([cols["cq"], cols["ckv"], kr_slot, cols["b"][:, :, :2 * hw], cols["c"][:, :, :2 * hw],
                             cols["iq"], ik_rep, cols["g"]], axis=2).astype(BF16)
    assert w_all.shape[-1] == N_ALL
    w_t = jnp.concatenate([cols["b"][:, :, 2 * hw:], cols["c"][:, :, 2 * hw:],
                           pad_last(cols["iw"], 0, LANES - IDX_HEADS)], axis=2)
    wuq = pad_last(w_uq.reshape(L, MLA_Q_RANK, MLA_HEADS, MLA_QK), 0, LANES - MLA_QK)
    wukv = w_ukv.reshape(L, MLA_KV_RANK, MLA_HEADS, MLA_NOPE + MLA_V)
    wk = pad_last(wukv[..., :MLA_NOPE], 0, LANES - MLA_NOPE)
    wv = wukv[..., MLA_NOPE:].reshape(L, MLA_KV_RANK, MLA_HEADS * MLA_V)
    row = lambda v: v.reshape(L, 1, -1)
    t = lambda v: jnp.swapaxes(v, 1, 2)
    return {
        "n1": row(norm1), "w_all": w_all, "w_t": t(w_t).astype(BF16), "g_cq": row(g_cq),
        "w_uq": wuq.reshape(L, MLA_Q_RANK, MLA_HEADS * LANES).astype(BF16), "g_ckv": row(g_ckv),
        "w_k": wk.reshape(L, MLA_KV_RANK, MLA_HEADS * LANES).astype(BF16), "w_v": t(wv).astype(BF16),
        "qn_a": row(pad_last(qn_mla * MLA_QK ** -0.5, 0, LANES - MLA_QK)),
        "kn_a": row(pad_last(kn_mla, 0, LANES - MLA_QK)),
        "qn_b": row(jnp.tile(qn_moba * HEAD_DIM ** -0.5, (1, MOBA_HEADS))), "kn_b": row(jnp.tile(kn_moba, (1, MOBA_HEADS))),
        "qn_c": row(jnp.tile(qn_dsa * HEAD_DIM ** -0.5, (1, DSA_HEADS))), "kn_c": row(jnp.tile(kn_dsa, (1, DSA_HEADS))),
        "w_br_a": w_br_a.astype(BF16), "w_br_b": w_br_b.astype(BF16), "w_br_c": w_br_c.astype(BF16),
        "w_out": w_out.astype(BF16), "n2": row(norm2), "w_gu": w_gu.astype(BF16), "w_d": w_down.astype(BF16),
    }


def _rope_tables(S):
    half = MLA_ROPE // 2
    freqs = ROPE_THETA ** (-jnp.arange(half, dtype=F32) / half)
    ang = jnp.arange(S, dtype=F32)[:, None] * freqs[None, :]
    cos, sin = jnp.cos(ang), jnp.sin(ang)
    zeros = lambda n: jnp.zeros((S, n), F32)
    tail = LANES - MLA_QK
    rc = jnp.concatenate([jnp.ones((S, MLA_NOPE), F32), cos, cos, jnp.ones((S, tail), F32)], axis=1)
    rs1 = jnp.concatenate([zeros(MLA_NOPE), -sin, zeros(half + tail)], axis=1)
    rs2 = jnp.concatenate([zeros(MLA_NOPE + half), sin, zeros(tail)], axis=1)
    return rc, rs1, rs2


def _dsa_groups(nq):
    single = min(4, nq)
    rest = nq - single
    return (1,) * single + (2,) * (rest // 2) + (1,) * (rest % 2)


def _tiles(S):
    dsa_q = min(256, S)
    return {"proj": min(256, S), "out": min(512, S), "mla_q": min(512, S), "dsa_q": dsa_q,
            "dsa_groups": _dsa_groups(S // dsa_q)}


def kernel(x, c, w_ada, b_ada, norm1, w_in, g_cq, w_uq, g_ckv, w_ukv, qn_mla, kn_mla, qn_moba, kn_moba, qn_dsa, kn_dsa, w_br_a, w_br_b, w_br_c, w_out, norm2, w_gu, w_down):
    B, S, D = x.shape
    L = w_ada.shape[0]
    t = _tiles(S)
    assert D == D_MODEL and S % MOBA_BLOCK == 0 and all(S % t[k] == 0 for k in ("proj", "out", "mla_q", "dsa_q"))
    n_slopes = MOBA_HEADS + DSA_HEADS
    slopes = [2.0 ** (-8.0 * (i + 1) / n_slopes) for i in range(n_slopes)]
    mod_all = _ada(c, w_ada, b_ada).reshape(L, B, 6, D)
    rope_tabs = _rope_tables(S)
    seg = jnp.arange(MOBA_HEADS * HEAD_DIM) // HEAD_DIM
    bd = (seg[:, None] == seg[None, :]).astype(BF16)
    lw = _pack_weights(norm1, w_in, g_cq, w_uq, g_ckv, w_ukv, qn_mla, kn_mla, qn_moba, kn_moba, qn_dsa, kn_dsa,
                       w_br_a, w_br_b, w_br_c, w_out, norm2, w_gu, w_down)
    for l in range(L):
        mod = mod_all[l]
        qa, ka, va, qb, kb, vb, qc, kc, vc, iq, ik, iw, gt = _inproj(x, mod, lw, l, rope_tabs, bd, t["proj"])
        ya = _mla_attention(qa, ka, va, tq=t["mla_q"])
        yb = _moba_attention(qb, kb, vb, tuple(slopes[0::2]))
        yc = _dsa_attention(qc, kc, vc, iq, ik, iw, tuple(slopes[1::2]), tq=t["dsa_q"], sizes=t["dsa_groups"])
        x = _merge(ya, yb, yc, gt, x, mod, lw, l, t["out"])
        x = _ffn(x, mod, lw, l, t["out"])
    return x
```

```python
import functools

import jax
import jax.numpy as jnp
from jax import lax
from jax.experimental import pallas as pl
from jax.experimental.pallas import tpu as pltpu

F32 = jnp.float32
BF16 = jnp.bfloat16

D_MODEL = 1024
HEAD_DIM = 64
RMS_EPS = 1e-6
MLA_HEADS = 8
MLA_NOPE = 64
MLA_ROPE = 32
MLA_QK = MLA_NOPE + MLA_ROPE
MLA_V = 64
MLA_Q_RANK = 768
MLA_KV_RANK = 256
ROPE_THETA = 10000.0
MOBA_HEADS = 4
MOBA_BLOCK = 256
MOBA_TOPK = 3
DSA_HEADS = 4
IDX_HEADS = 16
IDX_DIM = 32
DSA_TOPK = 256

LANES = 128
VMEM_LIMIT = 56 * 1024 * 1024

OFF_CQ = 0
OFF_CKV = OFF_CQ + MLA_Q_RANK
OFF_KR = OFF_CKV + MLA_KV_RANK
OFF_B = OFF_KR + LANES
OFF_C = OFF_B + 2 * MOBA_HEADS * HEAD_DIM
OFF_IQ = OFF_C + 2 * DSA_HEADS * HEAD_DIM
OFF_IK = OFF_IQ + IDX_HEADS * IDX_DIM
OFF_G = OFF_IK + LANES
N_ALL = OFF_G + 3 * D_MODEL
ROW_VB = 0
ROW_VC = ROW_VB + MOBA_HEADS * HEAD_DIM
ROW_IW = ROW_VC + DSA_HEADS * HEAD_DIM
N_T = ROW_IW + LANES


def _nt(a, b):
    return lax.dot_general(a, b, (((1,), (1,)), ((), ())), preferred_element_type=F32)


def _mm(a, b):
    return jnp.dot(a, b, preferred_element_type=F32)


def _split3(x):
    hi = x.astype(BF16)
    r = x - hi.astype(F32)
    mid = r.astype(BF16)
    lo = (r - mid.astype(F32)).astype(BF16)
    return hi, mid, lo


def _rms(x, g):
    return x * lax.rsqrt(jnp.mean(x * x, axis=-1, keepdims=True) + RMS_EPS) * g


def _const_spec(shape):
    nd = len(shape)
    return pl.BlockSpec(shape, lambda *_: (0,) * nd, pipeline_mode=pl.Buffered(1))


def _layer_spec(shape, l, part=0):
    rest = (0,) * (len(shape) - 2) + (part,)
    return pl.BlockSpec((1,) + tuple(shape[1:]), lambda *_: (l,) + rest, pipeline_mode=pl.Buffered(1))


def _params(*sem):
    return pltpu.CompilerParams(dimension_semantics=sem, vmem_limit_bytes=VMEM_LIMIT)


def _ada_kernel(c_ref, w_ref, b_ref, o_ref):
    c = c_ref[...]
    cond = (c * jax.nn.sigmoid(c)).astype(BF16)
    o_ref[0] = _mm(cond, w_ref[0].astype(BF16)) + b_ref[0]


def _ada(c, w_ada, b_ada):
    L, D, N = w_ada.shape
    B = c.shape[0]
    tn = 1024
    return pl.pallas_call(
        _ada_kernel,
        grid=(L, N // tn),
        in_specs=[
            pl.BlockSpec((B, D), lambda l, n: (0, 0)),
            pl.BlockSpec((1, D, tn), lambda l, n: (l, 0, n)),
            pl.BlockSpec((1, 1, tn), lambda l, n: (l, 0, n)),
        ],
        out_specs=pl.BlockSpec((1, B, tn), lambda l, n: (l, 0, n)),
        out_shape=jax.ShapeDtypeStruct((L, B, N), F32),
        compiler_params=_params("arbitrary", "arbitrary"),
        name="ada_mod",
    )(c, w_ada, b_ada.reshape(L, 1, N))


def _rope(x, c, s1, s2):
    return x * c + pltpu.roll(x, LANES - 16, 1) * s1 + pltpu.roll(x, 16, 1) * s2


def _inproj_kernel(x_ref, mod_ref, n1_ref, w_ref, wt_ref, gcq_ref, wuq_ref, gckv_ref, wk_ref, wv_ref,
                   qna_ref, kna_ref, rc_ref, rs1_ref, rs2_ref, bd_ref,
                   qnb_ref, knb_ref, qnc_ref, knc_ref,
                   qa_ref, ka_ref, va_ref, qb_ref, kb_ref, vb_ref, qc_ref, kc_ref, vc_ref,
                   iq_ref, ik_ref, iw_ref, gt_ref):
    x = x_ref[0]
    sh1 = mod_ref[0, 0:1, :]
    sc1 = mod_ref[0, 1:2, :]
    hb = (_rms(x, n1_ref[0]) * (1.0 + sc1) + sh1).astype(BF16)

    a = _mm(hb, w_ref[0, :, OFF_CQ:OFF_B])
    cqn = _rms(a[:, OFF_CQ:OFF_CKV], gcq_ref[0]).astype(BF16)
    ckvn = _rms(a[:, OFF_CKV:OFF_KR], gckv_ref[0]).astype(BF16)
    kr = a[:, OFF_KR:OFF_B]
    qraw = _mm(cqn, wuq_ref[0])
    kraw = _mm(ckvn, wk_ref[0])
    va_ref[0] = _nt(wv_ref[0], ckvn).astype(BF16)
    rc, rs1, rs2 = rc_ref[...], rs1_ref[...], rs2_ref[...]
    qna, kna = qna_ref[0], kna_ref[0]
    inv_qk = 1.0 / MLA_QK
    for h in range(MLA_HEADS):
        sl = slice(LANES * h, LANES * (h + 1))
        qh = qraw[:, sl]
        qh = qh * lax.rsqrt(jnp.sum(qh * qh, axis=-1, keepdims=True) * inv_qk + RMS_EPS) * qna
        qa_ref[0, :, sl] = _rope(qh, rc, rs1, rs2).astype(BF16)
        kh = kraw[:, sl] + kr
        kh = kh * lax.rsqrt(jnp.sum(kh * kh, axis=-1, keepdims=True) * inv_qk + RMS_EPS) * kna
        ka_ref[0, :, sl] = _rope(kh, rc, rs1, rs2).astype(BF16)
        if h in (1, 4, 7):
            g0 = D_MODEL * (h // 3)
            gt_ref[0, :, g0:g0 + D_MODEL] = jax.nn.sigmoid(
                _mm(hb, w_ref[0, :, OFF_G + g0:OFF_G + g0 + D_MODEL])).astype(BF16)

    bd = bd_ref[...]

    def segnorm(v, g):
        ss = sum(_mm(part, bd) for part in _split3(v * v)[:2])
        return v * lax.rsqrt(ss * (1.0 / HEAD_DIM) + RMS_EPS) * g

    hw = MOBA_HEADS * HEAD_DIM
    pb = _mm(hb, w_ref[0, :, OFF_B:OFF_C])
    qb_ref[0] = segnorm(pb[:, 0:hw], qnb_ref[0]).astype(BF16)
    kb_ref[0] = segnorm(pb[:, hw:2 * hw], knb_ref[0]).astype(BF16)
    pc = _mm(hb, w_ref[0, :, OFF_C:OFF_IQ])
    qc_ref[0] = segnorm(pc[:, 0:hw], qnc_ref[0]).astype(BF16)
    kc_ref[0] = segnorm(pc[:, hw:2 * hw], knc_ref[0]).astype(BF16)

    pt = _nt(wt_ref[0], hb)
    vb_ref[0] = pt[ROW_VB:ROW_VC, :].astype(BF16)
    vc_ref[0] = pt[ROW_VC:ROW_IW, :].astype(BF16)
    iw_ref[0] = pt[ROW_IW:N_T, :]

    pi = _mm(hb, w_ref[0, :, OFF_IQ:OFF_G])
    iq_ref[0] = pi[:, 0:OFF_IK - OFF_IQ].astype(BF16)
    ik_ref[0] = pi[:, OFF_IK - OFF_IQ:OFF_G - OFF_IQ].astype(BF16)


def _inproj(x, mod, lw, l, rope_tabs, bd, tm):
    B, S, D = x.shape
    hw = MOBA_HEADS * HEAD_DIM
    row = lambda w: pl.BlockSpec((1, tm, w), lambda b, i: (b, i, 0))
    tab = pl.BlockSpec((tm, LANES), lambda b, i: (i, 0))
    consts = [lw["n1"], lw["w_all"], lw["w_t"], lw["g_cq"], lw["w_uq"], lw["g_ckv"], lw["w_k"], lw["w_v"],
              lw["qn_a"], lw["kn_a"]]
    consts2 = [lw["qn_b"], lw["kn_b"], lw["qn_c"], lw["kn_c"]]
    widths = [MLA_HEADS * LANES, MLA_HEADS * LANES, MLA_HEADS * MLA_V, hw, hw, hw, hw, hw, hw,
              IDX_HEADS * IDX_DIM, LANES, LANES, 3 * D_MODEL]
    dtypes = [BF16] * 11 + [F32, BF16]
    transposed = [False, False, True] * 3 + [False, False, True, False]
    col = lambda w: pl.BlockSpec((1, w, tm), lambda b, i: (b, 0, i))
    return pl.pallas_call(
        _inproj_kernel,
        grid=(B, S // tm),
        in_specs=[row(D), pl.BlockSpec((1, 6, D), lambda b, i: (b, 0, 0))]
        + [_layer_spec(c.shape, l) for c in consts] + [tab, tab, tab, _const_spec(bd.shape)]
        + [_layer_spec(c.shape, l) for c in consts2],
        out_specs=[col(w) if tr else row(w) for w, tr in zip(widths, transposed)],
        out_shape=[jax.ShapeDtypeStruct((B, w, S) if tr else (B, S, w), dt)
                   for w, dt, tr in zip(widths, dtypes, transposed)],
        compiler_params=_params("arbitrary", "arbitrary"),
        name="in_proj",
    )(x, mod, *consts, *rope_tabs, bd, *consts2)


NEG_INF = float("-inf")


def _fold_rows(x, op, n=8):
    out = x[0:n, :]
    for t in range(1, x.shape[0] // n):
        out = op(out, x[n * t:n * (t + 1), :])
    return out


def _interleave(units):
    pending = []
    for make in units:
        first, second = make()
        for t in range(max(len(first), len(pending))):
            if t < len(first):
                first[t]()
            if t < len(pending):
                pending[t]()
        pending = second
    for step in pending:
        step()


class _Unit:
    def __init__(self, logits, n_chunks, ch, tq, score, vt_chunk, done):
        self.m8 = jnp.full((8, tq), NEG_INF, F32)
        self.l8 = None
        self.acc = None

        def first(j):
            s = score(j)
            logits[j * ch:(j + 1) * ch, :] = s
            self.m8 = jnp.maximum(self.m8, _fold_rows(s, jnp.maximum))

        def second(j):
            if j == 0:
                self.m = jnp.max(self.m8, axis=0, keepdims=True)
            p = jnp.exp(logits[j * ch:(j + 1) * ch, :] - self.m)
            f = _fold_rows(p, jnp.add)
            pv = _mm(vt_chunk(j), p.astype(BF16))
            self.l8 = f if self.l8 is None else self.l8 + f
            self.acc = pv if self.acc is None else self.acc + pv
            if j == n_chunks - 1:
                done(self.acc / jnp.sum(self.l8, axis=0, keepdims=True))

        self.steps = ([functools.partial(first, j) for j in range(n_chunks)],
                      [functools.partial(second, j) for j in range(n_chunks)])


def _mla_kernel(q_ref, k_ref, vt_ref, o_ref, s_ref, *, tq, nq):
    krow = lax.broadcasted_iota(jnp.int32, (tq, tq), 0)
    qcol = lax.broadcasted_iota(jnp.int32, (tq, tq), 1)
    causal = krow <= qcol
    outs = {}

    def unit(c, hh, u):
        sl = slice(LANES * hh, LANES * (hh + 1))
        q = q_ref[0, c * tq:(c + 1) * tq, sl]

        def score(j):
            s = _nt(k_ref[0, j * tq:(j + 1) * tq, sl], q)
            return jnp.where(causal, s, NEG_INF) if j == c else s

        def done(out):
            outs[c, hh] = out
            if hh == 1:
                pair = jnp.concatenate([outs[c, 0], outs[c, 1]], axis=0)
                o_ref[0, c * tq:(c + 1) * tq, :] = pair.T.astype(BF16)

        vt = lambda j: vt_ref[0, MLA_V * hh:MLA_V * (hh + 1), j * tq:(j + 1) * tq]
        return _Unit(s_ref.at[u % 3], c + 1, tq, tq, score, vt, done).steps

    _interleave([functools.partial(unit, c, hh, 2 * c + hh) for c in range(nq) for hh in range(2)])


def _mla_attention(qa, ka, vat, tq):
    B, S, _ = qa.shape
    pairs = MLA_HEADS // 2
    nq = S // tq
    wide = pl.BlockSpec((1, S, 2 * LANES), lambda b, h: (b, 0, h))
    return pl.pallas_call(
        functools.partial(_mla_kernel, tq=tq, nq=nq),
        grid=(B, pairs),
        in_specs=[wide, wide, pl.BlockSpec((1, 2 * MLA_V, S), lambda b, h: (b, h, 0))],
        out_specs=pl.BlockSpec((1, S, 2 * MLA_V), lambda b, h: (b, 0, h)),
        out_shape=jax.ShapeDtypeStruct((B, S, MLA_HEADS * MLA_V), BF16),
        scratch_shapes=[pltpu.VMEM((3, S, tq), F32)],
        compiler_params=_params("arbitrary", "arbitrary"),
        name="mla_attn",
    )(qa, ka, vat)


def _moba_kernel(q_ref, k_ref, vt_ref, o_ref, s_ref, *, nb, nbp, n_sel, slopes):
    blk = MOBA_BLOCK
    hw = MOBA_HEADS * HEAD_DIM
    lane = lax.broadcasted_iota(jnp.int32, (1, hw), 1)

    means = [jnp.mean(k_ref[0, n * blk:(n + 1) * blk, :].astype(F32), axis=0, keepdims=True)
             for n in range(nb)]
    km = jnp.concatenate(means + [jnp.zeros((1, hw), F32)] * (nbp - nb), axis=0)
    pieces = []
    for h in range(MOBA_HEADS):
        hm = (lane >= HEAD_DIM * h) & (lane < HEAD_DIM * (h + 1))
        pieces += [p.astype(F32) for p in _split3(jnp.where(hm, km, 0.0))]
    km_stack = jnp.concatenate(pieces, axis=0).astype(BF16)

    krow = lax.broadcasted_iota(jnp.int32, (blk, blk), 0)
    qcol = lax.broadcasted_iota(jnp.int32, (blk, blk), 1)
    causal = krow <= qcol
    krow_f = krow.astype(F32)
    bidx = lax.broadcasted_iota(jnp.int32, (nbp, blk), 0)

    gates = {}
    outs = {}

    def unit(c, h, u):
        q = q_ref[0, c * blk:(c + 1) * blk, :]
        hm = (lane >= HEAD_DIM * h) & (lane < HEAD_DIM * (h + 1))
        qh = jnp.where(hm, q, jnp.zeros_like(q))
        kbias = slopes[h] * krow_f
        if c > 0:
            if h == 0:
                gates[c] = _nt(km_stack, q)
            past = bidx < c
            g = sum(gates[c][(3 * h + p) * nbp:(3 * h + p + 1) * nbp, :] for p in range(3))
            g = jnp.where(past, g, NEG_INF)
            rank = jnp.zeros((nbp, blk), F32)
            for n2 in range(c):
                gm = g[n2:n2 + 1, :]
                beats = (gm > g) | ((gm == g) & (bidx > n2))
                rank = rank + jnp.where(beats, 1.0, 0.0)
            drop = jnp.where(past & (rank < n_sel), 0.0, NEG_INF)

        def score(j):
            s = _nt(k_ref[0, j * blk:(j + 1) * blk, :], qh) + kbias
            if j == c:
                return jnp.where(causal, s, NEG_INF)
            return s + (drop[j:j + 1, :] + slopes[h] * float((j - c) * blk))

        def done(out):
            outs[c, h] = out
            if h == MOBA_HEADS - 1:
                full = jnp.concatenate([outs[c, hh] for hh in range(MOBA_HEADS)], axis=0)
                o_ref[0, c * blk:(c + 1) * blk, :] = full.T.astype(BF16)

        vt = lambda j: vt_ref[0, HEAD_DIM * h:HEAD_DIM * (h + 1), j * blk:(j + 1) * blk]
        return _Unit(s_ref.at[u % 3], c + 1, blk, blk, score, vt, done).steps

    _interleave([functools.partial(unit, c, h, MOBA_HEADS * c + h) for c in range(nb) for h in range(MOBA_HEADS)])


def _moba_attention(qb, kb, vbt, slopes):
    B, S, hw = qb.shape
    blk = MOBA_BLOCK
    nb = S // blk
    nbp = -(-nb // 8) * 8
    n_sel = max(1, min(MOBA_TOPK, nb - 1))
    full = pl.BlockSpec((1, S, hw), lambda b: (b, 0, 0))
    return pl.pallas_call(
        functools.partial(_moba_kernel, nb=nb, nbp=nbp, n_sel=n_sel, slopes=slopes),
        grid=(B,),
        in_specs=[full, full, pl.BlockSpec((1, hw, S), lambda b: (b, 0, 0))],
        out_specs=full,
        out_shape=jax.ShapeDtypeStruct((B, S, hw), BF16),
        scratch_shapes=[pltpu.VMEM((3, S, blk), F32)],
        compiler_params=_params("arbitrary"),
        name="moba_attn",
    )(qb, kb, vbt)


KEY_POS_INF = 0x7F800000
KEY_NEG_INF = -0x7F800001


def _key_to_float(key):
    key = jnp.clip(key, KEY_NEG_INF, KEY_POS_INF)
    return lax.bitcast_convert_type(jnp.where(key >= 0, key, key ^ jnp.int32(0x7FFFFFFF)), F32)


def _kth_largest(sc_ref, sb_ref, rows, n_chunks, tq, k):
    i32 = jnp.int32

    def count(ref, c, one, zero, n):
        acc = None
        for j in range(n_chunks):
            f = _fold_rows(jnp.where(ref[rows(j), :] >= c, one, zero), jnp.add, n).astype(F32)
            acc = f if acc is None else acc + f
        return jnp.sum(acc, axis=0, keepdims=True)

    count16 = lambda c: count(sb_ref, c.astype(BF16), jnp.bfloat16(1), jnp.bfloat16(0), 16)
    count32 = lambda c: count(sc_ref, c, 1.0, 0.0, 8)
    grid_key = lambda k16: lax.shift_left(k16, 16) | jnp.where(k16 < 0, i32(0xFFFF), i32(0))

    zero = jnp.zeros((1, tq), i32)
    k16 = jnp.where(count16(_key_to_float(grid_key(zero))) >= k, zero, i32(-2 ** 15))

    def coarse(it, k16):
        cand = k16 | lax.shift_left(i32(1), i32(14) - it)
        return jnp.where(count16(_key_to_float(grid_key(cand))) >= k, cand, k16)

    g = grid_key(lax.fori_loop(0, 15, coarse, k16))
    state = (g - i32(0x8000), g + i32(0x10001), jnp.full((1, tq), jnp.inf, F32))

    def fine(_, state):
        lo, hi, n_lo = state
        mid = lo + lax.shift_right_arithmetic(hi - lo, 1)
        n_mid = count32(_key_to_float(mid))
        ok = n_mid >= k
        return jnp.where(ok, mid, lo), jnp.where(ok, hi, mid), jnp.where(ok, n_mid, n_lo)

    lo, _, n_lo = lax.fori_loop(0, 17, fine, state)
    return _key_to_float(lo), n_lo


def _dsa_kernel(q_ref, k_ref, vt_ref, iq_ref, ik_ref, iwt_ref, kpos_ref, o_ref, sc_ref, sb_ref, mb_ref, s_ref, *,
                tq, ch, groups, n_keep, slopes, idx_scale):
    i = pl.program_id(1)
    hw = DSA_HEADS * HEAD_DIM
    q0 = i * tq
    qpos = q0 + lax.broadcasted_iota(jnp.int32, (1, tq), 1)
    krow = lax.broadcasted_iota(jnp.int32, (ch, 1), 0)
    lane = lax.broadcasted_iota(jnp.int32, (1, LANES), 1)
    lane_h = lax.broadcasted_iota(jnp.int32, (1, hw), 1)
    grp = LANES // IDX_DIM

    def body(lo, hi):
        n_chunks = hi * tq // ch
        rows = lambda j: slice(j * ch, (j + 1) * ch)
        iwt = iwt_ref[0] * idx_scale
        past_chunks = lo * tq // ch
        qms = []
        for hh in range(IDX_HEADS):
            g, r = divmod(hh, grp)
            iqg = iq_ref[0, :, LANES * g:LANES * (g + 1)]
            qms.append(jnp.where((lane >= IDX_DIM * r) & (lane < IDX_DIM * (r + 1)), iqg, jnp.zeros_like(iqg)))
        for j in range(n_chunks):
            ikc = ik_ref[0, rows(j), :]
            score = None
            for hh in range(IDX_HEADS):
                term = jnp.maximum(_nt(ikc, qms[hh]), 0.0) * iwt[hh:hh + 1, :]
                score = term if score is None else score + term
            if j >= past_chunks:
                score = jnp.where(krow + j * ch <= qpos, score, NEG_INF)
            sc_ref[rows(j), :] = score
            sb_ref[rows(j), :] = score.astype(BF16)

        def mask_at_least(thr):
            for j in range(n_chunks):
                keep = sc_ref[rows(j), :] >= thr
                if j >= past_chunks:
                    keep = keep & (krow + j * ch <= qpos)
                mb_ref[rows(j), :] = jnp.where(keep, 0.0, NEG_INF)

        def count(pred):
            acc = None
            for j in range(n_chunks):
                f = _fold_rows(jnp.where(pred(sc_ref[rows(j), :]), 1.0, 0.0), jnp.add)
                acc = f if acc is None else acc + f
            return jnp.sum(acc, axis=0, keepdims=True)

        keep_all = functools.partial(mask_at_least, jnp.full((1, tq), NEG_INF, F32))

        def select_and_mask():
            thr, n_ge = _kth_largest(sc_ref, sb_ref, rows, n_chunks, tq, float(n_keep))
            tied = jnp.max(n_ge) > float(n_keep)
            pl.when(jnp.logical_not(tied))(functools.partial(mask_at_least, thr))

            @pl.when(tied)
            def _():
                need = float(n_keep) - count(lambda v: v > thr)
                tri = jnp.where(lax.broadcasted_iota(jnp.int32, (ch, ch), 1)
                                <= lax.broadcasted_iota(jnp.int32, (ch, ch), 0), 1.0, 0.0).astype(BF16)

                def chunk(j, seen):
                    at = pl.ds(pl.multiple_of(j * ch, ch), ch)
                    sc = sc_ref[at, :]
                    eq = jnp.where(sc == thr, 1.0, 0.0)
                    rank = _mm(tri, eq.astype(BF16)) + seen
                    keep = (sc > thr) | ((sc == thr) & (rank <= need))
                    keep = keep & (krow + j * ch <= qpos)
                    mb_ref[at, :] = jnp.where(keep, 0.0, NEG_INF)
                    return seen + jnp.sum(eq, axis=0, keepdims=True)

                lax.fori_loop(0, n_chunks, chunk, jnp.zeros((1, tq), F32))

        if n_chunks * ch <= n_keep:
            keep_all()
        elif lo * tq < n_keep:
            few = q0 + tq <= n_keep
            pl.when(few)(keep_all)
            pl.when(jnp.logical_not(few))(select_and_mask)
        else:
            select_and_mask()

        q = q_ref[0]
        q0f = q0.astype(F32)
        outs = {}

        def unit(h):
            hm = (lane_h >= HEAD_DIM * h) & (lane_h < HEAD_DIM * (h + 1))
            qh = jnp.where(hm, q, jnp.zeros_like(q))

            def score(j):
                kb = slopes[h] * (kpos_ref[rows(j), :] - q0f)
                kb = jnp.concatenate([kb] * (tq // LANES), axis=1)
                return _nt(k_ref[0, rows(j), :], qh) + kb + mb_ref[rows(j), :]

            def done(out):
                outs[h] = out
                if h == DSA_HEADS - 1:
                    full = jnp.concatenate([outs[hh] for hh in range(DSA_HEADS)], axis=0)
                    o_ref[0] = full.T.astype(BF16)

            vt = lambda j: vt_ref[0, HEAD_DIM * h:HEAD_DIM * (h + 1), rows(j)]
            return _Unit(s_ref.at[h], n_chunks, ch, tq, score, vt, done).steps

        _interleave([functools.partial(unit, h) for h in range(DSA_HEADS)])

    for lo, hi in groups:
        pl.when((i >= lo) & (i < hi))(functools.partial(body, lo, hi))


def _dsa_attention(qc, kc, vct, iq, ik, iwt, slopes, tq, sizes):
    B, S, hw = qc.shape
    nq = S // tq
    assert sum(sizes) == nq
    groups = tuple((sum(sizes[:n]), sum(sizes[:n + 1])) for n in range(len(sizes)))
    ch = tq
    n_keep = min(DSA_TOPK, S // 4)
    kpos = jnp.broadcast_to(jnp.arange(S, dtype=F32)[:, None], (S, LANES))
    full = lambda w: pl.BlockSpec((1, S, w), lambda b, i: (b, 0, 0))
    tile = lambda w: pl.BlockSpec((1, tq, w), lambda b, i: (b, i, 0))
    return pl.pallas_call(
        functools.partial(_dsa_kernel, tq=tq, ch=ch, groups=groups, n_keep=n_keep, slopes=slopes,
                          idx_scale=(IDX_DIM * IDX_HEADS) ** -0.5),
        grid=(B, nq),
        in_specs=[tile(hw), full(hw), pl.BlockSpec((1, hw, S), lambda b, i: (b, 0, 0)),
                  tile(IDX_HEADS * IDX_DIM), full(LANES), pl.BlockSpec((1, LANES, tq), lambda b, i: (b, 0, i)),
                  pl.BlockSpec((S, LANES), lambda b, i: (0, 0))],
        out_specs=tile(hw),
        out_shape=jax.ShapeDtypeStruct((B, S, hw), BF16),
        scratch_shapes=[pltpu.VMEM((S, tq), F32), pltpu.VMEM((S, tq), BF16), pltpu.VMEM((S, tq), F32),
                        pltpu.VMEM((DSA_HEADS, S, tq), F32)],
        compiler_params=_params("arbitrary", "arbitrary"),
        name="dsa_attn",
    )(qc, kc, vct, iq, ik, iwt, kpos)


def _merge_kernel(ya_ref, yb_ref, yc_ref, gt_ref, x_ref, mod_ref, wa_ref, wb_ref, wc_ref, wo_ref, o_ref):
    d = D_MODEL
    merged = gt_ref[0, :, 0:d].astype(F32) * _mm(ya_ref[0], wa_ref[0])
    merged = merged + gt_ref[0, :, d:2 * d].astype(F32) * _mm(yb_ref[0], wb_ref[0])
    merged = merged + gt_ref[0, :, 2 * d:3 * d].astype(F32) * _mm(yc_ref[0], wc_ref[0])
    o_ref[0] = x_ref[0] + mod_ref[0, 2:3, :] * _mm(merged.astype(BF16), wo_ref[0])


def _merge(ya, yb, yc, gt, x, mod, lw, l, tm):
    B, S, D = x.shape
    row = lambda w: pl.BlockSpec((1, tm, w), lambda b, i: (b, i, 0))
    consts = [lw["w_br_a"], lw["w_br_b"], lw["w_br_c"], lw["w_out"]]
    return pl.pallas_call(
        _merge_kernel,
        grid=(B, S // tm),
        in_specs=[row(ya.shape[-1]), row(yb.shape[-1]), row(yc.shape[-1]), row(3 * D), row(D),
                  pl.BlockSpec((1, 6, D), lambda b, i: (b, 0, 0))] + [_layer_spec(c.shape, l) for c in consts],
        out_specs=row(D),
        out_shape=jax.ShapeDtypeStruct((B, S, D), F32),
        compiler_params=_params("arbitrary", "arbitrary"),
        name="merge_out",
    )(ya, yb, yc, gt, x, mod, *consts)


def _ffn_kernel(x_ref, mod_ref, n2_ref, wg_ref, wu_ref, wd_ref, o_ref):
    x = x_ref[0]
    h = (_rms(x, n2_ref[0]) * (1.0 + mod_ref[0, 4:5, :]) + mod_ref[0, 3:4, :]).astype(BF16)
    g = _mm(h, wg_ref[0])
    u = _mm(h, wu_ref[0])
    act = (g * jax.nn.sigmoid(g) * u).astype(BF16)
    o_ref[0] = x + mod_ref[0, 5:6, :] * _mm(act, wd_ref[0])


def _ffn(x, mod, lw, l, tm):
    B, S, D = x.shape
    row = pl.BlockSpec((1, tm, D), lambda b, i: (b, i, 0))
    gu = lw["w_gu"]
    half = (gu.shape[0], gu.shape[1], gu.shape[2] // 2)
    consts = [lw["n2"], gu, gu, lw["w_d"]]
    return pl.pallas_call(
        _ffn_kernel,
        grid=(B, S // tm),
        in_specs=[row, pl.BlockSpec((1, 6, D), lambda b, i: (b, 0, 0)), _layer_spec(lw["n2"].shape, l),
                  _layer_spec(half, l, 0), _layer_spec(half, l, 1), _layer_spec(lw["w_d"].shape, l)],
        out_specs=row,
        out_shape=jax.ShapeDtypeStruct((B, S, D), F32),
        compiler_params=_params("arbitrary", "arbitrary"),
        name="swiglu",
    )(x, mod, *consts)


def _pack_weights(norm1, w_in, g_cq, w_uq, g_ckv, w_ukv, qn_mla, kn_mla, qn_moba, kn_moba, qn_dsa, kn_dsa,
                  w_br_a, w_br_b, w_br_c, w_out, norm2, w_gu, w_down):
    L = w_in.shape[0]
    o = 0
    cols = {}
    for name, wd in (("cq", MLA_Q_RANK), ("ckv", MLA_KV_RANK), ("kr", MLA_ROPE),
                     ("b", 3 * MOBA_HEADS * HEAD_DIM), ("c", 3 * DSA_HEADS * HEAD_DIM),
                     ("iq", IDX_HEADS * IDX_DIM), ("ik", IDX_DIM), ("iw", IDX_HEADS), ("g", 3 * D_MODEL)):
        cols[name] = w_in[:, :, o:o + wd]
        o += wd
    pad_last = lambda v, lo, hi: jnp.pad(v, ((0, 0),) * (v.ndim - 1) + ((lo, hi),))
    kr_slot = pad_last(cols["kr"], MLA_NOPE, LANES - MLA_QK)
    ik_rep = jnp.tile(cols["ik"], (1, 1, LANES // IDX_DIM))
    hw = MOBA_HEADS * HEAD_DIM
    w_all = jnp.concatenate([cols["cq"], cols["ckv"], kr_slot, cols["b"][:, :, :2 * hw], cols["c"][:, :, :2 * hw],
                             cols["iq"], ik_rep, cols["g"]], axis=2).astype(BF16)
    assert w_all.shape[-1] == N_ALL
    w_t = jnp.concatenate([cols["b"][:, :, 2 * hw:], cols["c"][:, :, 2 * hw:],
                           pad_last(cols["iw"], 0, LANES - IDX_HEADS)], axis=2)
    wuq = pad_last(w_uq.reshape(L, MLA_Q_RANK, MLA_HEADS, MLA_QK), 0, LANES - MLA_QK)
    wukv = w_ukv.reshape(L, MLA_KV_RANK, MLA_HEADS, MLA_NOPE + MLA_V)
    wk = pad_last(wukv[..., :MLA_NOPE], 0, LANES - MLA_NOPE)
    wv = wukv[..., MLA_NOPE:].reshape(L, MLA_KV_RANK, MLA_HEADS * MLA_V)
    row = lambda v: v.reshape(L, 1, -1)
    t = lambda v: jnp.swapaxes(v, 1, 2)
    return {
        "n1": row(norm1), "w_all": w_all, "w_t": t(w_t).astype(BF16), "g_cq": row(g_cq),
        "w_uq": wuq.reshape(L, MLA_Q_RANK, MLA_HEADS * LANES).astype(BF16), "g_ckv": row(g_ckv),
        "w_k": wk.reshape(L, MLA_KV_RANK, MLA_HEADS * LANES).astype(BF16), "w_v": t(wv).astype(BF16),
        "qn_a": row(pad_last(qn_mla * MLA_QK ** -0.5, 0, LANES - MLA_QK)),
        "kn_a": row(pad_last(kn_mla, 0, LANES - MLA_QK)),
        "qn_b": row(jnp.tile(qn_moba * HEAD_DIM ** -0.5, (1, MOBA_HEADS))), "kn_b": row(jnp.tile(kn_moba, (1, MOBA_HEADS))),
        "qn_c": row(jnp.tile(qn_dsa * HEAD_DIM ** -0.5, (1, DSA_HEADS))), "kn_c": row(jnp.tile(kn_dsa, (1, DSA_HEADS))),
        "w_br_a": w_br_a.astype(BF16), "w_br_b": w_br_b.astype(BF16), "w_br_c": w_br_c.astype(BF16),
        "w_out": w_out.astype(BF16), "n2": row(norm2), "w_gu": w_gu.astype(BF16), "w_d": w_down.astype(BF16),
    }


def _rope_tables(S):
    half = MLA_ROPE // 2
    freqs = ROPE_THETA ** (-jnp.arange(half, dtype=F32) / half)
    ang = jnp.arange(S, dtype=F32)[:, None] * freqs[None, :]
    cos, sin = jnp.cos(ang), jnp.sin(ang)
    zeros = lambda n: jnp.zeros((S, n), F32)
    tail = LANES - MLA_QK
    rc = jnp.concatenate([jnp.ones((S, MLA_NOPE), F32), cos, cos, jnp.ones((S, tail), F32)], axis=1)
    rs1 = jnp.concatenate([zeros(MLA_NOPE), -sin, zeros(half + tail)], axis=1)
    rs2 = jnp.concatenate([zeros(MLA_NOPE + half), sin, zeros(tail)], axis=1)
    return rc, rs1, rs2


def _dsa_groups(nq):
    single = min(4, nq)
    rest = nq - single
    return (1,) * single + (2,) * (rest // 2) + (1,) * (rest % 2)


def _tiles(S):
    dsa_q = min(256, S)
    return {"proj": min(256, S), "out": min(512, S), "mla_q": min(512, S), "dsa_q": dsa_q,
            "dsa_groups": _dsa_groups(S // dsa_q)}


def kernel(x, c, w_ada, b_ada, norm1, w_in, g_cq, w_uq, g_ckv, w_ukv, qn_mla, kn_mla, qn_moba, kn_moba, qn_dsa, kn_dsa, w_br_a, w_br_b, w_br_c, w_out, norm2, w_gu, w_down):
    B, S, D = x.shape
    L = w_ada.shape[0]
    t = _tiles(S)
    assert D == D_MODEL and S % MOBA_BLOCK == 0 and all(S % t[k] == 0 for k in ("proj", "out", "mla_q", "dsa_q"))
    n_slopes = MOBA_HEADS + DSA_HEADS
    slopes = [2.0 ** (-8.0 * (i + 1) / n_slopes) for i in range(n_slopes)]
    mod_all = _ada(c, w_ada, b_ada).reshape(L, B, 6, D)
    rope_tabs = _rope_tables(S)
    seg = jnp.arange(MOBA_HEADS * HEAD_DIM) // HEAD_DIM
    bd = (seg[:, None] == seg[None, :]).astype(BF16)
    lw = _pack_weights(norm1, w_in, g_cq, w_uq, g_ckv, w_ukv, qn_mla, kn_mla, qn_moba, kn_moba, qn_dsa, kn_dsa,
                       w_br_a, w_br_b, w_br_c, w_out, norm2, w_gu, w_down)
    for l in range(L):
        mod = mod_all[l]
        qa, ka, va, qb, kb, vb, qc, kc, vc, iq, ik, iw, gt = _inproj(x, mod, lw, l, rope_tabs, bd, t["proj"])
        ya = _mla_attention(qa, ka, va, tq=t["mla_q"])
        yb = _moba_attention(qb, kb, vb, tuple(slopes[0::2]))
        yc = _dsa_attention(qc, kc, vc, iq, ik, iw, tuple(slopes[1::2]), tq=t["dsa_q"], sizes=t["dsa_groups"])
        x = _merge(ya, yb, yc, gt, x, mod, lw, l, t["out"])
        x = _ffn(x, mod, lw, l, t["out"])
    return x
```
